```python
import math
import jax, jax.numpy as jnp
from jax import lax
import numpy as np

D_MODEL = 1024
BATCH = 1
SEQ = 16384
DEPTH = 2
DEC_BATCH = 32
DEC_SEQ = 4
PAST_LEN = 16384
PAGE_SIZE = 128

GLA_HEADS = 4
GLA_DK = D_MODEL // 16
GLA_DV = D_MODEL // 8
GLA_LOWRANK = 16
GLA_GATE_TEMP = 16.0
GLA_CHUNK = 64
RET_HEADS = 4
RET_DK = D_MODEL // 16
RET_DV = D_MODEL // 8
RET_CHUNK = 64
ROPE_BASE = 10000.0
MOBA_HEADS = 8
MOBA_HD = D_MODEL // 16
MOBA_BLOCK = 256
MOBA_TOPK = 3
MOBA_QBLOCK = 128
N_BUCKETS = 32
MAX_DISTANCE = 128
D_FF = 2816
N_EXPERTS = 8
TOP_K = 2
D_FF_EXPERT = 1408

GLA_QK_W = GLA_HEADS * GLA_DK
GLA_V_W = GLA_HEADS * GLA_DV
RET_QK_W = RET_HEADS * RET_DK
RET_V_W = RET_HEADS * RET_DV
MOBA_W = MOBA_HEADS * MOBA_HD
IN_SIZES = (GLA_QK_W, GLA_QK_W, GLA_V_W, GLA_LOWRANK, GLA_V_W,
            RET_QK_W, RET_QK_W, RET_V_W, RET_V_W,
            MOBA_W, MOBA_W, MOBA_W, 3 * D_MODEL)
N_IN = sum(IN_SIZES)
EPS = 1e-6
NEG_INF = -1e30

kernel_name = 'hybrid_gla_retnet_moba_decoder_step'


def rms_norm(x, w):
    xf = x.astype(jnp.float32)
    y = xf * lax.rsqrt(jnp.mean(xf * xf, -1, keepdims=True) + EPS) * w.astype(jnp.float32)
    return y.astype(x.dtype)


def head_rms(x, w):
    xf = x.astype(jnp.float32)
    return xf * lax.rsqrt(jnp.mean(xf * xf, -1, keepdims=True) + EPS) * w.astype(jnp.float32)


def head_layernorm(x, w):
    xf = x.astype(jnp.float32)
    xf = xf - jnp.mean(xf, -1, keepdims=True)
    return xf * lax.rsqrt(jnp.mean(xf * xf, -1, keepdims=True) + EPS) * w.astype(jnp.float32)


def rope(x, pos):
    half = x.shape[-1] // 2
    inv = ROPE_BASE ** (-jnp.arange(half, dtype=jnp.float32) / half)
    ang = pos.astype(jnp.float32)[:, None] * inv[None, :]
    cos = jnp.cos(ang)[None, :, None, :]
    sin = jnp.sin(ang)[None, :, None, :]
    xf = x.astype(jnp.float32)
    x1, x2 = xf[..., :half], xf[..., half:]
    return jnp.concatenate([x1 * cos - x2 * sin, x1 * sin + x2 * cos], -1)


def gla_chunked(q, k, v, log_a, s0):
    B, T, H, _ = q.shape
    DV = v.shape[-1]
    C = min(GLA_CHUNK, T)
    pad = (-T) % C
    n = (T + pad) // C

    def blocks(a):
        a = jnp.pad(a.astype(jnp.float32), ((0, 0), (0, pad), (0, 0), (0, 0)))
        return a.reshape(B, n, C, H, a.shape[-1]).transpose(1, 0, 3, 2, 4)

    causal = jnp.tril(jnp.ones((C, C), dtype=bool))

    def step(S, inp):
        qc, kc, vc, ac = inp
        b = jnp.cumsum(ac, axis=2)
        diff = b[:, :, :, None, :] - b[:, :, None, :, :]
        decay = jnp.exp(jnp.where(causal[:, :, None], diff, -jnp.inf))
        attn = jnp.einsum('bhtd,bhsd,bhtsd->bhts', qc, kc, decay)
        o = (jnp.einsum('bhts,bhsv->bhtv', attn, vc)
             + jnp.einsum('bhtd,bhdv->bhtv', qc * jnp.exp(b), S))
        b_last = b[:, :, -1:, :]
        S = (jnp.exp(b_last[:, :, 0, :])[..., None] * S
             + jnp.einsum('bhsd,bhsv->bhdv', kc * jnp.exp(b_last - b), vc))
        return S, o

    S, o = lax.scan(step, s0.astype(jnp.float32), (blocks(q), blocks(k), blocks(v), blocks(log_a)))
    o = o.transpose(1, 0, 3, 2, 4).reshape(B, n * C, H, DV)[:, :T]
    return o, S


def retention_chunked(q, k, v, log_gamma, s0):
    B, T, H, _ = q.shape
    DV = v.shape[-1]
    C = min(RET_CHUNK, T)
    pad = (-T) % C
    n = (T + pad) // C

    def blocks(a):
        a = jnp.pad(a.astype(jnp.float32), ((0, 0), (0, pad), (0, 0), (0, 0)))
        return a.reshape(B, n, C, H, a.shape[-1]).transpose(1, 0, 3, 2, 4)

    real = (jnp.arange(n * C) < T).astype(jnp.float32)
    ld = (real[:, None] * log_gamma[None, :]).reshape(n, C, H).transpose(0, 2, 1)
    causal = jnp.tril(jnp.ones((C, C), dtype=bool))

    def step(S, inp):
        qc, kc, vc, lc = inp
        b = jnp.cumsum(lc, axis=-1)
        decay = jnp.exp(jnp.where(causal, b[:, :, None] - b[:, None, :], -jnp.inf))
        scores = jnp.einsum('bhtd,bhsd->bhts', qc, kc) * decay[None]
        o = (jnp.einsum('bhts,bhsv->bhtv', scores, vc)
             + jnp.einsum('bhtd,bhdv->bhtv', qc, S) * jnp.exp(b)[None, :, :, None])
        b_last = b[:, -1]
        S = (jnp.exp(b_last)[None, :, None, None] * S
             + jnp.einsum('bhsd,bhsv->bhdv', kc * jnp.exp(b_last[:, None] - b)[None, :, :, None], vc))
        return S, o

    S, o = lax.scan(step, s0.astype(jnp.float32), (blocks(q), blocks(k), blocks(v), ld))
    o = o.transpose(1, 0, 3, 2, 4).reshape(B, n * C, H, DV)[:, :T]
    return o, S


def t5_bucket(dist):
    dist = jnp.maximum(dist, 0)
    max_exact = N_BUCKETS // 2
    large = max_exact + (jnp.log(jnp.maximum(dist, max_exact).astype(jnp.float32) / max_exact)
                         / math.log(MAX_DISTANCE / max_exact) * (N_BUCKETS - max_exact)).astype(jnp.int32)
    large = jnp.minimum(large, N_BUCKETS - 1)
    return jnp.where(dist < max_exact, dist, large)


def kv_timeline(past, new):
    parts = [new] if past is None else [past.astype(new.dtype), new]
    L = sum(a.shape[1] for a in parts)
    pad = (-L) % MOBA_BLOCK
    if pad:
        parts.append(jnp.zeros((new.shape[0], pad) + new.shape[2:], new.dtype))
    return jnp.concatenate(parts, axis=1) if len(parts) > 1 else parts[0]


def moba_attention(q, k_tl, v_tl, q_start, rel_bias):
    B, NQ, H, HD = q.shape
    nb = k_tl.shape[1] // MOBA_BLOCK
    kb = k_tl.reshape(B, nb, MOBA_BLOCK, H, HD)
    vb = v_tl.reshape(B, nb, MOBA_BLOCK, H, HD)
    k_mean = jnp.mean(kb.astype(jnp.float32), axis=2)
    n_sel = min(MOBA_TOPK, nb)
    qb = min(MOBA_QBLOCK, NQ)
    n_qb = -(-NQ // qb)
    q_pad = jnp.pad(q, ((0, 0), (0, n_qb * qb - NQ), (0, 0), (0, 0)))
    q_blocks = q_pad.reshape(B, n_qb, qb, H, HD).transpose(1, 0, 2, 3, 4)
    starts = q_start + qb * jnp.arange(n_qb, dtype=jnp.int32)
    scale = HD ** -0.5
    blk_ids = jnp.arange(nb)
    offs = jnp.arange(MOBA_BLOCK)
    bi = jnp.arange(B)[:, None, None, None]
    hi = jnp.arange(H)[None, None, :, None]
    h_b = jnp.arange(H)[None, None, :, None, None]

    def attend(args):
        qc, p0 = args
        qf = qc.astype(jnp.float32)
        qpos = p0 + jnp.arange(qb, dtype=jnp.int32)
        n_past = qpos // MOBA_BLOCK
        gate = jnp.einsum('bqhd,bnhd->bqhn', qf, k_mean)
        gate = jnp.where((blk_ids[None, :] < n_past[:, None])[None, :, None, :], gate, NEG_INF)
        _, idx = lax.top_k(gate, n_sel)
        valid = idx < n_past[None, :, None, None]
        kg = kb[bi, idx, :, hi]
        vg = vb[bi, idx, :, hi]
        kpos = idx[..., None] * MOBA_BLOCK + offs
        bias_sel = rel_bias[t5_bucket(qpos[None, :, None, None, None] - kpos), h_b]
        logit_sel = (jnp.einsum('bqhd,bqhksd->bqhks', qf, kg.astype(jnp.float32)) * scale
                     + bias_sel.astype(jnp.float32))
        logit_sel = jnp.where(valid[..., None], logit_sel, NEG_INF)
        own0 = (p0 // MOBA_BLOCK) * MOBA_BLOCK
        k_own = lax.dynamic_slice_in_dim(k_tl, own0, MOBA_BLOCK, axis=1)
        v_own = lax.dynamic_slice_in_dim(v_tl, own0, MOBA_BLOCK, axis=1)
        dist_own = qpos[:, None] - (own0 + offs)[None, :]
        bias_own = rel_bias[t5_bucket(dist_own)].transpose(0, 2, 1)
        logit_own = (jnp.einsum('bqhd,bshd->bqhs', qf, k_own.astype(jnp.float32)) * scale
                     + bias_own[None].astype(jnp.float32))
        logit_own = jnp.where((dist_own >= 0)[None, :, None, :], logit_own, NEG_INF)
        logits = jnp.concatenate([logit_sel.reshape(B, qb, H, n_sel * MOBA_BLOCK), logit_own], -1)
        probs = jax.nn.softmax(logits, axis=-1)
        p_sel = probs[..., :n_sel * MOBA_BLOCK].reshape(B, qb, H, n_sel, MOBA_BLOCK)
        p_own = probs[..., n_sel * MOBA_BLOCK:]
        out = (jnp.einsum('bqhks,bqhksd->bqhd', p_sel, vg.astype(jnp.float32))
               + jnp.einsum('bqhs,bshd->bqhd', p_own, v_own.astype(jnp.float32)))
        return out.astype(q.dtype)

    out = lax.map(attend, (q_blocks, starts))
    return out.transpose(1, 0, 2, 3, 4).reshape(B, n_qb * qb, H, HD)[:, :NQ]


def token_mixers(h, pos0, s_gla0, s_ret0, k_past, v_past, p, rel_bias):
    B, T, _ = h.shape
    dt = h.dtype
    split_at = np.cumsum(IN_SIZES)[:-1].tolist()
    (gq, gk, gv, glr, gr, rq, rk, rv, rg, mq, mk, mv, mg) = jnp.split(h @ p['w_in'], split_at, axis=-1)

    def heads(a, n):
        return a.reshape(B, T, n, -1)

    log_a = jax.nn.log_sigmoid((glr @ p['gla_gk_w2'] + p['gla_gk_b']).astype(jnp.float32)) / GLA_GATE_TEMP
    o_a, s_gla = gla_chunked(heads(gq, GLA_HEADS).astype(jnp.float32) * GLA_DK ** -0.5,
                             heads(gk, GLA_HEADS), heads(gv, GLA_HEADS), heads(log_a, GLA_HEADS), s_gla0)
    o_a = (head_rms(o_a, p['gla_norm']) * jax.nn.silu(heads(gr, GLA_HEADS).astype(jnp.float32)))
    o_a = o_a.reshape(B, T, GLA_V_W).astype(dt)
    pos = pos0 + jnp.arange(T, dtype=jnp.int32)
    log_gamma = jnp.log(1.0 - 2.0 ** (-5.0 - jnp.arange(RET_HEADS, dtype=jnp.float32)))
    o_b, s_ret = retention_chunked(rope(heads(rq, RET_HEADS), pos),
                                   rope(heads(rk, RET_HEADS), pos) * RET_DK ** -0.5,
                                   heads(rv, RET_HEADS), log_gamma, s_ret0)
    o_b = (head_layernorm(o_b, p['ret_norm']) * jax.nn.silu(heads(rg, RET_HEADS).astype(jnp.float32)))
    o_b = o_b.reshape(B, T, RET_V_W).astype(dt)
    q_m = head_rms(heads(mq, MOBA_HEADS), p['moba_qnorm']).astype(dt)
    k_m = head_rms(heads(mk, MOBA_HEADS), p['moba_knorm']).astype(dt)
    v_m = heads(mv, MOBA_HEADS)
    o_c = moba_attention(q_m, kv_timeline(k_past, k_m), kv_timeline(v_past, v_m), pos0, rel_bias)
    o_c = o_c.reshape(B, T, MOBA_W)
    g_a, g_b, g_c = jnp.split(jax.nn.sigmoid(mg), 3, axis=-1)
    merged = (g_a * (o_a @ p['w_br_gla']) + g_b * (o_b @ p['w_br_ret'])
              + g_c * (o_c @ p['w_br_moba']))
    return merged @ p['w_out'], s_gla, s_ret, k_m, v_m


def swiglu(h, w1, w3, w2):
    return (jax.nn.silu(h @ w1) * (h @ w3)) @ w2


def moe_swiglu(h, w_router, w1, w3, w2):
    logits = (h @ w_router).astype(jnp.float32)
    top_val, top_idx = lax.top_k(logits, TOP_K)
    wts = jax.nn.softmax(top_val, axis=-1)
    gates = jnp.sum(jax.nn.one_hot(top_idx, N_EXPERTS, dtype=jnp.float32) * wts[..., None], axis=-2)
    out = jnp.zeros(h.shape, jnp.float32)
    for e in range(N_EXPERTS):
        out = out + gates[..., e:e + 1] * swiglu(h, w1[e], w3[e], w2[e]).astype(jnp.float32)
    return out.astype(h.dtype)


def layer(x, c, pos0, s_gla0, s_ret0, k_past, v_past, p, rel_bias, chan):
    mod = (c @ p['ada_w'] + p['ada_b'])[:, None, :]
    sh1, sc1, g1, sh2, sc2, g2 = jnp.split(mod, 6, axis=-1)
    h = rms_norm(x, p['norm1']) * (1 + sc1) + sh1
    mix, s_gla, s_ret, k_new, v_new = token_mixers(h, pos0, s_gla0, s_ret0, k_past, v_past, p, rel_bias)
    x = x + g1 * mix
    h = rms_norm(x, p['norm2']) * (1 + sc2) + sh2
    x = x + g2 * chan(h)
    return x, s_gla, s_ret, k_new, v_new


def setup_inputs(seed: int = 0) -> dict:
    key = jax.random.key(seed)
    ks = iter(jax.random.split(key, 40))
    f32 = jnp.float32

    def nrm(shape, scale):
        return jax.random.normal(next(ks), shape, f32) * scale

    n_pages = PAST_LEN // PAGE_SIZE
    n_phys = (DEC_BATCH * n_pages * 5) // 4
    n_dense = (DEPTH + 1) // 2
    n_moe = DEPTH // 2
    x_prompt = nrm((BATCH, SEQ, D_MODEL), 1.0)
    x_sample = nrm((DEC_BATCH, DEC_SEQ, D_MODEL), 1.0)
    cache_k = nrm((DEPTH, n_phys, PAGE_SIZE, MOBA_HEADS, MOBA_HD), 1.0)
    cache_v = nrm((DEPTH, n_phys, PAGE_SIZE, MOBA_HEADS, MOBA_HD), 1.0)
    state_gla = nrm((DEPTH, DEC_BATCH, GLA_HEADS, GLA_DK, GLA_DV), 0.5)
    state_ret = nrm((DEPTH, DEC_BATCH, RET_HEADS, RET_DK, RET_DV), 0.5)
    page_table = jax.random.permutation(next(ks), n_phys)[:DEC_BATCH * n_pages].reshape(
        DEC_BATCH, n_pages).astype(jnp.int32)
    return {
        'x_prompt': x_prompt, 'x_sample': x_sample,
        'cache_k': cache_k, 'cache_v': cache_v,
        'state_gla': state_gla, 'state_ret': state_ret,
        'page_table': page_table,
        'c_prompt': nrm((BATCH, D_MODEL), 1.0), 'c_sample': nrm((DEC_BATCH, D_MODEL), 1.0),
        'ada_w': nrm((DEPTH, D_MODEL, 6 * D_MODEL), 0.5 * D_MODEL ** -0.5),
        'ada_b': nrm((DEPTH, 6 * D_MODEL), 0.02),
        'norm1': 1.0 + nrm((DEPTH, D_MODEL), 0.05),
        'norm2': 1.0 + nrm((DEPTH, D_MODEL), 0.05),
        'w_in': nrm((DEPTH, D_MODEL, N_IN), D_MODEL ** -0.5),
        'gla_gk_w2': nrm((DEPTH, GLA_LOWRANK, GLA_QK_W), GLA_LOWRANK ** -0.5),
        'gla_gk_b': nrm((DEPTH, GLA_QK_W), 0.1),
        'gla_norm': 1.0 + nrm((DEPTH, GLA_DV), 0.05),
        'ret_norm': 1.0 + nrm((DEPTH, RET_DV), 0.05),
        'moba_qnorm': 1.0 + nrm((DEPTH, MOBA_HD), 0.05),
        'moba_knorm': 1.0 + nrm((DEPTH, MOBA_HD), 0.05),
        'rel_bias': nrm((N_BUCKETS, MOBA_HEADS), 0.5),
        'w_br_gla': nrm((DEPTH, GLA_V_W, D_MODEL), GLA_V_W ** -0.5),
        'w_br_ret': nrm((DEPTH, RET_V_W, D_MODEL), RET_V_W ** -0.5),
        'w_br_moba': nrm((DEPTH, MOBA_W, D_MODEL), MOBA_W ** -0.5),
        'w_out': nrm((DEPTH, D_MODEL, D_MODEL), D_MODEL ** -0.5),
        'ffn_w1': nrm((n_dense, D_MODEL, D_FF), D_MODEL ** -0.5),
        'ffn_w3': nrm((n_dense, D_MODEL, D_FF), D_MODEL ** -0.5),
        'ffn_w2': nrm((n_dense, D_FF, D_MODEL), D_FF ** -0.5),
        'moe_router': nrm((n_moe, D_MODEL, N_EXPERTS), D_MODEL ** -0.5),
        'moe_w1': nrm((n_moe, N_EXPERTS, D_MODEL, D_FF_EXPERT), D_MODEL ** -0.5),
        'moe_w3': nrm((n_moe, N_EXPERTS, D_MODEL, D_FF_EXPERT), D_MODEL ** -0.5),
        'moe_w2': nrm((n_moe, N_EXPERTS, D_FF_EXPERT, D_MODEL), D_FF_EXPERT ** -0.5),
    }


def reference(x_prompt, x_sample, cache_k, cache_v, state_gla, state_ret, page_table,
              c_prompt, c_sample, ada_w, ada_b, norm1, norm2, w_in, gla_gk_w2, gla_gk_b,
              gla_norm, ret_norm, moba_qnorm, moba_knorm, rel_bias, w_br_gla, w_br_ret,
              w_br_moba, w_out, ffn_w1, ffn_w3, ffn_w2, moe_router, moe_w1, moe_w3, moe_w2):
    dec_b, n_pages = page_table.shape
    past_len = n_pages * cache_k.shape[2]
    bp = x_prompt.shape[0]
    xp, xs = x_prompt, x_sample
    kp_l, vp_l, gp_l, rp_l, ks_l, vs_l, gs_l, rs_l = [], [], [], [], [], [], [], []
    for l in range(DEPTH):
        p = dict(ada_w=ada_w[l], ada_b=ada_b[l], norm1=norm1[l], norm2=norm2[l], w_in=w_in[l],
                 gla_gk_w2=gla_gk_w2[l], gla_gk_b=gla_gk_b[l], gla_norm=gla_norm[l],
                 ret_norm=ret_norm[l], moba_qnorm=moba_qnorm[l], moba_knorm=moba_knorm[l],
                 w_br_gla=w_br_gla[l], w_br_ret=w_br_ret[l], w_br_moba=w_br_moba[l], w_out=w_out[l])
        i = l // 2
        if l % 2 == 0:
            chan = lambda h, i=i: swiglu(h, ffn_w1[i], ffn_w3[i], ffn_w2[i])
        else:
            chan = lambda h, i=i: moe_swiglu(h, moe_router[i], moe_w1[i], moe_w3[i], moe_w2[i])
        zg = jnp.zeros((bp, GLA_HEADS, GLA_DK, GLA_DV), jnp.float32)
        zr = jnp.zeros((bp, RET_HEADS, RET_DK, RET_DV), jnp.float32)
        xp, sgp, srp, kp, vp = layer(xp, c_prompt, 0, zg, zr, None, None, p, rel_bias, chan)
        k_past = cache_k[l, page_table].reshape(dec_b, past_len, MOBA_HEADS, MOBA_HD)
        v_past = cache_v[l, page_table].reshape(dec_b, past_len, MOBA_HEADS, MOBA_HD)
        xs, sgs, srs, ks_, vs_ = layer(xs, c_sample, past_len, state_gla[l], state_ret[l],
                                       k_past, v_past, p, rel_bias, chan)
        kp_l.append(kp); vp_l.append(vp)
        gp_l.append(sgp.astype(x_prompt.dtype)); rp_l.append(srp.astype(x_prompt.dtype))
        ks_l.append(ks_.astype(cache_k.dtype)); vs_l.append(vs_.astype(cache_v.dtype))
        gs_l.append(sgs.astype(state_gla.dtype)); rs_l.append(srs.astype(state_ret.dtype))
    return (xp, xs, jnp.stack(kp_l), jnp.stack(vp_l), jnp.stack(gp_l), jnp.stack(rp_l),
            jnp.stack(ks_l), jnp.stack(vs_l), jnp.stack(gs_l), jnp.stack(rs_l))
```

```python
import functools
import math

import numpy as np
import jax
import jax.numpy as jnp
from jax import lax
from jax.experimental import pallas as pl
from jax.experimental.pallas import tpu as pltpu

F32 = jnp.float32
BF16 = jnp.bfloat16
BS = pl.BlockSpec
SDS = jax.ShapeDtypeStruct

D_MODEL = 1024
GLA_HEADS, GLA_DK, GLA_DV, GLA_LOWRANK, GLA_GATE_TEMP = 4, 64, 128, 16, 16.0
RET_HEADS, RET_DK, RET_DV = 4, 64, 128
ROPE_BASE = 10000.0
MOBA_HEADS, MOBA_HD, MOBA_BLOCK, MOBA_TOPK = 8, 64, 256, 3
N_BUCKETS, MAX_DISTANCE = 32, 128
D_FF, N_EXPERTS, TOP_K, D_FF_EXPERT = 2816, 8, 2, 1408
EPS = 1e-6
BIG = 1e30
NEG = -3.0e38

QK_W = GLA_HEADS * GLA_DK
V_W = GLA_HEADS * GLA_DV
MOBA_W = MOBA_HEADS * MOBA_HD
N_MAIN = 7680
COL_GQ, COL_GK, COL_RQ, COL_RK = 12, 13, 18, 19
COL_GV, COL_GR, COL_RV, COL_RG, COL_MQ, COL_MK, COL_MV = 7, 8, 10, 11, 12, 13, 14

VMEM_LIMIT_BYTES = 56 * 1024 * 1024
CHUNK = 128
PAGES_PER_STEP = 16


def _cparams(*sem):
    return pltpu.CompilerParams(dimension_semantics=sem, vmem_limit_bytes=VMEM_LIMIT_BYTES)


def _dg(a, b, dims):
    return lax.dot_general(a, b, (dims, ((), ())), preferred_element_type=F32)


NN = ((1,), (0,))
NT = ((1,), (1,))
TN = ((0,), (0,))


def _split2(x):
    h = x.astype(BF16)
    m = (x - h.astype(F32)).astype(BF16)
    return h, m


def _split3(x):
    h = x.astype(BF16)
    r = x - h.astype(F32)
    m = r.astype(BF16)
    l = (r - m.astype(F32)).astype(BF16)
    return h, m, l


def _dot_x3(a, b, dims=NN):
    ah, am = _split2(a)
    bh, bm = _split2(b)
    return _dg(ah, bh, dims) + _dg(ah, bm, dims) + _dg(am, bh, dims)


def _sigmoid(x):
    return 1.0 / (1.0 + jnp.exp(-x))


def _silu(x):
    return x * _sigmoid(x)


def _mod_kernel(c_ref, w_ref, b_ref, o_ref):
    o_ref[...] = jnp.dot(c_ref[...].astype(BF16), w_ref[...].astype(BF16),
                         preferred_element_type=F32) + b_ref[...]


def _ada_mod(c_all, ada_w, ada_b3, l):
    R = c_all.shape[0]
    TN_ = 1536
    n = 6 * D_MODEL
    return pl.pallas_call(
        _mod_kernel, grid=(n // TN_,),
        in_specs=[BS((R, D_MODEL), lambda j: (0, 0)),
                  BS((None, D_MODEL, TN_), lambda j: (l, 0, j)),
                  BS((None, 1, TN_), lambda j: (l, 0, j))],
        out_specs=BS((R, TN_), lambda j: (0, j)),
        out_shape=SDS((R, n), F32),
        compiler_params=_cparams("arbitrary"))(c_all, ada_w, ada_b3)


def _modulated_norm(x, nw, sc, sh):
    ms = jnp.mean(x * x, axis=-1, keepdims=True)
    return (x * lax.rsqrt(ms + EPS) * nw) * (1.0 + sc) + sh


def _inproj_kernel(x_ref, nw_ref, sc_ref, sh_ref, w_ref, wg_ref, o_ref, og_ref, h_ref):
    @pl.when(pl.program_id(1) == 0)
    def _():
        h = _modulated_norm(x_ref[...], nw_ref[...], sc_ref[...], sh_ref[...]).astype(BF16)
        h_ref[...] = h
        og_ref[...] = jnp.dot(h, wg_ref[...], preferred_element_type=F32)

    o_ref[...] = jnp.dot(h_ref[...], w_ref[...], preferred_element_type=F32)


def _mod_spec(mod_rows, tm, ngrid):
    if mod_rows == 1:
        return BS((1, D_MODEL), (lambda i, j: (0, 0)) if ngrid == 2 else (lambda i: (0, 0)))
    return BS((tm, D_MODEL), (lambda i, j: (i, 0)) if ngrid == 2 else (lambda i: (i, 0)))


def _inproj(x, nw3, sc, sh, w_main, w_glr, l, tm):
    M = x.shape[0]
    TN_ = 512
    ms = _mod_spec(sc.shape[0], tm, 2)
    return pl.pallas_call(
        _inproj_kernel, grid=(M // tm, N_MAIN // TN_),
        in_specs=[BS((tm, D_MODEL), lambda i, j: (i, 0)),
                  BS((None, 1, D_MODEL), lambda i, j: (l, 0, 0)),
                  ms, ms,
                  BS((None, D_MODEL, TN_), lambda i, j: (l, 0, j)),
                  BS((None, D_MODEL, 128), lambda i, j: (l, 0, 0))],
        out_specs=[BS((tm, TN_), lambda i, j: (i, j)),
                   BS((tm, 128), lambda i, j: (i, 0))],
        out_shape=[SDS((M, N_MAIN), F32), SDS((M, 128), F32)],
        scratch_shapes=[pltpu.VMEM((tm, D_MODEL), BF16)],
        compiler_params=_cparams("arbitrary", "arbitrary"))(x, nw3, sc, sh, w_main, w_glr)


def _load_rows(ref, pad_ref, rows, C):
    if rows == C:
        return ref[0]
    pad_ref[...] = jnp.zeros_like(pad_ref)
    pad_ref[0:rows, :] = ref[0]
    return pad_ref[...]


def _state_init(S_ref, s0_ref, nh, dk, dv):
    S_ref[...] = jnp.zeros_like(S_ref)
    for h in range(nh):
        S_ref[dk * h:dk * (h + 1), dv * h:dv * (h + 1)] = s0_ref[0, h]


def _state_out(so_ref, S_ref, nh, dk, dv):
    for h in range(nh):
        so_ref[0, h] = S_ref[dk * h:dk * (h + 1), dv * h:dv * (h + 1)]


def _block_diag_mask():
    r = lax.broadcasted_iota(jnp.int32, (QK_W, V_W), 0) // GLA_DK
    c = lax.broadcasted_iota(jnp.int32, (QK_W, V_W), 1) // GLA_DV
    return r == c


def _gla_kernel(q_ref, k_ref, v_ref, r_ref, g_ref, w2_ref, b_ref, nw_ref, s0_ref,
                o_ref, so_ref, S_ref, *pads, C, rows, nc):
    c = pl.program_id(1)

    @pl.when(c == 0)
    def _():
        _state_init(S_ref, s0_ref, GLA_HEADS, GLA_DK, GLA_DV)

    pads = list(pads) + [None] * 5
    q = _load_rows(q_ref, pads[0], rows, C)
    k = _load_rows(k_ref, pads[1], rows, C)
    v = _load_rows(v_ref, pads[2], rows, C)
    r = _load_rows(r_ref, pads[3], rows, C)
    g = _load_rows(g_ref, pads[4], rows, C)

    row = lax.broadcasted_iota(jnp.int32, (C, QK_W), 0)
    x = jnp.dot(g.astype(BF16), w2_ref[...].astype(BF16), preferred_element_type=F32) + b_ref[...]
    log_a = (jnp.minimum(x, 0.0) - jnp.log(1.0 + jnp.exp(-jnp.abs(x)))) * (1.0 / GLA_GATE_TEMP)
    if rows < C:
        log_a = jnp.where(row < rows, log_a, 0.0)

    ri = lax.broadcasted_iota(jnp.int32, (C, C), 0)
    ci = lax.broadcasted_iota(jnp.int32, (C, C), 1)
    causal = ri >= ci
    ltri = jnp.where(causal, 1.0, 0.0).astype(BF16)
    ones_c = jnp.ones((C, 128), BF16)
    pieces = _split3(log_a)
    b = sum(_dg(ltri, p, NN) for p in pieces)
    b_last_col = sum(_dg(p, ones_c, TN) for p in pieces)
    mid = max(min(rows, C) // 2, 1)
    b_ref_row = b[mid - 1:mid, :]
    b_last = b[C - 1:C, :]

    qs = q * (GLA_DK ** -0.5)
    qt_h, qt_m = _split2(qs * jnp.exp(b - b_ref_row))
    kt_h, kt_m = _split2(k * jnp.exp(b_ref_row - b))
    q_state = (qs * jnp.exp(b)).astype(BF16)
    k_state = (k * jnp.exp(b_last - b)).astype(BF16)
    vb = v.astype(BF16)

    S = S_ref[...]
    o_state = _dg(q_state, S.astype(BF16), NN)
    lane_head = lax.broadcasted_iota(jnp.int32, (C, QK_W), 1) // GLA_DK
    nw = nw_ref[...]
    for h in range(GLA_HEADS):
        mine = lane_head == h
        qh_h = jnp.where(mine, qt_h, jnp.zeros_like(qt_h))
        qh_m = jnp.where(mine, qt_m, jnp.zeros_like(qt_m))
        a = _dg(qh_h, kt_h, NT) + _dg(qh_h, kt_m, NT) + _dg(qh_m, kt_h, NT)
        a = jnp.where(causal, a, 0.0).astype(BF16)
        sl = slice(GLA_DV * h, GLA_DV * (h + 1))
        oh = _dg(a, vb[:, sl], NN) + o_state[:, sl]
        ms = jnp.mean(oh * oh, axis=-1, keepdims=True)
        y = oh * lax.rsqrt(ms + EPS) * nw[:, sl] * _silu(r[:, sl])
        o_ref[0, :, sl] = y[0:rows].astype(o_ref.dtype)

    u = _dg(k_state, vb, TN)
    e_col = jnp.exp(b_last_col)
    e_full = jnp.concatenate([e_col] * (V_W // 128), axis=1)
    S_ref[...] = e_full * S + jnp.where(_block_diag_mask(), u, 0.0)

    @pl.when(c == nc - 1)
    def _():
        _state_out(so_ref, S_ref, GLA_HEADS, GLA_DK, GLA_DV)


def _mixer_specs(rows):
    def col(width, idx):
        return BS((1, rows, width), lambda b, c: (b, c, idx))
    return col


def _gla(P3, G3, w2p, bias3, nw3, s0, l, C, rows):
    B, T, _ = P3.shape
    nc = T // rows
    col = _mixer_specs(rows)
    kern = functools.partial(_gla_kernel, C=C, rows=rows, nc=nc)
    scratch = [pltpu.VMEM((QK_W, V_W), F32)]
    if rows < C:
        scratch += [pltpu.VMEM((C, w), F32) for w in (QK_W, QK_W, V_W, V_W, 128)]
    return pl.pallas_call(
        kern, grid=(B, nc),
        in_specs=[col(QK_W, COL_GQ), col(QK_W, COL_GK), col(V_W, COL_GV), col(V_W, COL_GR),
                  BS((1, rows, 128), lambda b, c: (b, c, 0)),
                  BS((None, 128, QK_W), lambda b, c: (l, 0, 0)),
                  BS((None, 1, QK_W), lambda b, c: (l, 0, 0)),
                  BS((None, 1, V_W), lambda b, c: (l, 0, 0)),
                  BS((1, GLA_HEADS, GLA_DK, GLA_DV), lambda b, c: (b, 0, 0, 0))],
        out_specs=[BS((1, rows, V_W), lambda b, c: (b, c, 0)),
                   BS((1, GLA_HEADS, GLA_DK, GLA_DV), lambda b, c: (b, 0, 0, 0))],
        out_shape=[SDS((B, T, V_W), BF16), SDS((B, GLA_HEADS, GLA_DK, GLA_DV), F32)],
        scratch_shapes=scratch,
        compiler_params=_cparams("arbitrary", "arbitrary"))(P3, P3, P3, P3, G3, w2p, bias3, nw3, s0)


def _ret_kernel(q_ref, k_ref, v_ref, g_ref, cos_ref, sin_ref, dm_ref, rs_ref, ks_ref, cd_ref,
                nw_ref, s0_ref, o_ref, so_ref, S_ref, *pads, C, rows, nc):
    c = pl.program_id(1)

    @pl.when(c == 0)
    def _():
        _state_init(S_ref, s0_ref, RET_HEADS, RET_DK, RET_DV)

    pads = list(pads) + [None] * 4
    q = _load_rows(q_ref, pads[0], rows, C)
    k = _load_rows(k_ref, pads[1], rows, C)
    v = _load_rows(v_ref, pads[2], rows, C)
    g = _load_rows(g_ref, pads[3], rows, C)

    cos = cos_ref[...]
    sin = sin_ref[...]
    first_half = (lax.broadcasted_iota(jnp.int32, (C, QK_W), 1) % RET_DK) < (RET_DK // 2)

    def rope(x):
        partner = jnp.where(first_half, pltpu.roll(x, QK_W - RET_DK // 2, 1),
                            pltpu.roll(x, RET_DK // 2, 1))
        return x * cos + partner * sin

    qr = rope(q)
    kr = rope(k) * (RET_DK ** -0.5)
    qb = qr.astype(BF16)
    kb = kr.astype(BF16)
    vb = v.astype(BF16)
    k_state = (kr * ks_ref[...]).astype(BF16)

    S = S_ref[...]
    o_state = _dg(qb, S.astype(BF16), NN) * rs_ref[...]
    lane_head = lax.broadcasted_iota(jnp.int32, (C, QK_W), 1) // RET_DK
    nw = nw_ref[...]
    for h in range(RET_HEADS):
        a = _dg(jnp.where(lane_head == h, qb, jnp.zeros_like(qb)), kb, NT) * dm_ref[h]
        sl = slice(RET_DV * h, RET_DV * (h + 1))
        oh = _dg(a.astype(BF16), vb[:, sl], NN) + o_state[:, sl]
        oh = oh - jnp.mean(oh, axis=-1, keepdims=True)
        var = jnp.mean(oh * oh, axis=-1, keepdims=True)
        y = oh * lax.rsqrt(var + EPS) * nw[:, sl] * _silu(g[:, sl])
        o_ref[0, :, sl] = y[0:rows].astype(o_ref.dtype)

    u = _dg(k_state, vb, TN)
    S_ref[...] = cd_ref[...] * S + jnp.where(_block_diag_mask(), u, 0.0)

    @pl.when(c == nc - 1)
    def _():
        _state_out(so_ref, S_ref, RET_HEADS, RET_DK, RET_DV)


def _ret_constants(C, rows):
    log_gamma = np.log(1.0 - 2.0 ** (-5.0 - np.arange(RET_HEADS, dtype=np.float64)))
    bt = np.minimum(np.arange(C) + 1, rows).astype(np.float64)
    diff = bt[:, None] - bt[None, :]
    causal = np.arange(C)[:, None] >= np.arange(C)[None, :]
    dm = np.where(causal[None], np.exp(diff[None] * log_gamma[:, None, None]), 0.0)
    rs = np.exp(bt[:, None] * np.repeat(log_gamma, RET_DV)[None, :])
    ks = np.exp((bt[-1] - bt)[:, None] * np.repeat(log_gamma, RET_DK)[None, :])
    cd = np.exp(bt[-1] * np.repeat(log_gamma, RET_DV))[None, :]
    return (jnp.asarray(dm, F32), jnp.asarray(rs, F32), jnp.asarray(ks, F32), jnp.asarray(cd, F32))


def _ret(P3, cos_t, sin_t, nw3, s0, l, C, rows):
    B, T, _ = P3.shape
    nc = T // rows
    col = _mixer_specs(rows)
    dm, rs, ks, cd = _ret_constants(C, rows)
    kern = functools.partial(_ret_kernel, C=C, rows=rows, nc=nc)
    scratch = [pltpu.VMEM((QK_W, V_W), F32)]
    if rows < C:
        scratch += [pltpu.VMEM((C, w), F32) for w in (QK_W, QK_W, V_W, V_W)]
    const = lambda shape: BS(shape, lambda b, c: (0,) * len(shape))
    return pl.pallas_call(
        kern, grid=(B, nc),
        in_specs=[col(QK_W, COL_RQ), col(QK_W, COL_RK), col(V_W, COL_RV), col(V_W, COL_RG),
                  BS((C, QK_W), lambda b, c: (c, 0)), BS((C, QK_W), lambda b, c: (c, 0)),
                  const((RET_HEADS, C, C)), const((C, V_W)), const((C, QK_W)), const((1, V_W)),
                  BS((None, 1, V_W), lambda b, c: (l, 0, 0)),
                  BS((1, RET_HEADS, RET_DK, RET_DV), lambda b, c: (b, 0, 0, 0))],
        out_specs=[BS((1, rows, V_W), lambda b, c: (b, c, 0)),
                   BS((1, RET_HEADS, RET_DK, RET_DV), lambda b, c: (b, 0, 0, 0))],
        out_shape=[SDS((B, T, V_W), BF16), SDS((B, RET_HEADS, RET_DK, RET_DV), F32)],
        scratch_shapes=scratch,
        compiler_params=_cparams("arbitrary", "arbitrary"))(
            P3, P3, P3, P3, cos_t, sin_t, dm, rs, ks, cd, nw3, s0)


def _rope_tables(pos0, n):
    half = RET_DK // 2
    inv = ROPE_BASE ** (-jnp.arange(half, dtype=F32) / half)
    pos = (pos0 + jnp.arange(n, dtype=jnp.int32)).astype(F32)
    ang = pos[:, None] * inv[None, :]
    cos, sin = jnp.cos(ang), jnp.sin(ang)
    cos_t = jnp.tile(jnp.concatenate([cos, cos], -1), (1, RET_HEADS))
    sin_t = jnp.tile(jnp.concatenate([-sin, sin], -1), (1, RET_HEADS))
    return cos_t, sin_t


def _head_rms64(x, g_ref, w):
    xh, xm = _split2(x * x)
    ms = jnp.dot(xh, g_ref[...], preferred_element_type=F32) + jnp.dot(xm, g_ref[...], preferred_element_type=F32)
    return x * lax.rsqrt(ms + EPS) * w


def _qknorm_kernel(q_ref, k_ref, g_ref, qw_ref, kw_ref, qn_ref, kn_ref):
    qn_ref[...] = _head_rms64(q_ref[...], g_ref, qw_ref[...])
    kn_ref[...] = _head_rms64(k_ref[...], g_ref, kw_ref[...])


def _prompt_prep_kernel(q_ref, k_ref, v_ref, g_ref, pl_ref, qw_ref, kw_ref,
                        qn_ref, kn_ref, km_ref, kp_ref, vb_ref):
    i = pl.program_id(0)
    qn_ref[...] = _head_rms64(q_ref[...], g_ref, qw_ref[...])
    kn = _head_rms64(k_ref[...], g_ref, kw_ref[...])
    kn_ref[...] = kn
    km_ref[0] = jnp.mean(kn, axis=0, keepdims=True)
    lane = lax.broadcasted_iota(jnp.int32, (MOBA_BLOCK, 2 * MOBA_W), 1) % 128
    placed = jnp.dot(kn.astype(BF16), pl_ref[...], preferred_element_type=F32)
    kp_ref[...] = (placed + jnp.where(lane == MOBA_HD + i, BIG, 0.0)).astype(BF16)
    vb_ref[...] = v_ref[...].astype(BF16)


def _group_mean_matrix():
    r = np.arange(MOBA_W)
    return jnp.asarray((r[:, None] // MOBA_HD == r[None, :] // MOBA_HD) / MOBA_HD, BF16)


def _placement_matrix():
    r = np.arange(MOBA_W)
    c = np.arange(2 * MOBA_W)
    tgt = (r // MOBA_HD) * 128 + r % MOBA_HD
    return jnp.asarray(tgt[:, None] == c[None, :], BF16)


def _qknorm(P, qw3, kw3, l, tm):
    M = P.shape[0]
    return pl.pallas_call(
        _qknorm_kernel, grid=(M // tm,),
        in_specs=[BS((tm, MOBA_W), lambda i: (i, COL_MQ)), BS((tm, MOBA_W), lambda i: (i, COL_MK)),
                  BS((MOBA_W, MOBA_W), lambda i: (0, 0)),
                  BS((None, 1, MOBA_W), lambda i: (l, 0, 0)), BS((None, 1, MOBA_W), lambda i: (l, 0, 0))],
        out_specs=[BS((tm, MOBA_W), lambda i: (i, 0)), BS((tm, MOBA_W), lambda i: (i, 0))],
        out_shape=[SDS((M, MOBA_W), F32), SDS((M, MOBA_W), F32)],
        compiler_params=_cparams("arbitrary"))(P, P, _group_mean_matrix(), qw3, kw3)


def _prompt_prep(P, qw3, kw3, l):
    T = P.shape[0]
    nb = T // MOBA_BLOCK
    tm = MOBA_BLOCK
    row = lambda w, idx: BS((tm, w), lambda i: (i, idx))
    return pl.pallas_call(
        _prompt_prep_kernel, grid=(nb,),
        in_specs=[row(MOBA_W, COL_MQ), row(MOBA_W, COL_MK), row(MOBA_W, COL_MV),
                  BS((MOBA_W, MOBA_W), lambda i: (0, 0)),
                  BS((MOBA_W, 2 * MOBA_W), lambda i: (0, 0)),
                  BS((None, 1, MOBA_W), lambda i: (l, 0, 0)), BS((None, 1, MOBA_W), lambda i: (l, 0, 0))],
        out_specs=[row(MOBA_W, 0), row(MOBA_W, 0), BS((1, 1, MOBA_W), lambda i: (i, 0, 0)),
                   row(2 * MOBA_W, 0), row(MOBA_W, 0)],
        out_shape=[SDS((T, MOBA_W), F32), SDS((T, MOBA_W), F32), SDS((nb, 1, MOBA_W), F32),
                   SDS((T, 2 * MOBA_W), BF16), SDS((T, MOBA_W), BF16)],
        compiler_params=_cparams("arbitrary"))(P, P, P, _group_mean_matrix(), _placement_matrix(), qw3, kw3)


def _placed_block_means(kmean):
    nb = kmean.shape[0]
    km = kmean.reshape(nb, MOBA_HEADS, MOBA_HD).transpose(1, 2, 0)
    km = jnp.pad(km, ((0, 0), (0, 0), (MOBA_HD, MOBA_HD - nb)))
    full = km[:, :, None, :] * jnp.eye(MOBA_HEADS, dtype=F32)[:, None, :, None]
    return full.reshape(MOBA_W, 2 * MOBA_W)


def _select_top3(gate, valid):
    g1 = jnp.where(valid, gate, NEG)
    m1 = jnp.max(g1, axis=-1, keepdims=True)
    g2 = jnp.where(g1 >= m1, NEG, g1)
    m2 = jnp.max(g2, axis=-1, keepdims=True)
    g3 = jnp.where(g2 >= m2, NEG, g2)
    m3 = jnp.max(g3, axis=-1, keepdims=True)
    return valid & (g1 >= m3)


def _gate_kernel(qn_ref, km_ref, pl_ref, qp_ref):
    i = pl.program_id(0)
    qn = qn_ref[...]
    gate = jnp.dot(qn.astype(BF16), km_ref[...].astype(BF16), preferred_element_type=F32)
    qpart = jnp.dot((qn * (MOBA_HD ** -0.5)).astype(BF16), pl_ref[...], preferred_element_type=F32)
    lane = lax.broadcasted_iota(jnp.int32, (MOBA_BLOCK, 128), 1)
    blk = lane - MOBA_HD
    valid = (blk >= 0) & (blk < i)
    for h in range(MOBA_HEADS):
        sl = slice(128 * h, 128 * (h + 1))
        sel = _select_top3(gate[:, sl], valid) | (blk == i)
        maskpart = jnp.where((blk >= 0) & jnp.logical_not(sel), -1.0, 0.0)
        qp_ref[:, sl] = (qpart[:, sl] + maskpart).astype(BF16)


def _gate(qn, km_placed):
    T = qn.shape[0]
    nb = T // MOBA_BLOCK
    return pl.pallas_call(
        _gate_kernel, grid=(nb,),
        in_specs=[BS((MOBA_BLOCK, MOBA_W), lambda i: (i, 0)),
                  BS((MOBA_W, 2 * MOBA_W), lambda i: (0, 0)),
                  BS((MOBA_W, 2 * MOBA_W), lambda i: (0, 0))],
        out_specs=BS((MOBA_BLOCK, 2 * MOBA_W), lambda i: (i, 0)),
        out_shape=SDS((T, 2 * MOBA_W), BF16),
        compiler_params=_cparams("arbitrary"))(qn, km_placed, _placement_matrix())


def _bucket_np(dist):
    dist = np.maximum(dist, 0)
    max_exact = N_BUCKETS // 2
    large = max_exact + (np.log(np.maximum(dist, max_exact).astype(np.float64) / max_exact)
                         / math.log(MAX_DISTANCE / max_exact) * (N_BUCKETS - max_exact)).astype(np.int64)
    large = np.minimum(large, N_BUCKETS - 1)
    return np.where(dist < max_exact, dist, large).astype(np.int32)


def _bias_kernel(rb_ref, idx_ref, ok_ref, o_ref):
    h = pl.program_id(0)
    idx = idx_ref[...]
    far = rb_ref[N_BUCKETS - 1, h]
    acc = jnp.zeros(idx.shape, F32)
    for b in range(N_BUCKETS - 1):
        acc = jnp.where(idx == b, rb_ref[b, h] - far, acc)
    o_ref[0] = jnp.where(ok_ref[...] > 0, acc, -BIG)


def _bias_tables(rel_bias, dist):
    idx = jnp.asarray(_bucket_np(dist))
    ok = jnp.asarray((dist >= 0).astype(np.int32))
    shp = dist.shape
    zeros = (0,) * len(shp)
    return pl.pallas_call(
        _bias_kernel, grid=(MOBA_HEADS,),
        in_specs=[BS(memory_space=pltpu.SMEM), BS(shp, lambda h: zeros), BS(shp, lambda h: zeros)],
        out_specs=BS((1,) + shp, lambda h: (h,) + zeros),
        out_shape=SDS((MOBA_HEADS,) + shp, F32),
        compiler_params=_cparams("arbitrary"))(rel_bias, idx, ok)


def _attn_kernel(qp_ref, kp_ref, vb_ref, bt_ref, o_ref, acc_ref):
    i = pl.program_id(1)
    tq = MOBA_BLOCK

    def tile(hh, n, m, l, bias):
        q = qp_ref[:, 128 * hh:128 * (hh + 1)]
        start = pl.multiple_of(n * MOBA_BLOCK, MOBA_BLOCK)
        kblk = kp_ref[pl.ds(start, MOBA_BLOCK), 128 * hh:128 * (hh + 1)]
        s = _dg(q, kblk, NT)
        if bias is not None:
            s = s + bias
        m_new = jnp.maximum(m, jnp.max(s, axis=-1, keepdims=True))
        alpha = jnp.exp(m - m_new)
        p = jnp.exp(s - m_new)
        l_new = alpha * l + jnp.sum(p, axis=-1, keepdims=True)
        vblk = vb_ref[pl.ds(start, MOBA_BLOCK), :]
        acc_ref[hh] = alpha * acc_ref[hh] + jnp.dot(p.astype(BF16), vblk, preferred_element_type=F32)
        return m_new, l_new

    acc_ref[...] = jnp.zeros_like(acc_ref)
    init = tuple(jnp.full((tq, 1), NEG, F32) if j % 2 == 0 else jnp.zeros((tq, 1), F32) for j in range(4))

    def far_body(n, carry):
        m0, l0, m1, l1 = carry
        m0, l0 = tile(0, n, m0, l0, None)
        m1, l1 = tile(1, n, m1, l1, None)
        return m0, l0, m1, l1

    carry = lax.fori_loop(0, jnp.maximum(i - 1, 0), far_body, init)

    def near(n, t, carry):
        m0, l0, m1, l1 = carry
        m0, l0 = tile(0, n, m0, l0, bt_ref[0, t])
        m1, l1 = tile(1, n, m1, l1, bt_ref[1, t])
        return m0, l0, m1, l1

    carry = lax.cond(i >= 1, lambda cr: near(i - 1, 1, cr), lambda cr: cr, carry)
    m0, l0, m1, l1 = near(i, 0, carry)
    lane = lax.broadcasted_iota(jnp.int32, (tq, 128), 1)
    o_ref[...] = jnp.where(lane < MOBA_HD, acc_ref[0] / l0, acc_ref[1] / l1).astype(o_ref.dtype)


def _prompt_attention(qp, kp, vb, btab):
    T = qp.shape[0]
    nb = T // MOBA_BLOCK
    return pl.pallas_call(
        _attn_kernel, grid=(MOBA_HEADS // 2, nb),
        in_specs=[BS((MOBA_BLOCK, 256), lambda hp, i: (i, hp)),
                  BS((T, 256), lambda hp, i: (0, hp)),
                  BS((T, 128), lambda hp, i: (0, hp)),
                  BS((2, 2, MOBA_BLOCK, MOBA_BLOCK), lambda hp, i: (hp, 0, 0, 0))],
        out_specs=BS((MOBA_BLOCK, 128), lambda hp, i: (i, hp)),
        out_shape=SDS((T, MOBA_W), BF16),
        scratch_shapes=[pltpu.VMEM((2, MOBA_BLOCK, 128), F32)],
        compiler_params=_cparams("arbitrary", "arbitrary"))(qp, kp, vb, btab)


def _kstream_kernel(pt_ref, *refs):
    npg = PAGES_PER_STEP
    pages, (q_ref, km_ref, lg_ref) = refs[:npg], refs[npg:]
    q_h, q_m = _split2(q_ref[0])
    for r in range(npg):
        pg = pages[r][...]
        p_h, p_m = _split2(pg)
        lg_ref[0, :, 128 * r:128 * (r + 1)] = _dg(q_h, p_h, NT) + _dg(q_h, p_m, NT) + _dg(q_m, p_h, NT)
        s = jnp.sum(pg, axis=0, keepdims=True)
        if r % 2 == 0:
            prev = s
        else:
            km_ref[0, r // 2:r // 2 + 1, :] = (prev + s) * (1.0 / MOBA_BLOCK)


def _kstream(page_table, cache4, qbd, l):
    nseq, n_pages = page_table.shape
    npg = PAGES_PER_STEP
    ng = n_pages // npg
    page_specs = [BS((None, None, 128, MOBA_W), functools.partial(
        lambda b, g, pt, r: (l, pt[b, g * npg + r], 0, 0), r=r)) for r in range(npg)]
    nq = qbd.shape[1]
    grid_spec = pltpu.PrefetchScalarGridSpec(
        num_scalar_prefetch=1, grid=(nseq, ng),
        in_specs=page_specs + [BS((1, nq, MOBA_W), lambda b, g, pt: (b, 0, 0))],
        out_specs=[BS((1, npg // 2, MOBA_W), lambda b, g, pt: (b, g, 0)),
                   BS((1, nq, npg * 128), lambda b, g, pt: (b, 0, g))])
    return pl.pallas_call(
        _kstream_kernel, grid_spec=grid_spec,
        out_shape=[SDS((nseq, n_pages // 2, MOBA_W), F32), SDS((nseq, nq, n_pages * 128), F32)],
        compiler_params=_cparams("arbitrary", "arbitrary"))(page_table, *([cache4] * npg), qbd)


def _vstream_kernel(pt_ref, *refs, nblk):
    npg = PAGES_PER_STEP
    pages = refs[:npg]
    (lg_ref, km_ref, q_ref, kn_ref, vn_ref, bpast_ref, bown_ref, ex_ref, o_ref,
     p_ref, acc_ref, own_ref) = refs[npg:]
    g = pl.program_id(1)
    nq = q_ref.shape[1]

    @pl.when(g == 0)
    def _():
        q = q_ref[0]
        gate = _dg(q.astype(BF16), km_ref[0].astype(BF16), NT)
        lane = lax.broadcasted_iota(jnp.int32, (nq, nblk), 1)
        sel = _select_top3(gate, lane >= 0)
        selx = jnp.dot(jnp.where(sel, 1.0, 0.0).astype(BF16), ex_ref[...], preferred_element_type=F32)
        lg = lg_ref[0]
        L = lg.shape[1]
        lg = jnp.where(selx > 0.5, lg, -BIG)
        last = lg[:, L - MOBA_BLOCK:] + bpast_ref[...]
        s_own = _dg(q.astype(BF16), kn_ref[0].astype(BF16), NT) + bown_ref[...]
        m = jnp.maximum(jnp.maximum(jnp.max(lg[:, :L - MOBA_BLOCK], axis=-1, keepdims=True),
                                    jnp.max(last, axis=-1, keepdims=True)),
                        jnp.max(s_own, axis=-1, keepdims=True))
        p_far = jnp.exp(lg[:, :L - MOBA_BLOCK] - m)
        p_last = jnp.exp(last - m)
        p_own = jnp.exp(s_own - m)
        denom = (jnp.sum(p_far, axis=-1, keepdims=True) + jnp.sum(p_last, axis=-1, keepdims=True)
                 + jnp.sum(p_own, axis=-1, keepdims=True))
        inv = 1.0 / denom
        p_ref[:, :L - MOBA_BLOCK] = p_far * inv
        p_ref[:, L - MOBA_BLOCK:] = p_last * inv
        own_ref[...] = jnp.dot((p_own * inv).astype(BF16), vn_ref[0].astype(BF16), preferred_element_type=F32)
        acc_ref[...] = jnp.zeros_like(acc_ref)

    acc = acc_ref[...]
    for r in range(npg):
        start = pl.multiple_of((g * npg + r) * 128, 128)
        acc = acc + _dot_x3(p_ref[:, pl.ds(start, 128)], pages[r][...])
    acc_ref[...] = acc

    @pl.when(g == pl.num_programs(1) - 1)
    def _():
        o_ref[0] = acc_ref[...] + own_ref[...]


def _vstream(page_table, cache4, logits, kmean, qbd, kn_new, vn_new, b_past, b_own, expand, l):
    nseq, n_pages = page_table.shape
    npg = PAGES_PER_STEP
    ng = n_pages // npg
    nq = qbd.shape[1]
    nblk = n_pages // 2
    L = n_pages * 128
    page_specs = [BS((None, None, 128, MOBA_W), functools.partial(
        lambda b, g, pt, r: (l, pt[b, g * npg + r], 0, 0), r=r)) for r in range(npg)]
    per_seq = lambda shape: BS((1,) + shape, lambda b, g, pt: (b,) + (0,) * len(shape))
    const = lambda shape: BS(shape, lambda b, g, pt: (0,) * len(shape))
    grid_spec = pltpu.PrefetchScalarGridSpec(
        num_scalar_prefetch=1, grid=(nseq, ng),
        in_specs=page_specs + [per_seq((nq, L)), per_seq((nblk, MOBA_W)), per_seq((nq, MOBA_W)),
                               per_seq((8, MOBA_W)), per_seq((8, MOBA_W)),
                               const((nq, MOBA_BLOCK)), const((nq, 8)), const((nblk, L))],
        out_specs=per_seq((nq, MOBA_W)),
        scratch_shapes=[pltpu.VMEM((nq, L), F32), pltpu.VMEM((nq, MOBA_W), F32),
                        pltpu.VMEM((nq, MOBA_W), F32)])
    return pl.pallas_call(
        functools.partial(_vstream_kernel, nblk=nblk), grid_spec=grid_spec,
        out_shape=SDS((nseq, nq, MOBA_W), F32),
        compiler_params=_cparams("arbitrary", "arbitrary"))(
            page_table, *([cache4] * npg), logits, kmean, qbd, kn_new, vn_new, b_past, b_own, expand)


def _merge_kernel(oa_ref, ob_ref, oc_ref, mg_ref, x_ref, g1_ref, wa_ref, wb_ref, wc_ref, wo_ref, o_ref):
    mg = mg_ref[...]
    ya = jnp.dot(oa_ref[...], wa_ref[...], preferred_element_type=F32)
    yb = jnp.dot(ob_ref[...], wb_ref[...], preferred_element_type=F32)
    yc = jnp.dot(oc_ref[...], wc_ref[...], preferred_element_type=F32)
    merged = (_sigmoid(mg[:, 0:D_MODEL]) * ya + _sigmoid(mg[:, D_MODEL:2 * D_MODEL]) * yb
              + _sigmoid(mg[:, 2 * D_MODEL:]) * yc)
    mix = jnp.dot(merged.astype(BF16), wo_ref[...], preferred_element_type=F32)
    o_ref[...] = x_ref[...] + g1_ref[...] * mix


def _merge(oa, ob, oc, P, x, g1, wa, wb, wc, wo, l, tm):
    M = x.shape[0]
    row = lambda w: BS((tm, w), lambda i: (i, 0))
    wspec = lambda k: BS((None, k, D_MODEL), lambda i: (l, 0, 0))
    return pl.pallas_call(
        _merge_kernel, grid=(M // tm,),
        in_specs=[row(V_W), row(V_W), row(MOBA_W), row(3 * D_MODEL), row(D_MODEL),
                  _mod_spec(g1.shape[0], tm, 1),
                  wspec(V_W), wspec(V_W), wspec(MOBA_W), wspec(D_MODEL)],
        out_specs=row(D_MODEL),
        out_shape=SDS((M, D_MODEL), F32),
        compiler_params=_cparams("arbitrary"))(oa, ob, oc, P, x, g1, wa, wb, wc, wo)


def _ffn_kernel(x_ref, nw_ref, sc_ref, sh_ref, g2_ref, w1_ref, w3_ref, w2_ref, o_ref, h_ref, acc_ref):
    j = pl.program_id(1)

    @pl.when(j == 0)
    def _():
        h_ref[...] = _modulated_norm(x_ref[...], nw_ref[...], sc_ref[...], sh_ref[...]).astype(BF16)
        acc_ref[...] = jnp.zeros_like(acc_ref)

    h = h_ref[...]
    a = jnp.dot(h, w1_ref[...], preferred_element_type=F32)
    b = jnp.dot(h, w3_ref[...], preferred_element_type=F32)
    acc_ref[...] += jnp.dot((_silu(a) * b).astype(BF16), w2_ref[...], preferred_element_type=F32)

    @pl.when(j == pl.num_programs(1) - 1)
    def _():
        o_ref[...] = x_ref[...] + g2_ref[...] * acc_ref[...]


def _ffn(x, nw3, sc, sh, g2, w1, w3, w2, l, li, tm):
    M = x.shape[0]
    tf = 256
    ms = _mod_spec(sc.shape[0], tm, 2)
    return pl.pallas_call(
        _ffn_kernel, grid=(M // tm, D_FF // tf),
        in_specs=[BS((tm, D_MODEL), lambda i, j: (i, 0)),
                  BS((None, 1, D_MODEL), lambda i, j: (l, 0, 0)), ms, ms, ms,
                  BS((None, D_MODEL, tf), lambda i, j: (li, 0, j)),
                  BS((None, D_MODEL, tf), lambda i, j: (li, 0, j)),
                  BS((None, tf, D_MODEL), lambda i, j: (li, j, 0))],
        out_specs=BS((tm, D_MODEL), lambda i, j: (i, 0)),
        out_shape=SDS((M, D_MODEL), F32),
        scratch_shapes=[pltpu.VMEM((tm, D_MODEL), BF16), pltpu.VMEM((tm, D_MODEL), F32)],
        compiler_params=_cparams("arbitrary", "arbitrary"))(x, nw3, sc, sh, g2, w1, w3, w2)


def _moe_kernel(x_ref, nw_ref, sc_ref, sh_ref, g2_ref, wr_ref, w1_ref, w3_ref, w2_ref, o_ref,
                h_ref, gate_ref, acc_ref):
    e = pl.program_id(1)
    tm = x_ref.shape[0]
    lane = lax.broadcasted_iota(jnp.int32, (tm, 128), 1)

    @pl.when(e == 0)
    def _():
        h = _modulated_norm(x_ref[...], nw_ref[...], sc_ref[...], sh_ref[...])
        h_ref[...] = h.astype(BF16)
        router = jnp.dot(h.astype(BF16), wr_ref[...].astype(BF16), preferred_element_type=F32)
        logits = jnp.where(lane < N_EXPERTS, router, NEG)
        m1 = jnp.max(logits, axis=-1, keepdims=True)
        i1 = jnp.min(jnp.where(logits >= m1, lane, 128), axis=-1, keepdims=True)
        rest = jnp.where(lane == i1, NEG, logits)
        m2 = jnp.max(rest, axis=-1, keepdims=True)
        i2 = jnp.min(jnp.where(rest >= m2, lane, 128), axis=-1, keepdims=True)
        e2 = jnp.exp(m2 - m1)
        w_first = 1.0 / (1.0 + e2)
        gate_ref[...] = jnp.where(lane == i1, w_first, 0.0) + jnp.where(lane == i2, e2 * w_first, 0.0)
        acc_ref[...] = jnp.zeros_like(acc_ref)

    h = h_ref[...]
    a = jnp.dot(h, w1_ref[...], preferred_element_type=F32)
    b = jnp.dot(h, w3_ref[...], preferred_element_type=F32)
    y = jnp.dot((_silu(a) * b).astype(BF16), w2_ref[...], preferred_element_type=F32)
    ge = jnp.sum(jnp.where(lane == e, gate_ref[...], 0.0), axis=-1, keepdims=True)
    acc_ref[...] += ge * y

    @pl.when(e == pl.num_programs(1) - 1)
    def _():
        o_ref[...] = x_ref[...] + g2_ref[...] * acc_ref[...]


def _moe(x, nw3, sc, sh, g2, wr, w1, w3, w2, l, li, tm):
    M = x.shape[0]
    ms = _mod_spec(sc.shape[0], tm, 2)
    return pl.pallas_call(
        _moe_kernel, grid=(M // tm, N_EXPERTS),
        in_specs=[BS((tm, D_MODEL), lambda i, e: (i, 0)),
                  BS((None, 1, D_MODEL), lambda i, e: (l, 0, 0)), ms, ms, ms,
                  BS((None, D_MODEL, 128), lambda i, e: (li, 0, 0)),
                  BS((None, None, D_MODEL, D_FF_EXPERT), lambda i, e: (li, e, 0, 0)),
                  BS((None, None, D_MODEL, D_FF_EXPERT), lambda i, e: (li, e, 0, 0)),
                  BS((None, None, D_FF_EXPERT, D_MODEL), lambda i, e: (li, e, 0, 0))],
        out_specs=BS((tm, D_MODEL), lambda i, e: (i, 0)),
        out_shape=SDS((M, D_MODEL), F32),
        scratch_shapes=[pltpu.VMEM((tm, D_MODEL), BF16), pltpu.VMEM((tm, 128), F32),
                        pltpu.VMEM((tm, D_MODEL), F32)],
        compiler_params=_cparams("arbitrary", "arbitrary"))(x, nw3, sc, sh, g2, wr, w1, w3, w2)


def kernel(x_prompt, x_sample, cache_k, cache_v, state_gla, state_ret, page_table, c_prompt, c_sample,
           ada_w, ada_b, norm1, norm2, w_in, gla_gk_w2, gla_gk_b, gla_norm, ret_norm, moba_qnorm,
           moba_knorm, rel_bias, w_br_gla, w_br_ret, w_br_moba, w_out, ffn_w1, ffn_w3, ffn_w2,
           moe_router, moe_w1, moe_w3, moe_w2):
    depth = w_in.shape[0]
    bp, T, _ = x_prompt.shape
    nseq, nd, _ = x_sample.shape
    n_pages, page = page_table.shape[1], cache_k.shape[2]
    past_len = n_pages * page
    nb = T // MOBA_BLOCK
    assert bp == 1 and T % MOBA_BLOCK == 0 and nb <= MOBA_HD and page == 128
    assert past_len % MOBA_BLOCK == 0 and n_pages % PAGES_PER_STEP == 0 and nd <= 8
    ms_rows = nseq * nd
    nq = nd * MOBA_HEADS

    sizes = np.cumsum([QK_W, QK_W, V_W, GLA_LOWRANK, V_W, QK_W, QK_W, V_W, V_W, MOBA_W, MOBA_W, MOBA_W])
    glr0, glr1, mg0 = int(sizes[2]), int(sizes[3]), int(sizes[-1])
    w_main = jnp.concatenate([w_in[:, :, mg0:], w_in[:, :, :glr0], w_in[:, :, glr1:mg0]], axis=2).astype(BF16)
    w_glr = jnp.pad(w_in[:, :, glr0:glr1], ((0, 0), (0, 0), (0, 128 - GLA_LOWRANK))).astype(BF16)
    w2p = jnp.pad(gla_gk_w2, ((0, 0), (0, 128 - GLA_LOWRANK), (0, 0)))
    gkb3 = gla_gk_b[:, None, :]
    gla_nw3 = jnp.tile(gla_norm, (1, GLA_HEADS))[:, None, :]
    ret_nw3 = jnp.tile(ret_norm, (1, RET_HEADS))[:, None, :]
    qw3 = jnp.tile(moba_qnorm, (1, MOBA_HEADS))[:, None, :]
    kw3 = jnp.tile(moba_knorm, (1, MOBA_HEADS))[:, None, :]
    n1_3, n2_3 = norm1[:, None, :], norm2[:, None, :]
    ada_b3 = ada_b[:, None, :]
    wa, wb, wc, wo = (w.astype(BF16) for w in (w_br_gla, w_br_ret, w_br_moba, w_out))
    f1, f3, f2 = ffn_w1.astype(BF16), ffn_w3.astype(BF16), ffn_w2.astype(BF16)
    e1, e3, e2 = moe_w1.astype(BF16), moe_w3.astype(BF16), moe_w2.astype(BF16)
    wr = jnp.pad(moe_router, ((0, 0), (0, 0), (0, 128 - N_EXPERTS)))

    cos_p, sin_p = _rope_tables(0, T)
    cos_s, sin_s = _rope_tables(past_len, CHUNK)
    tq = np.arange(MOBA_BLOCK)
    d_own = tq[:, None] - tq[None, :]
    btab = _bias_tables(rel_bias, np.stack([d_own, d_own + MOBA_BLOCK]))
    qi = np.arange(8)
    d_past = (MOBA_BLOCK + qi[:, None] - tq[None, :])
    d_new = np.where((qi[None, :] < nd) & (qi[:, None] < nd), qi[:, None] - qi[None, :], -1)
    bt_past = _bias_tables(rel_bias, d_past)[:, :nd].transpose(1, 0, 2).reshape(nq, MOBA_BLOCK)
    bt_own = _bias_tables(rel_bias, np.pad(d_new, ((0, 0), (0, 120)), constant_values=-1))
    bt_own = bt_own[:, :nd, :8].transpose(1, 0, 2).reshape(nq, 8)
    expand = jnp.asarray(np.arange(past_len // MOBA_BLOCK)[:, None] == (np.arange(past_len) // MOBA_BLOCK)[None, :], BF16)
    head_of_lane = np.arange(MOBA_W) // MOBA_HD
    qh_mask = jnp.asarray(head_of_lane[None, :] == np.arange(MOBA_HEADS)[:, None], F32)
    ck4 = cache_k.reshape(depth, cache_k.shape[1], page, MOBA_W)
    cv4 = cache_v.reshape(depth, cache_v.shape[1], page, MOBA_W)

    c_all = jnp.concatenate([c_prompt, c_sample, jnp.zeros((-(bp + nseq) % 8, D_MODEL), F32)], axis=0)
    zero_gla = jnp.zeros((bp, GLA_HEADS, GLA_DK, GLA_DV), F32)
    zero_ret = jnp.zeros((bp, RET_HEADS, RET_DK, RET_DV), F32)

    xp = x_prompt.reshape(T, D_MODEL)
    xs = x_sample.reshape(ms_rows, D_MODEL)
    outs = {k: [] for k in ("kp", "vp", "gp", "rp", "ks", "vs", "gs", "rs")}
    for l in range(depth):
        mod = _ada_mod(c_all, ada_w, ada_b3, l)
        mp = [mod[0:1, j * D_MODEL:(j + 1) * D_MODEL] for j in range(6)]
        msm = [jnp.repeat(mod[bp:bp + nseq, j * D_MODEL:(j + 1) * D_MODEL], nd, axis=0) for j in range(6)]
        li = l // 2

        def channel(x, m, tm):
            if l % 2 == 0:
                return _ffn(x, n2_3, m[4], m[3], m[5], f1, f3, f2, l, li, tm)
            return _moe(x, n2_3, m[4], m[3], m[5], wr, e1, e3, e2, l, li, tm)

        P, G = _inproj(xp, n1_3, mp[1], mp[0], w_main, w_glr, l, 1024 if T % 1024 == 0 else MOBA_BLOCK)
        P3, G3 = P.reshape(1, T, N_MAIN), G.reshape(1, T, 128)
        oa, sg = _gla(P3, G3, w2p, gkb3, gla_nw3, zero_gla, l, CHUNK, CHUNK)
        ob, sr = _ret(P3, cos_p, sin_p, ret_nw3, zero_ret, l, CHUNK, CHUNK)
        qn, kn, kmean, kp, vb = _prompt_prep(P, qw3, kw3, l)
        qp = _gate(qn, _placed_block_means(kmean.reshape(nb, MOBA_W)))
        oc = _prompt_attention(qp, kp, vb, btab)
        xm = _merge(oa.reshape(T, V_W), ob.reshape(T, V_W), oc, P, xp, mp[2], wa, wb, wc, wo, l, 512 if T % 512 == 0 else MOBA_BLOCK)
        xp = channel(xm, mp, 512 if T % 512 == 0 else MOBA_BLOCK)
        outs["kp"].append(kn.reshape(bp, T, MOBA_HEADS, MOBA_HD))
        outs["vp"].append(P[:, COL_MV * MOBA_W:(COL_MV + 1) * MOBA_W].reshape(bp, T, MOBA_HEADS, MOBA_HD))
        outs["gp"].append(sg)
        outs["rp"].append(sr)

        Ps, Gs = _inproj(xs, n1_3, msm[1], msm[0], w_main, w_glr, l, ms_rows)
        Ps3, Gs3 = Ps.reshape(nseq, nd, N_MAIN), Gs.reshape(nseq, nd, 128)
        oas, sgs = _gla(Ps3, Gs3, w2p, gkb3, gla_nw3, state_gla[l], l, CHUNK, nd)
        obs, srs = _ret(Ps3, cos_s, sin_s, ret_nw3, state_ret[l], l, CHUNK, nd)
        qns, kns = _qknorm(Ps, qw3, kw3, l, ms_rows)
        vns = Ps[:, COL_MV * MOBA_W:(COL_MV + 1) * MOBA_W]
        qbd = ((qns * (MOBA_HD ** -0.5)).reshape(nseq, nd, 1, MOBA_W) * qh_mask[None, None]).reshape(nseq, nq, MOBA_W)
        kmean_s, logits = _kstream(page_table, ck4, qbd, l)
        pad8 = lambda a: jnp.pad(a.reshape(nseq, nd, MOBA_W), ((0, 0), (0, 8 - nd), (0, 0)))
        ocs_full = _vstream(page_table, cv4, logits, kmean_s, qbd, pad8(kns), pad8(vns), bt_past, bt_own, expand, l)
        ocs = jnp.sum(ocs_full.reshape(nseq, nd, MOBA_HEADS, MOBA_W) * qh_mask[None, None], axis=2)
        xms = _merge(oas.reshape(ms_rows, V_W), obs.reshape(ms_rows, V_W), ocs.reshape(ms_rows, MOBA_W).astype(BF16),
                     Ps, xs, msm[2], wa, wb, wc, wo, l, ms_rows)
        xs = channel(xms, msm, ms_rows)
        outs["ks"].append(kns.reshape(nseq, nd, MOBA_HEADS, MOBA_HD))
        outs["vs"].append(vns.reshape(nseq, nd, MOBA_HEADS, MOBA_HD))
        outs["gs"].append(sgs)
        outs["rs"].append(srs)

    st = lambda k: jnp.stack(outs[k])
    return (xp.reshape(bp, T, D_MODEL), xs.reshape(nseq, nd, D_MODEL), st("kp"), st("vp"), st("gp"), st("rp"),
            st("ks"), st("vs"), st("gs"), st("rs"))
```

```python
import functools
import math

import numpy as np
import jax
import jax.numpy as jnp
from jax import lax
from jax.experimental import pallas as pl
from jax.experimental.pallas import tpu as pltpu

F32 = jnp.float32
BF16 = jnp.bfloat16
BS = pl.BlockSpec
SDS = jax.ShapeDtypeStruct

D_MODEL = 1024
GLA_HEADS, GLA_DK, GLA_DV, GLA_LOWRANK, GLA_GATE_TEMP = 4, 64, 128, 16, 16.0
RET_HEADS, RET_DK, RET_DV = 4, 64, 128
ROPE_BASE = 10000.0
MOBA_HEADS, MOBA_HD, MOBA_BLOCK, MOBA_TOPK = 8, 64, 256, 3
N_BUCKETS, MAX_DISTANCE = 32, 128
D_FF, N_EXPERTS, TOP_K, D_FF_EXPERT = 2816, 8, 2, 1408
EPS = 1e-6
BIG = 1e30
NEG = -3.0e38
LOG2E = math.log2(math.e)

QK_W = GLA_HEADS * GLA_DK
V_W = GLA_HEADS * GLA_DV
MOBA_W = MOBA_HEADS * MOBA_HD
N_MAIN = 7680
COL_GQ, COL_GK, COL_RQ, COL_RK = 12, 13, 18, 19
COL_GV, COL_GR, COL_RV, COL_RG, COL_MQ, COL_MK, COL_MV = 7, 8, 10, 11, 12, 13, 14

VMEM_LIMIT_BYTES = 56 * 1024 * 1024
CHUNK = 128
ATTN_COLS = 128
ATTN_UNROLL = 4
PAGES_PER_STEP = 16


def _cparams(*sem):
    return pltpu.CompilerParams(dimension_semantics=sem, vmem_limit_bytes=VMEM_LIMIT_BYTES)


def _dg(a, b, dims):
    return lax.dot_general(a, b, (dims, ((), ())), preferred_element_type=F32)


NN = ((1,), (0,))
NT = ((1,), (1,))
TN = ((0,), (0,))


def _split2(x):
    h = x.astype(BF16)
    m = (x - h.astype(F32)).astype(BF16)
    return h, m


def _split3(x):
    h = x.astype(BF16)
    r = x - h.astype(F32)
    m = r.astype(BF16)
    l = (r - m.astype(F32)).astype(BF16)
    return h, m, l


def _dot_x3(a, b, dims=NN):
    ah, am = _split2(a)
    bh, bm = _split2(b)
    return _dg(ah, bh, dims) + _dg(ah, bm, dims) + _dg(am, bh, dims)


def _sigmoid(x):
    return 1.0 / (1.0 + jnp.exp(-x))


def _silu(x):
    return x * _sigmoid(x)


def _mod_kernel(c_ref, w_ref, b_ref, o_ref):
    o_ref[...] = jnp.dot(c_ref[...].astype(BF16), w_ref[...].astype(BF16),
                         preferred_element_type=F32) + b_ref[...]


def _ada_mod(c_all, ada_w, ada_b3, l):
    R = c_all.shape[0]
    TN_ = 1536
    n = 6 * D_MODEL
    return pl.pallas_call(
        _mod_kernel, grid=(n // TN_,),
        in_specs=[BS((R, D_MODEL), lambda j: (0, 0)),
                  BS((None, D_MODEL, TN_), lambda j: (l, 0, j)),
                  BS((None, 1, TN_), lambda j: (l, 0, j))],
        out_specs=BS((R, TN_), lambda j: (0, j)),
        out_shape=SDS((R, n), F32),
        compiler_params=_cparams("arbitrary"))(c_all, ada_w, ada_b3)


def _modulated_norm(x, nw, sc, sh):
    ms = jnp.mean(x * x, axis=-1, keepdims=True)
    return (x * lax.rsqrt(ms + EPS) * nw) * (1.0 + sc) + sh


def _inproj_kernel(x_ref, nw_ref, sc_ref, sh_ref, w_ref, wg_ref, o_ref, og_ref, h_ref):
    @pl.when(pl.program_id(1) == 0)
    def _():
        h = _modulated_norm(x_ref[...], nw_ref[...], sc_ref[...], sh_ref[...]).astype(BF16)
        h_ref[...] = h
        og_ref[...] = jnp.dot(h, wg_ref[...], preferred_element_type=F32)

    o_ref[...] = jnp.dot(h_ref[...], w_ref[...], preferred_element_type=F32)


def _mod_spec(mod_rows, tm, ngrid):
    if mod_rows == 1:
        return BS((1, D_MODEL), (lambda i, j: (0, 0)) if ngrid == 2 else (lambda i: (0, 0)))
    return BS((tm, D_MODEL), (lambda i, j: (i, 0)) if ngrid == 2 else (lambda i: (i, 0)))


def _inproj(x, nw3, sc, sh, w_main, w_glr, l, tm):
    M = x.shape[0]
    TN_ = 512
    ms = _mod_spec(sc.shape[0], tm, 2)
    return pl.pallas_call(
        _inproj_kernel, grid=(M // tm, N_MAIN // TN_),
        in_specs=[BS((tm, D_MODEL), lambda i, j: (i, 0)),
                  BS((None, 1, D_MODEL), lambda i, j: (l, 0, 0)),
                  ms, ms,
                  BS((None, D_MODEL, TN_), lambda i, j: (l, 0, j)),
                  BS((None, D_MODEL, 128), lambda i, j: (l, 0, 0))],
        out_specs=[BS((tm, TN_), lambda i, j: (i, j)),
                   BS((tm, 128), lambda i, j: (i, 0))],
        out_shape=[SDS((M, N_MAIN), F32), SDS((M, 128), F32)],
        scratch_shapes=[pltpu.VMEM((tm, D_MODEL), BF16)],
        compiler_params=_cparams("arbitrary", "arbitrary"))(x, nw3, sc, sh, w_main, w_glr)


def _load_rows(ref, pad_ref, rows, C):
    if rows == C:
        return ref[0]
    pad_ref[...] = jnp.zeros_like(pad_ref)
    pad_ref[0:rows, :] = ref[0]
    return pad_ref[...]


def _state_init(S_ref, s0_ref, nh, dk, dv):
    S_ref[...] = jnp.zeros_like(S_ref)
    for h in range(nh):
        S_ref[dk * h:dk * (h + 1), dv * h:dv * (h + 1)] = s0_ref[0, h]


def _state_out(so_ref, S_ref, nh, dk, dv):
    for h in range(nh):
        so_ref[0, h] = S_ref[dk * h:dk * (h + 1), dv * h:dv * (h + 1)]


def _block_diag_mask():
    r = lax.broadcasted_iota(jnp.int32, (QK_W, V_W), 0) // GLA_DK
    c = lax.broadcasted_iota(jnp.int32, (QK_W, V_W), 1) // GLA_DV
    return r == c


def _gla_kernel(q_ref, k_ref, v_ref, r_ref, g_ref, w2_ref, b_ref, nw_ref, s0_ref,
                o_ref, so_ref, S_ref, *pads, C, rows, nc):
    c = pl.program_id(1)

    @pl.when(c == 0)
    def _():
        _state_init(S_ref, s0_ref, GLA_HEADS, GLA_DK, GLA_DV)

    pads = list(pads) + [None] * 5
    q = _load_rows(q_ref, pads[0], rows, C)
    k = _load_rows(k_ref, pads[1], rows, C)
    v = _load_rows(v_ref, pads[2], rows, C)
    r = _load_rows(r_ref, pads[3], rows, C)
    g = _load_rows(g_ref, pads[4], rows, C)

    row = lax.broadcasted_iota(jnp.int32, (C, QK_W), 0)
    x = jnp.dot(g.astype(BF16), w2_ref[...].astype(BF16), preferred_element_type=F32) + b_ref[...]
    log_a = (jnp.minimum(x, 0.0) - jnp.log(1.0 + jnp.exp(-jnp.abs(x)))) * (1.0 / GLA_GATE_TEMP)
    if rows < C:
        log_a = jnp.where(row < rows, log_a, 0.0)

    ri = lax.broadcasted_iota(jnp.int32, (C, C), 0)
    ci = lax.broadcasted_iota(jnp.int32, (C, C), 1)
    causal = ri >= ci
    ltri = jnp.where(causal, 1.0, 0.0).astype(BF16)
    ones_c = jnp.ones((C, 128), BF16)
    pieces = _split3(log_a)
    b = sum(_dg(ltri, p, NN) for p in pieces)
    b_last_col = sum(_dg(p, ones_c, TN) for p in pieces)
    mid = max(min(rows, C) // 2, 1)
    b_ref_row = b[mid - 1:mid, :]
    b_last = b[C - 1:C, :]

    qs = q * (GLA_DK ** -0.5)
    qt_h, qt_m = _split2(qs * jnp.exp(b - b_ref_row))
    kt_h, kt_m = _split2(k * jnp.exp(b_ref_row - b))
    q_state = (qs * jnp.exp(b)).astype(BF16)
    k_state = (k * jnp.exp(b_last - b)).astype(BF16)
    vb = v.astype(BF16)

    S = S_ref[...]
    o_state = _dg(q_state, S.astype(BF16), NN)
    lane_head = lax.broadcasted_iota(jnp.int32, (C, QK_W), 1) // GLA_DK
    nw = nw_ref[...]
    for h in range(GLA_HEADS):
        mine = lane_head == h
        qh_h = jnp.where(mine, qt_h, jnp.zeros_like(qt_h))
        qh_m = jnp.where(mine, qt_m, jnp.zeros_like(qt_m))
        a = _dg(qh_h, kt_h, NT) + _dg(qh_h, kt_m, NT) + _dg(qh_m, kt_h, NT)
        a = jnp.where(causal, a, 0.0).astype(BF16)
        sl = slice(GLA_DV * h, GLA_DV * (h + 1))
        oh = _dg(a, vb[:, sl], NN) + o_state[:, sl]
        ms = jnp.mean(oh * oh, axis=-1, keepdims=True)
        y = oh * lax.rsqrt(ms + EPS) * nw[:, sl] * _silu(r[:, sl])
        o_ref[0, :, sl] = y[0:rows].astype(o_ref.dtype)

    u = _dg(k_state, vb, TN)
    e_col = jnp.exp(b_last_col)
    e_full = jnp.concatenate([e_col] * (V_W // 128), axis=1)
    S_ref[...] = e_full * S + jnp.where(_block_diag_mask(), u, 0.0)

    @pl.when(c == nc - 1)
    def _():
        _state_out(so_ref, S_ref, GLA_HEADS, GLA_DK, GLA_DV)


def _mixer_specs(rows):
    def col(width, idx):
        return BS((1, rows, width), lambda b, c: (b, c, idx))
    return col


def _gla(P3, G3, w2p, bias3, nw3, s0, l, C, rows):
    B, T, _ = P3.shape
    nc = T // rows
    col = _mixer_specs(rows)
    kern = functools.partial(_gla_kernel, C=C, rows=rows, nc=nc)
    scratch = [pltpu.VMEM((QK_W, V_W), F32)]
    if rows < C:
        scratch += [pltpu.VMEM((C, w), F32) for w in (QK_W, QK_W, V_W, V_W, 128)]
    return pl.pallas_call(
        kern, grid=(B, nc),
        in_specs=[col(QK_W, COL_GQ), col(QK_W, COL_GK), col(V_W, COL_GV), col(V_W, COL_GR),
                  BS((1, rows, 128), lambda b, c: (b, c, 0)),
                  BS((None, 128, QK_W), lambda b, c: (l, 0, 0)),
                  BS((None, 1, QK_W), lambda b, c: (l, 0, 0)),
                  BS((None, 1, V_W), lambda b, c: (l, 0, 0)),
                  BS((1, GLA_HEADS, GLA_DK, GLA_DV), lambda b, c: (b, 0, 0, 0))],
        out_specs=[BS((1, rows, V_W), lambda b, c: (b, c, 0)),
                   BS((1, GLA_HEADS, GLA_DK, GLA_DV), lambda b, c: (b, 0, 0, 0))],
        out_shape=[SDS((B, T, V_W), BF16), SDS((B, GLA_HEADS, GLA_DK, GLA_DV), F32)],
        scratch_shapes=scratch,
        compiler_params=_cparams("arbitrary", "arbitrary"))(P3, P3, P3, P3, G3, w2p, bias3, nw3, s0)


def _ret_kernel(q_ref, k_ref, v_ref, g_ref, cos_ref, sin_ref, dm_ref, rs_ref, ks_ref, cd_ref,
                nw_ref, s0_ref, o_ref, so_ref, S_ref, *pads, C, rows, nc):
    c = pl.program_id(1)

    @pl.when(c == 0)
    def _():
        _state_init(S_ref, s0_ref, RET_HEADS, RET_DK, RET_DV)

    pads = list(pads) + [None] * 4
    q = _load_rows(q_ref, pads[0], rows, C)
    k = _load_rows(k_ref, pads[1], rows, C)
    v = _load_rows(v_ref, pads[2], rows, C)
    g = _load_rows(g_ref, pads[3], rows, C)

    cos = cos_ref[...]
    sin = sin_ref[...]
    first_half = (lax.broadcasted_iota(jnp.int32, (C, QK_W), 1) % RET_DK) < (RET_DK // 2)

    def rope(x):
        partner = jnp.where(first_half, pltpu.roll(x, QK_W - RET_DK // 2, 1),
                            pltpu.roll(x, RET_DK // 2, 1))
        return x * cos + partner * sin

    qr = rope(q)
    kr = rope(k) * (RET_DK ** -0.5)
    qb = qr.astype(BF16)
    kb = kr.astype(BF16)
    vb = v.astype(BF16)
    k_state = (kr * ks_ref[...]).astype(BF16)

    S = S_ref[...]
    o_state = _dg(qb, S.astype(BF16), NN) * rs_ref[...]
    lane_head = lax.broadcasted_iota(jnp.int32, (C, QK_W), 1) // RET_DK
    nw = nw_ref[...]
    for h in range(RET_HEADS):
        a = _dg(jnp.where(lane_head == h, qb, jnp.zeros_like(qb)), kb, NT) * dm_ref[h]
        sl = slice(RET_DV * h, RET_DV * (h + 1))
        oh = _dg(a.astype(BF16), vb[:, sl], NN) + o_state[:, sl]
        oh = oh - jnp.mean(oh, axis=-1, keepdims=True)
        var = jnp.mean(oh * oh, axis=-1, keepdims=True)
        y = oh * lax.rsqrt(var + EPS) * nw[:, sl] * _silu(g[:, sl])
        o_ref[0, :, sl] = y[0:rows].astype(o_ref.dtype)

    u = _dg(k_state, vb, TN)
    S_ref[...] = cd_ref[...] * S + jnp.where(_block_diag_mask(), u, 0.0)

    @pl.when(c == nc - 1)
    def _():
        _state_out(so_ref, S_ref, RET_HEADS, RET_DK, RET_DV)


def _ret_constants(C, rows):
    log_gamma = np.log(1.0 - 2.0 ** (-5.0 - np.arange(RET_HEADS, dtype=np.float64)))
    bt = np.minimum(np.arange(C) + 1, rows).astype(np.float64)
    diff = bt[:, None] - bt[None, :]
    causal = np.arange(C)[:, None] >= np.arange(C)[None, :]
    dm = np.where(causal[None], np.exp(diff[None] * log_gamma[:, None, None]), 0.0)
    rs = np.exp(bt[:, None] * np.repeat(log_gamma, RET_DV)[None, :])
    ks = np.exp((bt[-1] - bt)[:, None] * np.repeat(log_gamma, RET_DK)[None, :])
    cd = np.exp(bt[-1] * np.repeat(log_gamma, RET_DV))[None, :]
    return (jnp.asarray(dm, F32), jnp.asarray(rs, F32), jnp.asarray(ks, F32), jnp.asarray(cd, F32))


def _ret(P3, cos_t, sin_t, nw3, s0, l, C, rows):
    B, T, _ = P3.shape
    nc = T // rows
    col = _mixer_specs(rows)
    dm, rs, ks, cd = _ret_constants(C, rows)
    kern = functools.partial(_ret_kernel, C=C, rows=rows, nc=nc)
    scratch = [pltpu.VMEM((QK_W, V_W), F32)]
    if rows < C:
        scratch += [pltpu.VMEM((C, w), F32) for w in (QK_W, QK_W, V_W, V_W)]
    const = lambda shape: BS(shape, lambda b, c: (0,) * len(shape))
    return pl.pallas_call(
        kern, grid=(B, nc),
        in_specs=[col(QK_W, COL_RQ), col(QK_W, COL_RK), col(V_W, COL_RV), col(V_W, COL_RG),
                  BS((C, QK_W), lambda b, c: (c, 0)), BS((C, QK_W), lambda b, c: (c, 0)),
                  const((RET_HEADS, C, C)), const((C, V_W)), const((C, QK_W)), const((1, V_W)),
                  BS((None, 1, V_W), lambda b, c: (l, 0, 0)),
                  BS((1, RET_HEADS, RET_DK, RET_DV), lambda b, c: (b, 0, 0, 0))],
        out_specs=[BS((1, rows, V_W), lambda b, c: (b, c, 0)),
                   BS((1, RET_HEADS, RET_DK, RET_DV), lambda b, c: (b, 0, 0, 0))],
        out_shape=[SDS((B, T, V_W), BF16), SDS((B, RET_HEADS, RET_DK, RET_DV), F32)],
        scratch_shapes=scratch,
        compiler_params=_cparams("arbitrary", "arbitrary"))(
            P3, P3, P3, P3, cos_t, sin_t, dm, rs, ks, cd, nw3, s0)


def _rope_tables(pos0, n):
    half = RET_DK // 2
    inv = ROPE_BASE ** (-jnp.arange(half, dtype=F32) / half)
    pos = (pos0 + jnp.arange(n, dtype=jnp.int32)).astype(F32)
    ang = pos[:, None] * inv[None, :]
    cos, sin = jnp.cos(ang), jnp.sin(ang)
    cos_t = jnp.tile(jnp.concatenate([cos, cos], -1), (1, RET_HEADS))
    sin_t = jnp.tile(jnp.concatenate([-sin, sin], -1), (1, RET_HEADS))
    return cos_t, sin_t


def _head_rms64(x, g_ref, w):
    xh, xm = _split2(x * x)
    ms = jnp.dot(xh, g_ref[...], preferred_element_type=F32) + jnp.dot(xm, g_ref[...], preferred_element_type=F32)
    return x * lax.rsqrt(ms + EPS) * w


def _qknorm_kernel(q_ref, k_ref, g_ref, qw_ref, kw_ref, qn_ref, kn_ref):
    qn_ref[...] = _head_rms64(q_ref[...], g_ref, qw_ref[...])
    kn_ref[...] = _head_rms64(k_ref[...], g_ref, kw_ref[...])


def _prompt_prep_kernel(q_ref, k_ref, v_ref, g_ref, pl_ref, eye_ref, qw_ref, kw_ref,
                        qn_ref, kn_ref, km_ref, kp_ref, vt_ref):
    i = pl.program_id(0)
    qn_ref[...] = _head_rms64(q_ref[...], g_ref, qw_ref[...])
    kn = _head_rms64(k_ref[...], g_ref, kw_ref[...])
    kn_ref[...] = kn
    km_ref[0] = jnp.mean(kn, axis=0, keepdims=True)
    lane = lax.broadcasted_iota(jnp.int32, (MOBA_BLOCK, 2 * MOBA_W), 1) % 128
    placed = jnp.dot(kn.astype(BF16), pl_ref[...], preferred_element_type=F32)
    kp_ref[...] = (placed + jnp.where(lane == MOBA_HD + i, BIG, 0.0)).astype(BF16)
    vt_ref[...] = _dg(eye_ref[...], v_ref[...].astype(BF16), NT).astype(BF16)


def _group_mean_matrix():
    r = np.arange(MOBA_W)
    return jnp.asarray((r[:, None] // MOBA_HD == r[None, :] // MOBA_HD) / MOBA_HD, BF16)


def _placement_matrix():
    r = np.arange(MOBA_W)
    c = np.arange(2 * MOBA_W)
    tgt = (r // MOBA_HD) * 128 + r % MOBA_HD
    return jnp.asarray(tgt[:, None] == c[None, :], BF16)


def _qknorm(P, qw3, kw3, l, tm):
    M = P.shape[0]
    return pl.pallas_call(
        _qknorm_kernel, grid=(M // tm,),
        in_specs=[BS((tm, MOBA_W), lambda i: (i, COL_MQ)), BS((tm, MOBA_W), lambda i: (i, COL_MK)),
                  BS((MOBA_W, MOBA_W), lambda i: (0, 0)),
                  BS((None, 1, MOBA_W), lambda i: (l, 0, 0)), BS((None, 1, MOBA_W), lambda i: (l, 0, 0))],
        out_specs=[BS((tm, MOBA_W), lambda i: (i, 0)), BS((tm, MOBA_W), lambda i: (i, 0))],
        out_shape=[SDS((M, MOBA_W), F32), SDS((M, MOBA_W), F32)],
        compiler_params=_cparams("arbitrary"))(P, P, _group_mean_matrix(), qw3, kw3)


def _prompt_prep(P, qw3, kw3, l):
    T = P.shape[0]
    nb = T // MOBA_BLOCK
    tm = MOBA_BLOCK
    row = lambda w, idx: BS((tm, w), lambda i: (i, idx))
    return pl.pallas_call(
        _prompt_prep_kernel, grid=(nb,),
        in_specs=[row(MOBA_W, COL_MQ), row(MOBA_W, COL_MK), row(MOBA_W, COL_MV),
                  BS((MOBA_W, MOBA_W), lambda i: (0, 0)),
                  BS((MOBA_W, 2 * MOBA_W), lambda i: (0, 0)),
                  BS((MOBA_W, MOBA_W), lambda i: (0, 0)),
                  BS((None, 1, MOBA_W), lambda i: (l, 0, 0)), BS((None, 1, MOBA_W), lambda i: (l, 0, 0))],
        out_specs=[row(MOBA_W, 0), row(MOBA_W, 0), BS((1, 1, MOBA_W), lambda i: (i, 0, 0)),
                   row(2 * MOBA_W, 0), BS((MOBA_W, tm), lambda i: (0, i))],
        out_shape=[SDS((T, MOBA_W), F32), SDS((T, MOBA_W), F32), SDS((nb, 1, MOBA_W), F32),
                   SDS((T, 2 * MOBA_W), BF16), SDS((MOBA_W, T), BF16)],
        compiler_params=_cparams("arbitrary"))(
            P, P, P, _group_mean_matrix(), _placement_matrix(), jnp.eye(MOBA_W, dtype=BF16), qw3, kw3)


def _placed_block_means(kmean):
    nb = kmean.shape[0]
    km = kmean.reshape(nb, MOBA_HEADS, MOBA_HD).transpose(1, 2, 0)
    km = jnp.pad(km, ((0, 0), (0, 0), (MOBA_HD, MOBA_HD - nb)))
    full = km[:, :, None, :] * jnp.eye(MOBA_HEADS, dtype=F32)[:, None, :, None]
    return full.reshape(MOBA_W, 2 * MOBA_W)


def _select_top3(gate, valid, axis=-1):
    g1 = jnp.where(valid, gate, NEG)
    m1 = jnp.max(g1, axis=axis, keepdims=True)
    g2 = jnp.where(g1 >= m1, NEG, g1)
    m2 = jnp.max(g2, axis=axis, keepdims=True)
    g3 = jnp.where(g2 >= m2, NEG, g2)
    m3 = jnp.max(g3, axis=axis, keepdims=True)
    return valid & (g1 >= m3)


def _gate_kernel(qn_ref, kmt_ref, plt_ref, qpt_ref):
    i = pl.program_id(0)
    qn = qn_ref[...]
    gate = _dg(kmt_ref[...].astype(BF16), qn.astype(BF16), NT)
    qpart = _dg(plt_ref[...], (qn * (MOBA_HD ** -0.5 * LOG2E)).astype(BF16), NT)
    blk = lax.broadcasted_iota(jnp.int32, (128, MOBA_BLOCK), 0) - MOBA_HD
    valid = (blk >= 0) & (blk < i)
    for h in range(MOBA_HEADS):
        sl = slice(128 * h, 128 * (h + 1))
        sel = _select_top3(gate[sl], valid, axis=0) | (blk == i)
        maskpart = jnp.where((blk >= 0) & jnp.logical_not(sel), -1.0, 0.0)
        qpt_ref[sl, :] = (qpart[sl] + maskpart).astype(BF16)


def _gate(qn, km_placed_t):
    T = qn.shape[0]
    nb = T // MOBA_BLOCK
    return pl.pallas_call(
        _gate_kernel, name="moba_gate", grid=(nb,),
        in_specs=[BS((MOBA_BLOCK, MOBA_W), lambda i: (i, 0)),
                  BS((2 * MOBA_W, MOBA_W), lambda i: (0, 0)),
                  BS((2 * MOBA_W, MOBA_W), lambda i: (0, 0))],
        out_specs=BS((2 * MOBA_W, MOBA_BLOCK), lambda i: (0, i)),
        out_shape=SDS((2 * MOBA_W, T), BF16),
        compiler_params=_cparams("arbitrary"))(qn, km_placed_t, _placement_matrix().T)


def _bucket_np(dist):
    dist = np.maximum(dist, 0)
    max_exact = N_BUCKETS // 2
    large = max_exact + (np.log(np.maximum(dist, max_exact).astype(np.float64) / max_exact)
                         / math.log(MAX_DISTANCE / max_exact) * (N_BUCKETS - max_exact)).astype(np.int64)
    large = np.minimum(large, N_BUCKETS - 1)
    return np.where(dist < max_exact, dist, large).astype(np.int32)


def _bias_kernel(rb_ref, idx_ref, ok_ref, o_ref, *, scale):
    h = pl.program_id(0)
    idx = idx_ref[...]
    far = rb_ref[N_BUCKETS - 1, h]
    acc = jnp.zeros(idx.shape, F32)
    for b in range(N_BUCKETS - 1):
        acc = jnp.where(idx == b, (rb_ref[b, h] - far) * scale, acc)
    o_ref[0] = jnp.where(ok_ref[...] > 0, acc, -BIG)


def _bias_tables(rel_bias, dist, scale=1.0):
    idx = jnp.asarray(_bucket_np(dist))
    ok = jnp.asarray((dist >= 0).astype(np.int32))
    shp = dist.shape
    zeros = (0,) * len(shp)
    return pl.pallas_call(
        functools.partial(_bias_kernel, scale=scale), name="moba_bias_tables", grid=(MOBA_HEADS,),
        in_specs=[BS(memory_space=pltpu.SMEM), BS(shp, lambda h: zeros), BS(shp, lambda h: zeros)],
        out_specs=BS((1,) + shp, lambda h: (h,) + zeros),
        out_shape=SDS((MOBA_HEADS,) + shp, F32),
        compiler_params=_cparams("arbitrary"))(rel_bias, idx, ok)


def _attn_kernel(qpt_ref, kp_ref, vt_ref, bt_ref, eye_ref, o_ref, m_ref, l_ref, acc_ref, s_ref):
    i = pl.program_id(1)
    tq = MOBA_BLOCK
    m_ref[...] = jnp.full(m_ref.shape, NEG, F32)
    l_ref[...] = jnp.zeros_like(l_ref)
    acc_ref[...] = jnp.zeros_like(acc_ref)

    def logits(slot, hh, n, t):
        qt = qpt_ref[128 * hh:128 * (hh + 1), :]
        start = pl.multiple_of(n * MOBA_BLOCK, MOBA_BLOCK)
        kblk = kp_ref[pl.ds(start, MOBA_BLOCK), 128 * hh:128 * (hh + 1)]
        s = jnp.dot(kblk, qt, preferred_element_type=F32)
        if t is not None:
            s = s + bt_ref[hh, t]
        s_ref[slot] = s

    def softmax_pv(slot, hh, n):
        ps, alphas = [], []
        for c in range(tq // ATTN_COLS):
            cols = slice(ATTN_COLS * c, ATTN_COLS * (c + 1))
            sc = s_ref[slot, :, cols]
            m_old = m_ref[hh, :, cols]
            m_new = jnp.maximum(m_old, jnp.max(sc, axis=0, keepdims=True))
            alpha = jnp.exp2(m_old - m_new)
            p = jnp.exp2(sc - m_new)
            l_ref[hh, :, cols] = alpha * l_ref[hh, :, cols] + jnp.sum(p, axis=0, keepdims=True)
            m_ref[hh, :, cols] = m_new
            ps.append(p.astype(BF16))
            alphas.append(alpha)
        start = pl.multiple_of(n * MOBA_BLOCK, MOBA_BLOCK)
        vt = vt_ref[MOBA_HD * hh:MOBA_HD * (hh + 1), pl.ds(start, MOBA_BLOCK)]
        pv = jnp.dot(vt, jnp.concatenate(ps, axis=1), preferred_element_type=F32)
        acc_ref[hh] = jnp.concatenate(alphas, axis=1) * acc_ref[hh] + pv

    def group(tiles):
        for slot, (hh, n, t) in enumerate(tiles):
            logits(slot, hh, n, t)
        for slot, (hh, n, t) in enumerate(tiles):
            softmax_pv(slot, hh, n)

    n_far = jnp.maximum(i - 1, 0)

    def far_tiles(first, count):
        return [(hh, first + b, None) for b in range(count) for hh in range(2)]

    def far_body(j, carry):
        group(far_tiles(ATTN_UNROLL * j, ATTN_UNROLL))
        return carry

    lax.fori_loop(0, n_far // ATTN_UNROLL, far_body, 0)
    done = (n_far // ATTN_UNROLL) * ATTN_UNROLL
    width = ATTN_UNROLL // 2
    while width >= 1:
        @pl.when((n_far - done) % (2 * width) >= width)
        def _(done=done, width=width):
            group(far_tiles(done, width))
        done = done + jnp.where((n_far - done) % (2 * width) >= width, width, 0)
        width //= 2

    @pl.when(i >= 1)
    def _():
        group([(0, i - 1, 1), (1, i - 1, 1), (0, i, 0), (1, i, 0)])

    @pl.when(i == 0)
    def _():
        group([(0, i, 0), (1, i, 0)])

    out_t = jnp.concatenate([acc_ref[0] / l_ref[0], acc_ref[1] / l_ref[1]], axis=0).astype(BF16)
    o_ref[...] = _dg(out_t, eye_ref[...], TN).astype(o_ref.dtype)


def _prompt_attention(qpt, kp, vt, btab_t):
    T = kp.shape[0]
    nb = T // MOBA_BLOCK
    return pl.pallas_call(
        _attn_kernel, name="moba_prompt_attention", grid=(MOBA_HEADS // 2, nb),
        in_specs=[BS((256, MOBA_BLOCK), lambda hp, i: (hp, i)),
                  BS((T, 256), lambda hp, i: (0, hp)),
                  BS((128, T), lambda hp, i: (hp, 0)),
                  BS((2, 2, MOBA_BLOCK, MOBA_BLOCK), lambda hp, i: (hp, 0, 0, 0)),
                  BS((128, 128), lambda hp, i: (0, 0))],
        out_specs=BS((MOBA_BLOCK, 128), lambda hp, i: (i, hp)),
        out_shape=SDS((T, MOBA_W), BF16),
        scratch_shapes=[pltpu.VMEM((2, 1, MOBA_BLOCK), F32), pltpu.VMEM((2, 1, MOBA_BLOCK), F32),
                        pltpu.VMEM((2, MOBA_HD, MOBA_BLOCK), F32),
                        pltpu.VMEM((2 * ATTN_UNROLL, MOBA_BLOCK, MOBA_BLOCK), F32)],
        compiler_params=_cparams("arbitrary", "arbitrary"))(qpt, kp, vt, btab_t, jnp.eye(128, dtype=BF16))


def _kstream_kernel(pt_ref, *refs):
    npg = PAGES_PER_STEP
    pages, (q_ref, km_ref, lg_ref) = refs[:npg], refs[npg:]
    q_h, q_m = _split2(q_ref[0])
    for r in range(npg):
        pg = pages[r][...]
        p_h, p_m = _split2(pg)
        lg_ref[0, :, 128 * r:128 * (r + 1)] = _dg(q_h, p_h, NT) + _dg(q_h, p_m, NT) + _dg(q_m, p_h, NT)
        s = jnp.sum(pg, axis=0, keepdims=True)
        if r % 2 == 0:
            prev = s
        else:
            km_ref[0, r // 2:r // 2 + 1, :] = (prev + s) * (1.0 / MOBA_BLOCK)


def _kstream(page_table, cache4, qbd, l):
    nseq, n_pages = page_table.shape
    npg = PAGES_PER_STEP
    ng = n_pages // npg
    page_specs = [BS((None, None, 128, MOBA_W), functools.partial(
        lambda b, g, pt, r: (l, pt[b, g * npg + r], 0, 0), r=r)) for r in range(npg)]
    nq = qbd.shape[1]
    grid_spec = pltpu.PrefetchScalarGridSpec(
        num_scalar_prefetch=1, grid=(nseq, ng),
        in_specs=page_specs + [BS((1, nq, MOBA_W), lambda b, g, pt: (b, 0, 0))],
        out_specs=[BS((1, npg // 2, MOBA_W), lambda b, g, pt: (b, g, 0)),
                   BS((1, nq, npg * 128), lambda b, g, pt: (b, 0, g))])
    return pl.pallas_call(
        _kstream_kernel, grid_spec=grid_spec,
        out_shape=[SDS((nseq, n_pages // 2, MOBA_W), F32), SDS((nseq, nq, n_pages * 128), F32)],
        compiler_params=_cparams("arbitrary", "arbitrary"))(page_table, *([cache4] * npg), qbd)


def _vstream_kernel(pt_ref, *refs, nblk):
    npg = PAGES_PER_STEP
    pages = refs[:npg]
    (lg_ref, km_ref, q_ref, kn_ref, vn_ref, bpast_ref, bown_ref, ex_ref, o_ref,
     p_ref, acc_ref, own_ref) = refs[npg:]
    g = pl.program_id(1)
    nq = q_ref.shape[1]

    @pl.when(g == 0)
    def _():
        q = q_ref[0]
        gate = _dg(q.astype(BF16), km_ref[0].astype(BF16), NT)
        lane = lax.broadcasted_iota(jnp.int32, (nq, nblk), 1)
        sel = _select_top3(gate, lane >= 0)
        selx = jnp.dot(jnp.where(sel, 1.0, 0.0).astype(BF16), ex_ref[...], preferred_element_type=F32)
        lg = lg_ref[0]
        L = lg.shape[1]
        lg = jnp.where(selx > 0.5, lg, -BIG)
        last = lg[:, L - MOBA_BLOCK:] + bpast_ref[...]
        s_own = _dg(q.astype(BF16), kn_ref[0].astype(BF16), NT) + bown_ref[...]
        m = jnp.maximum(jnp.maximum(jnp.max(lg[:, :L - MOBA_BLOCK], axis=-1, keepdims=True),
                                    jnp.max(last, axis=-1, keepdims=True)),
                        jnp.max(s_own, axis=-1, keepdims=True))
        p_far = jnp.exp(lg[:, :L - MOBA_BLOCK] - m)
        p_last = jnp.exp(last - m)
        p_own = jnp.exp(s_own - m)
        denom = (jnp.sum(p_far, axis=-1, keepdims=True) + jnp.sum(p_last, axis=-1, keepdims=True)
                 + jnp.sum(p_own, axis=-1, keepdims=True))
        inv = 1.0 / denom
        p_ref[:, :L - MOBA_BLOCK] = p_far * inv
        p_ref[:, L - MOBA_BLOCK:] = p_last * inv
        own_ref[...] = jnp.dot((p_own * inv).astype(BF16), vn_ref[0].astype(BF16), preferred_element_type=F32)
        acc_ref[...] = jnp.zeros_like(acc_ref)

    acc = acc_ref[...]
    for r in range(npg):
        start = pl.multiple_of((g * npg + r) * 128, 128)
        acc = acc + _dot_x3(p_ref[:, pl.ds(start, 128)], pages[r][...])
    acc_ref[...] = acc

    @pl.when(g == pl.num_programs(1) - 1)
    def _():
        o_ref[0] = acc_ref[...] + own_ref[...]


def _vstream(page_table, cache4, logits, kmean, qbd, kn_new, vn_new, b_past, b_own, expand, l):
    nseq, n_pages = page_table.shape
    npg = PAGES_PER_STEP
    ng = n_pages // npg
    nq = qbd.shape[1]
    nblk = n_pages // 2
    L = n_pages * 128
    page_specs = [BS((None, None, 128, MOBA_W), functools.partial(
        lambda b, g, pt, r: (l, pt[b, g * npg + r], 0, 0), r=r)) for r in range(npg)]
    per_seq = lambda shape: BS((1,) + shape, lambda b, g, pt: (b,) + (0,) * len(shape))
    const = lambda shape: BS(shape, lambda b, g, pt: (0,) * len(shape))
    grid_spec = pltpu.PrefetchScalarGridSpec(
        num_scalar_prefetch=1, grid=(nseq, ng),
        in_specs=page_specs + [per_seq((nq, L)), per_seq((nblk, MOBA_W)), per_seq((nq, MOBA_W)),
                               per_seq((8, MOBA_W)), per_seq((8, MOBA_W)),
                               const((nq, MOBA_BLOCK)), const((nq, 8)), const((nblk, L))],
        out_specs=per_seq((nq, MOBA_W)),
        scratch_shapes=[pltpu.VMEM((nq, L), F32), pltpu.VMEM((nq, MOBA_W), F32),
                        pltpu.VMEM((nq, MOBA_W), F32)])
    return pl.pallas_call(
        functools.partial(_vstream_kernel, nblk=nblk), grid_spec=grid_spec,
        out_shape=SDS((nseq, nq, MOBA_W), F32),
        compiler_params=_cparams("arbitrary", "arbitrary"))(
            page_table, *([cache4] * npg), logits, kmean, qbd, kn_new, vn_new, b_past, b_own, expand)


def _merge_kernel(oa_ref, ob_ref, oc_ref, mg_ref, x_ref, g1_ref, wa_ref, wb_ref, wc_ref, wo_ref, o_ref):
    mg = mg_ref[...]
    ya = jnp.dot(oa_ref[...], wa_ref[...], preferred_element_type=F32)
    yb = jnp.dot(ob_ref[...], wb_ref[...], preferred_element_type=F32)
    yc = jnp.dot(oc_ref[...], wc_ref[...], preferred_element_type=F32)
    merged = (_sigmoid(mg[:, 0:D_MODEL]) * ya + _sigmoid(mg[:, D_MODEL:2 * D_MODEL]) * yb
              + _sigmoid(mg[:, 2 * D_MODEL:]) * yc)
    mix = jnp.dot(merged.astype(BF16), wo_ref[...], preferred_element_type=F32)
    o_ref[...] = x_ref[...] + g1_ref[...] * mix


def _merge(oa, ob, oc, P, x, g1, wa, wb, wc, wo, l, tm):
    M = x.shape[0]
    row = lambda w: BS((tm, w), lambda i: (i, 0))
    wspec = lambda k: BS((None, k, D_MODEL), lambda i: (l, 0, 0))
    return pl.pallas_call(
        _merge_kernel, grid=(M // tm,),
        in_specs=[row(V_W), row(V_W), row(MOBA_W), row(3 * D_MODEL), row(D_MODEL),
                  _mod_spec(g1.shape[0], tm, 1),
                  wspec(V_W), wspec(V_W), wspec(MOBA_W), wspec(D_MODEL)],
        out_specs=row(D_MODEL),
        out_shape=SDS((M, D_MODEL), F32),
        compiler_params=_cparams("arbitrary"))(oa, ob, oc, P, x, g1, wa, wb, wc, wo)


def _ffn_kernel(x_ref, nw_ref, sc_ref, sh_ref, g2_ref, w1_ref, w3_ref, w2_ref, o_ref, h_ref, acc_ref):
    j = pl.program_id(1)

    @pl.when(j == 0)
    def _():
        h_ref[...] = _modulated_norm(x_ref[...], nw_ref[...], sc_ref[...], sh_ref[...]).astype(BF16)
        acc_ref[...] = jnp.zeros_like(acc_ref)

    h = h_ref[...]
    a = jnp.dot(h, w1_ref[...], preferred_element_type=F32)
    b = jnp.dot(h, w3_ref[...], preferred_element_type=F32)
    acc_ref[...] += jnp.dot((_silu(a) * b).astype(BF16), w2_ref[...], preferred_element_type=F32)

    @pl.when(j == pl.num_programs(1) - 1)
    def _():
        o_ref[...] = x_ref[...] + g2_ref[...] * acc_ref[...]


def _ffn(x, nw3, sc, sh, g2, w1, w3, w2, l, li, tm):
    M = x.shape[0]
    tf = 256
    ms = _mod_spec(sc.shape[0], tm, 2)
    return pl.pallas_call(
        _ffn_kernel, grid=(M // tm, D_FF // tf),
        in_specs=[BS((tm, D_MODEL), lambda i, j: (i, 0)),
                  BS((None, 1, D_MODEL), lambda i, j: (l, 0, 0)), ms, ms, ms,
                  BS((None, D_MODEL, tf), lambda i, j: (li, 0, j)),
                  BS((None, D_MODEL, tf), lambda i, j: (li, 0, j)),
                  BS((None, tf, D_MODEL), lambda i, j: (li, j, 0))],
        out_specs=BS((tm, D_MODEL), lambda i, j: (i, 0)),
        out_shape=SDS((M, D_MODEL), F32),
        scratch_shapes=[pltpu.VMEM((tm, D_MODEL), BF16), pltpu.VMEM((tm, D_MODEL), F32)],
        compiler_params=_cparams("arbitrary", "arbitrary"))(x, nw3, sc, sh, g2, w1, w3, w2)


def _moe_kernel(x_ref, nw_ref, sc_ref, sh_ref, g2_ref, wr_ref, w1_ref, w3_ref, w2_ref, o_ref,
                h_ref, gate_ref, acc_ref):
    e = pl.program_id(1)
    tm = x_ref.shape[0]
    lane = lax.broadcasted_iota(jnp.int32, (tm, 128), 1)

    @pl.when(e == 0)
    def _():
        h = _modulated_norm(x_ref[...], nw_ref[...], sc_ref[...], sh_ref[...])
        h_ref[...] = h.astype(BF16)
        router = jnp.dot(h.astype(BF16), wr_ref[...].astype(BF16), preferred_element_type=F32)
        logits = jnp.where(lane < N_EXPERTS, router, NEG)
        m1 = jnp.max(logits, axis=-1, keepdims=True)
        i1 = jnp.min(jnp.where(logits >= m1, lane, 128), axis=-1, keepdims=True)
        rest = jnp.where(lane == i1, NEG, logits)
        m2 = jnp.max(rest, axis=-1, keepdims=True)
        i2 = jnp.min(jnp.where(rest >= m2, lane, 128), axis=-1, keepdims=True)
        e2 = jnp.exp(m2 - m1)
        w_first = 1.0 / (1.0 + e2)
        gate_ref[...] = jnp.where(lane == i1, w_first, 0.0) + jnp.where(lane == i2, e2 * w_first, 0.0)
        acc_ref[...] = jnp.zeros_like(acc_ref)

    h = h_ref[...]
    a = jnp.dot(h, w1_ref[...], preferred_element_type=F32)
    b = jnp.dot(h, w3_ref[...], preferred_element_type=F32)
    y = jnp.dot((_silu(a) * b).astype(BF16), w2_ref[...], preferred_element_type=F32)
    ge = jnp.sum(jnp.where(lane == e, gate_ref[...], 0.0), axis=-1, keepdims=True)
    acc_ref[...] += ge * y

    @pl.when(e == pl.num_programs(1) - 1)
    def _():
        o_ref[...] = x_ref[...] + g2_ref[...] * acc_ref[...]


def _moe(x, nw3, sc, sh, g2, wr, w1, w3, w2, l, li, tm):
    M = x.shape[0]
    ms = _mod_spec(sc.shape[0], tm, 2)
    return pl.pallas_call(
        _moe_kernel, grid=(M // tm, N_EXPERTS),
        in_specs=[BS((tm, D_MODEL), lambda i, e: (i, 0)),
                  BS((None, 1, D_MODEL), lambda i, e: (l, 0, 0)), ms, ms, ms,
                  BS((None, D_MODEL, 128), lambda i, e: (li, 0, 0)),
                  BS((None, None, D_MODEL, D_FF_EXPERT), lambda i, e: (li, e, 0, 0)),
                  BS((None, None, D_MODEL, D_FF_EXPERT), lambda i, e: (li, e, 0, 0)),
                  BS((None, None, D_FF_EXPERT, D_MODEL), lambda i, e: (li, e, 0, 0))],
        out_specs=BS((tm, D_MODEL), lambda i, e: (i, 0)),
        out_shape=SDS((M, D_MODEL), F32),
        scratch_shapes=[pltpu.VMEM((tm, D_MODEL), BF16), pltpu.VMEM((tm, 128), F32),
                        pltpu.VMEM((tm, D_MODEL), F32)],
        compiler_params=_cparams("arbitrary", "arbitrary"))(x, nw3, sc, sh, g2, wr, w1, w3, w2)


def kernel(x_prompt, x_sample, cache_k, cache_v, state_gla, state_ret, page_table, c_prompt, c_sample,
           ada_w, ada_b, norm1, norm2, w_in, gla_gk_w2, gla_gk_b, gla_norm, ret_norm, moba_qnorm,
           moba_knorm, rel_bias, w_br_gla, w_br_ret, w_br_moba, w_out, ffn_w1, ffn_w3, ffn_w2,
           moe_router, moe_w1, moe_w3, moe_w2):
    depth = w_in.shape[0]
    bp, T, _ = x_prompt.shape
    nseq, nd, _ = x_sample.shape
    n_pages, page = page_table.shape[1], cache_k.shape[2]
    past_len = n_pages * page
    nb = T // MOBA_BLOCK
    assert bp == 1 and T % MOBA_BLOCK == 0 and nb <= MOBA_HD and page == 128
    assert past_len % MOBA_BLOCK == 0 and n_pages % PAGES_PER_STEP == 0 and nd <= 8
    ms_rows = nseq * nd
    nq = nd * MOBA_HEADS

    sizes = np.cumsum([QK_W, QK_W, V_W, GLA_LOWRANK, V_W, QK_W, QK_W, V_W, V_W, MOBA_W, MOBA_W, MOBA_W])
    glr0, glr1, mg0 = int(sizes[2]), int(sizes[3]), int(sizes[-1])
    w_main = jnp.concatenate([w_in[:, :, mg0:], w_in[:, :, :glr0], w_in[:, :, glr1:mg0]], axis=2).astype(BF16)
    w_glr = jnp.pad(w_in[:, :, glr0:glr1], ((0, 0), (0, 0), (0, 128 - GLA_LOWRANK))).astype(BF16)
    w2p = jnp.pad(gla_gk_w2, ((0, 0), (0, 128 - GLA_LOWRANK), (0, 0)))
    gkb3 = gla_gk_b[:, None, :]
    gla_nw3 = jnp.tile(gla_norm, (1, GLA_HEADS))[:, None, :]
    ret_nw3 = jnp.tile(ret_norm, (1, RET_HEADS))[:, None, :]
    qw3 = jnp.tile(moba_qnorm, (1, MOBA_HEADS))[:, None, :]
    kw3 = jnp.tile(moba_knorm, (1, MOBA_HEADS))[:, None, :]
    n1_3, n2_3 = norm1[:, None, :], norm2[:, None, :]
    ada_b3 = ada_b[:, None, :]
    wa, wb, wc, wo = (w.astype(BF16) for w in (w_br_gla, w_br_ret, w_br_moba, w_out))
    f1, f3, f2 = ffn_w1.astype(BF16), ffn_w3.astype(BF16), ffn_w2.astype(BF16)
    e1, e3, e2 = moe_w1.astype(BF16), moe_w3.astype(BF16), moe_w2.astype(BF16)
    wr = jnp.pad(moe_router, ((0, 0), (0, 0), (0, 128 - N_EXPERTS)))

    cos_p, sin_p = _rope_tables(0, T)
    cos_s, sin_s = _rope_tables(past_len, CHUNK)
    tq = np.arange(MOBA_BLOCK)
    d_own = tq[:, None] - tq[None, :]
    btab = _bias_tables(rel_bias, np.stack([d_own.T, d_own.T + MOBA_BLOCK]), LOG2E)
    qi = np.arange(8)
    d_past = (MOBA_BLOCK + qi[:, None] - tq[None, :])
    d_new = np.where((qi[None, :] < nd) & (qi[:, None] < nd), qi[:, None] - qi[None, :], -1)
    bt_past = _bias_tables(rel_bias, d_past)[:, :nd].transpose(1, 0, 2).reshape(nq, MOBA_BLOCK)
    bt_own = _bias_tables(rel_bias, np.pad(d_new, ((0, 0), (0, 120)), constant_values=-1))
    bt_own = bt_own[:, :nd, :8].transpose(1, 0, 2).reshape(nq, 8)
    expand = jnp.asarray(np.arange(past_len // MOBA_BLOCK)[:, None] == (np.arange(past_len) // MOBA_BLOCK)[None, :], BF16)
    head_of_lane = np.arange(MOBA_W) // MOBA_HD
    qh_mask = jnp.asarray(head_of_lane[None, :] == np.arange(MOBA_HEADS)[:, None], F32)
    ck4 = cache_k.reshape(depth, cache_k.shape[1], page, MOBA_W)
    cv4 = cache_v.reshape(depth, cache_v.shape[1], page, MOBA_W)

    c_all = jnp.concatenate([c_prompt, c_sample, jnp.zeros((-(bp + nseq) % 8, D_MODEL), F32)], axis=0)
    zero_gla = jnp.zeros((bp, GLA_HEADS, GLA_DK, GLA_DV), F32)
    zero_ret = jnp.zeros((bp, RET_HEADS, RET_DK, RET_DV), F32)

    xp = x_prompt.reshape(T, D_MODEL)
    xs = x_sample.reshape(ms_rows, D_MODEL)
    outs = {k: [] for k in ("kp", "vp", "gp", "rp", "ks", "vs", "gs", "rs")}
    for l in range(depth):
        mod = _ada_mod(c_all, ada_w, ada_b3, l)
        mp = [mod[0:1, j * D_MODEL:(j + 1) * D_MODEL] for j in range(6)]
        msm = [jnp.repeat(mod[bp:bp + nseq, j * D_MODEL:(j + 1) * D_MODEL], nd, axis=0) for j in range(6)]
        li = l // 2

        def channel(x, m, tm):
            if l % 2 == 0:
                return _ffn(x, n2_3, m[4], m[3], m[5], f1, f3, f2, l, li, tm)
            return _moe(x, n2_3, m[4], m[3], m[5], wr, e1, e3, e2, l, li, tm)

        P, G = _inproj(xp, n1_3, mp[1], mp[0], w_main, w_glr, l, 1024 if T % 1024 == 0 else MOBA_BLOCK)
        P3, G3 = P.reshape(1, T, N_MAIN), G.reshape(1, T, 128)
        oa, sg = _gla(P3, G3, w2p, gkb3, gla_nw3, zero_gla, l, CHUNK, CHUNK)
        ob, sr = _ret(P3, cos_p, sin_p, ret_nw3, zero_ret, l, CHUNK, CHUNK)
        qn, kn, kmean, kp, vt = _prompt_prep(P, qw3, kw3, l)
        qpt = _gate(qn, _placed_block_means(kmean.reshape(nb, MOBA_W)).T)
        oc = _prompt_attention(qpt, kp, vt, btab)
        xm = _merge(oa.reshape(T, V_W), ob.reshape(T, V_W), oc, P, xp, mp[2], wa, wb, wc, wo, l, 512 if T % 512 == 0 else MOBA_BLOCK)
        xp = channel(xm, mp, 512 if T % 512 == 0 else MOBA_BLOCK)
        outs["kp"].append(kn.reshape(bp, T, MOBA_HEADS, MOBA_HD))
        outs["vp"].append(P[:, COL_MV * MOBA_W:(COL_MV + 1) * MOBA_W].reshape(bp, T, MOBA_HEADS, MOBA_HD))
        outs["gp"].append(sg)
        outs["rp"].append(sr)

        Ps, Gs = _inproj(xs, n1_3, msm[1], msm[0], w_main, w_glr, l, ms_rows)
        Ps3, Gs3 = Ps.reshape(nseq, nd, N_MAIN), Gs.reshape(nseq, nd, 128)
        oas, sgs = _gla(Ps3, Gs3, w2p, gkb3, gla_nw3, state_gla[l], l, CHUNK, nd)
        obs, srs = _ret(Ps3, cos_s, sin_s, ret_nw3, state_ret[l], l, CHUNK, nd)
        qns, kns = _qknorm(Ps, qw3, kw3, l, ms_rows)
        vns = Ps[:, COL_MV * MOBA_W:(COL_MV + 1) * MOBA_W]
        qbd = ((qns * (MOBA_HD ** -0.5)).reshape(nseq, nd, 1, MOBA_W) * qh_mask[None, None]).reshape(nseq, nq, MOBA_W)
        kmean_s, logits = _kstream(page_table, ck4, qbd, l)
        pad8 = lambda a: jnp.pad(a.reshape(nseq, nd, MOBA_W), ((0, 0), (0, 8 - nd), (0, 0)))
        ocs_full = _vstream(page_table, cv4, logits, kmean_s, qbd, pad8(kns), pad8(vns), bt_past, bt_own, expand, l)
        ocs = jnp.sum(ocs_full.reshape(nseq, nd, MOBA_HEADS, MOBA_W) * qh_mask[None, None], axis=2)
        xms = _merge(oas.reshape(ms_rows, V_W), obs.reshape(ms_rows, V_W), ocs.reshape(ms_rows, MOBA_W).astype(BF16),
                     Ps, xs, msm[2], wa, wb, wc, wo, l, ms_rows)
        xs = channel(xms, msm, ms_rows)
        outs["ks"].append(kns.reshape(nseq, nd, MOBA_HEADS, MOBA_HD))
        outs["vs"].append(vns.reshape(nseq, nd, MOBA_HEADS, MOBA_HD))
        outs["gs"].append(sgs)
        outs["rs"].append(srs)

    st = lambda k: jnp.stack(outs[k])
    return (xp.reshape(bp, T, D_MODEL), xs.reshape(nseq, nd, D_MODEL), st("kp"), st("vp"), st("gp"), st("rp"),
            st("ks"), st("vs"), st("gs"), st("rs"))
```

```python
import functools
import math

import numpy as np
import jax
import jax.numpy as jnp
from jax import lax
from jax.experimental import pallas as pl
from jax.experimental.pallas import tpu as pltpu

F32 = jnp.float32
BF16 = jnp.bfloat16
BS = pl.BlockSpec
SDS = jax.ShapeDtypeStruct

D_MODEL = 1024
GLA_HEADS, GLA_DK, GLA_DV, GLA_LOWRANK, GLA_GATE_TEMP = 4, 64, 128, 16, 16.0
RET_HEADS, RET_DK, RET_DV = 4, 64, 128
ROPE_BASE = 10000.0
MOBA_HEADS, MOBA_HD, MOBA_BLOCK, MOBA_TOPK = 8, 64, 256, 3
N_BUCKETS, MAX_DISTANCE = 32, 128
D_FF, N_EXPERTS, TOP_K, D_FF_EXPERT = 2816, 8, 2, 1408
EPS = 1e-6
BIG = 1e30
NEG = -3.0e38
LOG2E = math.log2(math.e)

QK_W = GLA_HEADS * GLA_DK
V_W = GLA_HEADS * GLA_DV
MOBA_W = MOBA_HEADS * MOBA_HD
N_MAIN = 7680
COL_GQ, COL_GK, COL_RQ, COL_RK = 12, 13, 18, 19
COL_GV, COL_GR, COL_RV, COL_RG, COL_MQ, COL_MK, COL_MV = 7, 8, 10, 11, 12, 13, 14

VMEM_LIMIT_BYTES = 56 * 1024 * 1024
CHUNK = 128
ATTN_COLS = 128
ATTN_UNROLL = 8
PAGES_PER_STEP = 16
DECODE_ROWS = 4


def _cparams(*sem):
    return pltpu.CompilerParams(dimension_semantics=sem, vmem_limit_bytes=VMEM_LIMIT_BYTES)


def _dg(a, b, dims):
    return lax.dot_general(a, b, (dims, ((), ())), preferred_element_type=F32)


NN = ((1,), (0,))
NT = ((1,), (1,))
TN = ((0,), (0,))


def _split2(x):
    h = x.astype(BF16)
    m = (x - h.astype(F32)).astype(BF16)
    return h, m


def _split3(x):
    h = x.astype(BF16)
    r = x - h.astype(F32)
    m = r.astype(BF16)
    l = (r - m.astype(F32)).astype(BF16)
    return h, m, l


def _dot_x3(a, b, dims=NN):
    ah, am = _split2(a)
    bh, bm = _split2(b)
    return _dg(ah, bh, dims) + _dg(ah, bm, dims) + _dg(am, bh, dims)


def _sigmoid(x):
    return 1.0 / (1.0 + jnp.exp(-x))


def _silu(x):
    return x * _sigmoid(x)


def _mod_kernel(c_ref, w_ref, b_ref, o_ref):
    o_ref[...] = jnp.dot(c_ref[...].astype(BF16), w_ref[...].astype(BF16),
                         preferred_element_type=F32) + b_ref[...]


def _ada_mod(c_all, ada_w, ada_b3, l):
    R = c_all.shape[0]
    TN_ = 1536
    n = 6 * D_MODEL
    return pl.pallas_call(
        _mod_kernel, grid=(n // TN_,),
        in_specs=[BS((R, D_MODEL), lambda j: (0, 0)),
                  BS((None, D_MODEL, TN_), lambda j: (l, 0, j)),
                  BS((None, 1, TN_), lambda j: (l, 0, j))],
        out_specs=BS((R, TN_), lambda j: (0, j)),
        out_shape=SDS((R, n), F32),
        compiler_params=_cparams("arbitrary"))(c_all, ada_w, ada_b3)


def _modulated_norm(x, nw, sc, sh):
    ms = jnp.mean(x * x, axis=-1, keepdims=True)
    return (x * lax.rsqrt(ms + EPS) * nw) * (1.0 + sc) + sh


def _inproj_kernel(x_ref, nw_ref, sc_ref, sh_ref, w_ref, wg_ref, o_ref, og_ref, h_ref):
    @pl.when(pl.program_id(1) == 0)
    def _():
        h = _modulated_norm(x_ref[...], nw_ref[...], sc_ref[...], sh_ref[...]).astype(BF16)
        h_ref[...] = h
        og_ref[...] = jnp.dot(h, wg_ref[...], preferred_element_type=F32)

    o_ref[...] = jnp.dot(h_ref[...], w_ref[...], preferred_element_type=F32)


def _mod_spec(mod_rows, tm, ngrid):
    if mod_rows == 1:
        return BS((1, D_MODEL), (lambda i, j: (0, 0)) if ngrid == 2 else (lambda i: (0, 0)))
    return BS((tm, D_MODEL), (lambda i, j: (i, 0)) if ngrid == 2 else (lambda i: (i, 0)))


def _inproj(x, nw3, sc, sh, w_main, w_glr, l, tm):
    M = x.shape[0]
    TN_ = 512
    ms = _mod_spec(sc.shape[0], tm, 2)
    return pl.pallas_call(
        _inproj_kernel, grid=(M // tm, N_MAIN // TN_),
        in_specs=[BS((tm, D_MODEL), lambda i, j: (i, 0)),
                  BS((None, 1, D_MODEL), lambda i, j: (l, 0, 0)),
                  ms, ms,
                  BS((None, D_MODEL, TN_), lambda i, j: (l, 0, j)),
                  BS((None, D_MODEL, 128), lambda i, j: (l, 0, 0))],
        out_specs=[BS((tm, TN_), lambda i, j: (i, j)),
                   BS((tm, 128), lambda i, j: (i, 0))],
        out_shape=[SDS((M, N_MAIN), F32), SDS((M, 128), F32)],
        scratch_shapes=[pltpu.VMEM((tm, D_MODEL), BF16)],
        compiler_params=_cparams("arbitrary", "arbitrary"))(x, nw3, sc, sh, w_main, w_glr)


def _load_rows(ref, pad_ref, rows, C):
    if rows == C:
        return ref[0]
    pad_ref[...] = jnp.zeros_like(pad_ref)
    pad_ref[0:rows, :] = ref[0]
    return pad_ref[...]


def _state_init(S_ref, s0_ref, nh, dk, dv):
    S_ref[...] = jnp.zeros_like(S_ref)
    for h in range(nh):
        S_ref[dk * h:dk * (h + 1), dv * h:dv * (h + 1)] = s0_ref[0, h]


def _state_out(so_ref, S_ref, nh, dk, dv):
    for h in range(nh):
        so_ref[0, h] = S_ref[dk * h:dk * (h + 1), dv * h:dv * (h + 1)]


def _block_diag_mask():
    r = lax.broadcasted_iota(jnp.int32, (QK_W, V_W), 0) // GLA_DK
    c = lax.broadcasted_iota(jnp.int32, (QK_W, V_W), 1) // GLA_DV
    return r == c


def _gla_kernel(q_ref, k_ref, v_ref, r_ref, g_ref, w2_ref, b_ref, nw_ref, s0_ref,
                o_ref, so_ref, S_ref, *pads, C, rows, nc):
    c = pl.program_id(1)

    @pl.when(c == 0)
    def _():
        _state_init(S_ref, s0_ref, GLA_HEADS, GLA_DK, GLA_DV)

    pads = list(pads) + [None] * 5
    q = _load_rows(q_ref, pads[0], rows, C)
    k = _load_rows(k_ref, pads[1], rows, C)
    v = _load_rows(v_ref, pads[2], rows, C)
    r = _load_rows(r_ref, pads[3], rows, C)
    g = _load_rows(g_ref, pads[4], rows, C)

    row = lax.broadcasted_iota(jnp.int32, (C, QK_W), 0)
    x = jnp.dot(g.astype(BF16), w2_ref[...].astype(BF16), preferred_element_type=F32) + b_ref[...]
    log_a = (jnp.minimum(x, 0.0) - jnp.log(1.0 + jnp.exp(-jnp.abs(x)))) * (1.0 / GLA_GATE_TEMP)
    if rows < C:
        log_a = jnp.where(row < rows, log_a, 0.0)

    ri = lax.broadcasted_iota(jnp.int32, (C, C), 0)
    ci = lax.broadcasted_iota(jnp.int32, (C, C), 1)
    causal = ri >= ci
    ltri = jnp.where(causal, 1.0, 0.0).astype(BF16)
    ones_c = jnp.ones((C, 128), BF16)
    pieces = _split3(log_a)
    b = sum(_dg(ltri, p, NN) for p in pieces)
    b_last_col = sum(_dg(p, ones_c, TN) for p in pieces)
    mid = max(min(rows, C) // 2, 1)
    b_ref_row = b[mid - 1:mid, :]
    b_last = b[C - 1:C, :]

    qs = q * (GLA_DK ** -0.5)
    qt_h, qt_m = _split2(qs * jnp.exp(b - b_ref_row))
    kt_h, kt_m = _split2(k * jnp.exp(b_ref_row - b))
    q_state = (qs * jnp.exp(b)).astype(BF16)
    k_state = (k * jnp.exp(b_last - b)).astype(BF16)
    vb = v.astype(BF16)

    S = S_ref[...]
    o_state = _dg(q_state, S.astype(BF16), NN)
    lane_head = lax.broadcasted_iota(jnp.int32, (C, QK_W), 1) // GLA_DK
    nw = nw_ref[...]
    for h in range(GLA_HEADS):
        mine = lane_head == h
        qh_h = jnp.where(mine, qt_h, jnp.zeros_like(qt_h))
        qh_m = jnp.where(mine, qt_m, jnp.zeros_like(qt_m))
        a = _dg(qh_h, kt_h, NT) + _dg(qh_h, kt_m, NT) + _dg(qh_m, kt_h, NT)
        a = jnp.where(causal, a, 0.0).astype(BF16)
        sl = slice(GLA_DV * h, GLA_DV * (h + 1))
        oh = _dg(a, vb[:, sl], NN) + o_state[:, sl]
        ms = jnp.mean(oh * oh, axis=-1, keepdims=True)
        y = oh * lax.rsqrt(ms + EPS) * nw[:, sl] * _silu(r[:, sl])
        o_ref[0, :, sl] = y[0:rows].astype(o_ref.dtype)

    u = _dg(k_state, vb, TN)
    e_col = jnp.exp(b_last_col)
    e_full = jnp.concatenate([e_col] * (V_W // 128), axis=1)
    S_ref[...] = e_full * S + jnp.where(_block_diag_mask(), u, 0.0)

    @pl.when(c == nc - 1)
    def _():
        _state_out(so_ref, S_ref, GLA_HEADS, GLA_DK, GLA_DV)


def _mixer_specs(rows):
    def col(width, idx):
        return BS((1, rows, width), lambda b, c: (b, c, idx))
    return col


def _gla(P3, G3, w2p, bias3, nw3, s0, l, C, rows):
    B, T, _ = P3.shape
    nc = T // rows
    col = _mixer_specs(rows)
    kern = functools.partial(_gla_kernel, C=C, rows=rows, nc=nc)
    scratch = [pltpu.VMEM((QK_W, V_W), F32)]
    if rows < C:
        scratch += [pltpu.VMEM((C, w), F32) for w in (QK_W, QK_W, V_W, V_W, 128)]
    return pl.pallas_call(
        kern, grid=(B, nc),
        in_specs=[col(QK_W, COL_GQ), col(QK_W, COL_GK), col(V_W, COL_GV), col(V_W, COL_GR),
                  BS((1, rows, 128), lambda b, c: (b, c, 0)),
                  BS((None, 128, QK_W), lambda b, c: (l, 0, 0)),
                  BS((None, 1, QK_W), lambda b, c: (l, 0, 0)),
                  BS((None, 1, V_W), lambda b, c: (l, 0, 0)),
                  BS((1, GLA_HEADS, GLA_DK, GLA_DV), lambda b, c: (b, 0, 0, 0))],
        out_specs=[BS((1, rows, V_W), lambda b, c: (b, c, 0)),
                   BS((1, GLA_HEADS, GLA_DK, GLA_DV), lambda b, c: (b, 0, 0, 0))],
        out_shape=[SDS((B, T, V_W), BF16), SDS((B, GLA_HEADS, GLA_DK, GLA_DV), F32)],
        scratch_shapes=scratch,
        compiler_params=_cparams("arbitrary", "arbitrary"))(P3, P3, P3, P3, G3, w2p, bias3, nw3, s0)


def _ret_kernel(q_ref, k_ref, v_ref, g_ref, cos_ref, sin_ref, dm_ref, rs_ref, ks_ref, cd_ref,
                nw_ref, s0_ref, o_ref, so_ref, S_ref, *pads, C, rows, nc):
    c = pl.program_id(1)

    @pl.when(c == 0)
    def _():
        _state_init(S_ref, s0_ref, RET_HEADS, RET_DK, RET_DV)

    pads = list(pads) + [None] * 4
    q = _load_rows(q_ref, pads[0], rows, C)
    k = _load_rows(k_ref, pads[1], rows, C)
    v = _load_rows(v_ref, pads[2], rows, C)
    g = _load_rows(g_ref, pads[3], rows, C)

    cos = cos_ref[...]
    sin = sin_ref[...]
    first_half = (lax.broadcasted_iota(jnp.int32, (C, QK_W), 1) % RET_DK) < (RET_DK // 2)

    def rope(x):
        partner = jnp.where(first_half, pltpu.roll(x, QK_W - RET_DK // 2, 1),
                            pltpu.roll(x, RET_DK // 2, 1))
        return x * cos + partner * sin

    qr = rope(q)
    kr = rope(k) * (RET_DK ** -0.5)
    qb = qr.astype(BF16)
    kb = kr.astype(BF16)
    vb = v.astype(BF16)
    k_state = (kr * ks_ref[...]).astype(BF16)

    S = S_ref[...]
    o_state = _dg(qb, S.astype(BF16), NN) * rs_ref[...]
    lane_head = lax.broadcasted_iota(jnp.int32, (C, QK_W), 1) // RET_DK
    nw = nw_ref[...]
    for h in range(RET_HEADS):
        a = _dg(jnp.where(lane_head == h, qb, jnp.zeros_like(qb)), kb, NT) * dm_ref[h]
        sl = slice(RET_DV * h, RET_DV * (h + 1))
        oh = _dg(a.astype(BF16), vb[:, sl], NN) + o_state[:, sl]
        oh = oh - jnp.mean(oh, axis=-1, keepdims=True)
        var = jnp.mean(oh * oh, axis=-1, keepdims=True)
        y = oh * lax.rsqrt(var + EPS) * nw[:, sl] * _silu(g[:, sl])
        o_ref[0, :, sl] = y[0:rows].astype(o_ref.dtype)

    u = _dg(k_state, vb, TN)
    S_ref[...] = cd_ref[...] * S + jnp.where(_block_diag_mask(), u, 0.0)

    @pl.when(c == nc - 1)
    def _():
        _state_out(so_ref, S_ref, RET_HEADS, RET_DK, RET_DV)


def _ret_constants(C, rows):
    log_gamma = np.log(1.0 - 2.0 ** (-5.0 - np.arange(RET_HEADS, dtype=np.float64)))
    bt = np.minimum(np.arange(C) + 1, rows).astype(np.float64)
    diff = bt[:, None] - bt[None, :]
    causal = np.arange(C)[:, None] >= np.arange(C)[None, :]
    dm = np.where(causal[None], np.exp(diff[None] * log_gamma[:, None, None]), 0.0)
    rs = np.exp(bt[:, None] * np.repeat(log_gamma, RET_DV)[None, :])
    ks = np.exp((bt[-1] - bt)[:, None] * np.repeat(log_gamma, RET_DK)[None, :])
    cd = np.exp(bt[-1] * np.repeat(log_gamma, RET_DV))[None, :]
    return (jnp.asarray(dm, F32), jnp.asarray(rs, F32), jnp.asarray(ks, F32), jnp.asarray(cd, F32))


def _ret(P3, cos_t, sin_t, nw3, s0, l, C, rows):
    B, T, _ = P3.shape
    nc = T // rows
    col = _mixer_specs(rows)
    dm, rs, ks, cd = _ret_constants(C, rows)
    kern = functools.partial(_ret_kernel, C=C, rows=rows, nc=nc)
    scratch = [pltpu.VMEM((QK_W, V_W), F32)]
    if rows < C:
        scratch += [pltpu.VMEM((C, w), F32) for w in (QK_W, QK_W, V_W, V_W)]
    const = lambda shape: BS(shape, lambda b, c: (0,) * len(shape))
    return pl.pallas_call(
        kern, grid=(B, nc),
        in_specs=[col(QK_W, COL_RQ), col(QK_W, COL_RK), col(V_W, COL_RV), col(V_W, COL_RG),
                  BS((C, QK_W), lambda b, c: (c, 0)), BS((C, QK_W), lambda b, c: (c, 0)),
                  const((RET_HEADS, C, C)), const((C, V_W)), const((C, QK_W)), const((1, V_W)),
                  BS((None, 1, V_W), lambda b, c: (l, 0, 0)),
                  BS((1, RET_HEADS, RET_DK, RET_DV), lambda b, c: (b, 0, 0, 0))],
        out_specs=[BS((1, rows, V_W), lambda b, c: (b, c, 0)),
                   BS((1, RET_HEADS, RET_DK, RET_DV), lambda b, c: (b, 0, 0, 0))],
        out_shape=[SDS((B, T, V_W), BF16), SDS((B, RET_HEADS, RET_DK, RET_DV), F32)],
        scratch_shapes=scratch,
        compiler_params=_cparams("arbitrary", "arbitrary"))(
            P3, P3, P3, P3, cos_t, sin_t, dm, rs, ks, cd, nw3, s0)


def _rope_tables(pos0, n):
    half = RET_DK // 2
    inv = ROPE_BASE ** (-jnp.arange(half, dtype=F32) / half)
    pos = (pos0 + jnp.arange(n, dtype=jnp.int32)).astype(F32)
    ang = pos[:, None] * inv[None, :]
    cos, sin = jnp.cos(ang), jnp.sin(ang)
    cos_t = jnp.tile(jnp.concatenate([cos, cos], -1), (1, RET_HEADS))
    sin_t = jnp.tile(jnp.concatenate([-sin, sin], -1), (1, RET_HEADS))
    return cos_t, sin_t


def _head_rms64(x, g_ref, w):
    xh, xm = _split2(x * x)
    ms = jnp.dot(xh, g_ref[...], preferred_element_type=F32) + jnp.dot(xm, g_ref[...], preferred_element_type=F32)
    return x * lax.rsqrt(ms + EPS) * w


def _qknorm_kernel(q_ref, k_ref, g_ref, qw_ref, kw_ref, qn_ref, kn_ref):
    qn_ref[...] = _head_rms64(q_ref[...], g_ref, qw_ref[...])
    kn_ref[...] = _head_rms64(k_ref[...], g_ref, kw_ref[...])


def _prompt_prep_kernel(q_ref, k_ref, v_ref, g_ref, pl_ref, eye_ref, qw_ref, kw_ref,
                        qn_ref, kn_ref, km_ref, kp_ref, vt_ref):
    i = pl.program_id(0)
    qn_ref[...] = _head_rms64(q_ref[...], g_ref, qw_ref[...])
    kn = _head_rms64(k_ref[...], g_ref, kw_ref[...])
    kn_ref[...] = kn
    km_ref[0] = jnp.mean(kn, axis=0, keepdims=True)
    lane = lax.broadcasted_iota(jnp.int32, (MOBA_BLOCK, 2 * MOBA_W), 1) % 128
    placed = jnp.dot(kn.astype(BF16), pl_ref[...], preferred_element_type=F32)
    kp_ref[...] = (placed + jnp.where(lane == MOBA_HD + i, BIG, 0.0)).astype(BF16)
    vt_ref[...] = _dg(eye_ref[...], v_ref[...].astype(BF16), NT).astype(BF16)


def _group_mean_matrix():
    r = np.arange(MOBA_W)
    return jnp.asarray((r[:, None] // MOBA_HD == r[None, :] // MOBA_HD) / MOBA_HD, BF16)


def _placement_matrix():
    r = np.arange(MOBA_W)
    c = np.arange(2 * MOBA_W)
    tgt = (r // MOBA_HD) * 128 + r % MOBA_HD
    return jnp.asarray(tgt[:, None] == c[None, :], BF16)


def _qknorm(P, qw3, kw3, l, tm):
    M = P.shape[0]
    return pl.pallas_call(
        _qknorm_kernel, grid=(M // tm,),
        in_specs=[BS((tm, MOBA_W), lambda i: (i, COL_MQ)), BS((tm, MOBA_W), lambda i: (i, COL_MK)),
                  BS((MOBA_W, MOBA_W), lambda i: (0, 0)),
                  BS((None, 1, MOBA_W), lambda i: (l, 0, 0)), BS((None, 1, MOBA_W), lambda i: (l, 0, 0))],
        out_specs=[BS((tm, MOBA_W), lambda i: (i, 0)), BS((tm, MOBA_W), lambda i: (i, 0))],
        out_shape=[SDS((M, MOBA_W), F32), SDS((M, MOBA_W), F32)],
        compiler_params=_cparams("arbitrary"))(P, P, _group_mean_matrix(), qw3, kw3)


def _prompt_prep(P, qw3, kw3, l):
    T = P.shape[0]
    nb = T // MOBA_BLOCK
    tm = MOBA_BLOCK
    row = lambda w, idx: BS((tm, w), lambda i: (i, idx))
    return pl.pallas_call(
        _prompt_prep_kernel, grid=(nb,),
        in_specs=[row(MOBA_W, COL_MQ), row(MOBA_W, COL_MK), row(MOBA_W, COL_MV),
                  BS((MOBA_W, MOBA_W), lambda i: (0, 0)),
                  BS((MOBA_W, 2 * MOBA_W), lambda i: (0, 0)),
                  BS((MOBA_W, MOBA_W), lambda i: (0, 0)),
                  BS((None, 1, MOBA_W), lambda i: (l, 0, 0)), BS((None, 1, MOBA_W), lambda i: (l, 0, 0))],
        out_specs=[row(MOBA_W, 0), row(MOBA_W, 0), BS((1, 1, MOBA_W), lambda i: (i, 0, 0)),
                   row(2 * MOBA_W, 0), BS((MOBA_W, tm), lambda i: (0, i))],
        out_shape=[SDS((T, MOBA_W), F32), SDS((T, MOBA_W), F32), SDS((nb, 1, MOBA_W), F32),
                   SDS((T, 2 * MOBA_W), BF16), SDS((MOBA_W, T), BF16)],
        compiler_params=_cparams("arbitrary"))(
            P, P, P, _group_mean_matrix(), _placement_matrix(), jnp.eye(MOBA_W, dtype=BF16), qw3, kw3)


def _placed_block_means(kmean):
    nb = kmean.shape[0]
    km = kmean.reshape(nb, MOBA_HEADS, MOBA_HD).transpose(1, 2, 0)
    km = jnp.pad(km, ((0, 0), (0, 0), (MOBA_HD, MOBA_HD - nb)))
    full = km[:, :, None, :] * jnp.eye(MOBA_HEADS, dtype=F32)[:, None, :, None]
    return full.reshape(MOBA_W, 2 * MOBA_W)


def _select_top3(gate, valid, axis=-1):
    g1 = jnp.where(valid, gate, NEG)
    m1 = jnp.max(g1, axis=axis, keepdims=True)
    g2 = jnp.where(g1 >= m1, NEG, g1)
    m2 = jnp.max(g2, axis=axis, keepdims=True)
    g3 = jnp.where(g2 >= m2, NEG, g2)
    m3 = jnp.max(g3, axis=axis, keepdims=True)
    return valid & (g1 >= m3)


def _gate_kernel(qn_ref, kmt_ref, plt_ref, qpt_ref):
    i = pl.program_id(0)
    qn = qn_ref[...]
    gate = _dg(kmt_ref[...].astype(BF16), qn.astype(BF16), NT)
    qpart = _dg(plt_ref[...], (qn * (MOBA_HD ** -0.5 * LOG2E)).astype(BF16), NT)
    blk = lax.broadcasted_iota(jnp.int32, (128, MOBA_BLOCK), 0) - MOBA_HD
    valid = (blk >= 0) & (blk < i)
    for h in range(MOBA_HEADS):
        sl = slice(128 * h, 128 * (h + 1))
        sel = _select_top3(gate[sl], valid, axis=0) | (blk == i)
        maskpart = jnp.where((blk >= 0) & jnp.logical_not(sel), -1.0, 0.0)
        qpt_ref[sl, :] = (qpart[sl] + maskpart).astype(BF16)


def _gate(qn, km_placed_t):
    T = qn.shape[0]
    nb = T // MOBA_BLOCK
    return pl.pallas_call(
        _gate_kernel, name="moba_gate", grid=(nb,),
        in_specs=[BS((MOBA_BLOCK, MOBA_W), lambda i: (i, 0)),
                  BS((2 * MOBA_W, MOBA_W), lambda i: (0, 0)),
                  BS((2 * MOBA_W, MOBA_W), lambda i: (0, 0))],
        out_specs=BS((2 * MOBA_W, MOBA_BLOCK), lambda i: (0, i)),
        out_shape=SDS((2 * MOBA_W, T), BF16),
        compiler_params=_cparams("arbitrary"))(qn, km_placed_t, _placement_matrix().T)


def _bucket_np(dist):
    dist = np.maximum(dist, 0)
    max_exact = N_BUCKETS // 2
    large = max_exact + (np.log(np.maximum(dist, max_exact).astype(np.float64) / max_exact)
                         / math.log(MAX_DISTANCE / max_exact) * (N_BUCKETS - max_exact)).astype(np.int64)
    large = np.minimum(large, N_BUCKETS - 1)
    return np.where(dist < max_exact, dist, large).astype(np.int32)


def _bias_kernel(rb_ref, idx_ref, ok_ref, o_ref, *, scale):
    h = pl.program_id(0)
    idx = idx_ref[...]
    far = rb_ref[N_BUCKETS - 1, h]
    acc = jnp.zeros(idx.shape, F32)
    for b in range(N_BUCKETS - 1):
        acc = jnp.where(idx == b, (rb_ref[b, h] - far) * scale, acc)
    o_ref[0] = jnp.where(ok_ref[...] > 0, acc, -BIG)


def _bias_tables(rel_bias, dist, scale=1.0):
    idx = jnp.asarray(_bucket_np(dist))
    ok = jnp.asarray((dist >= 0).astype(np.int32))
    shp = dist.shape
    zeros = (0,) * len(shp)
    return pl.pallas_call(
        functools.partial(_bias_kernel, scale=scale), name="moba_bias_tables", grid=(MOBA_HEADS,),
        in_specs=[BS(memory_space=pltpu.SMEM), BS(shp, lambda h: zeros), BS(shp, lambda h: zeros)],
        out_specs=BS((1,) + shp, lambda h: (h,) + zeros),
        out_shape=SDS((MOBA_HEADS,) + shp, F32),
        compiler_params=_cparams("arbitrary"))(rel_bias, idx, ok)


def _attn_kernel(qpt_ref, kp_ref, vt_ref, bt_ref, eye_ref, o_ref, m_ref, l_ref, acc_ref, s_ref):
    i = pl.program_id(1)
    tq = MOBA_BLOCK
    m_ref[...] = jnp.full(m_ref.shape, NEG, F32)
    l_ref[...] = jnp.zeros_like(l_ref)
    acc_ref[...] = jnp.zeros_like(acc_ref)

    def logits(slot, hh, n, t):
        qt = qpt_ref[128 * hh:128 * (hh + 1), :]
        start = pl.multiple_of(n * MOBA_BLOCK, MOBA_BLOCK)
        kblk = kp_ref[pl.ds(start, MOBA_BLOCK), 128 * hh:128 * (hh + 1)]
        s = jnp.dot(kblk, qt, preferred_element_type=F32)
        if t is not None:
            s = s + bt_ref[hh, t]
        s_ref[slot] = s

    def softmax_pv(slot, hh, n):
        ps, alphas = [], []
        for c in range(tq // ATTN_COLS):
            cols = slice(ATTN_COLS * c, ATTN_COLS * (c + 1))
            sc = s_ref[slot, :, cols]
            m_old = m_ref[hh, :, cols]
            m_new = jnp.maximum(m_old, jnp.max(sc, axis=0, keepdims=True))
            alpha = jnp.exp2(m_old - m_new)
            p = jnp.exp2(sc - m_new)
            l_ref[hh, :, cols] = alpha * l_ref[hh, :, cols] + jnp.sum(p, axis=0, keepdims=True)
            m_ref[hh, :, cols] = m_new
            ps.append(p.astype(BF16))
            alphas.append(alpha)
        start = pl.multiple_of(n * MOBA_BLOCK, MOBA_BLOCK)
        vt = vt_ref[MOBA_HD * hh:MOBA_HD * (hh + 1), pl.ds(start, MOBA_BLOCK)]
        pv = jnp.dot(vt, jnp.concatenate(ps, axis=1), preferred_element_type=F32)
        acc_ref[hh] = jnp.concatenate(alphas, axis=1) * acc_ref[hh] + pv

    def group(tiles):
        for slot, (hh, n, t) in enumerate(tiles):
            logits(slot, hh, n, t)
        for slot, (hh, n, t) in enumerate(tiles):
            softmax_pv(slot, hh, n)

    n_far = jnp.maximum(i - 1, 0)

    def far_tiles(first, count):
        return [(hh, first + b, None) for b in range(count) for hh in range(2)]

    def far_body(j, carry):
        group(far_tiles(ATTN_UNROLL * j, ATTN_UNROLL))
        return carry

    lax.fori_loop(0, n_far // ATTN_UNROLL, far_body, 0)
    done = (n_far // ATTN_UNROLL) * ATTN_UNROLL
    width = ATTN_UNROLL // 2
    while width >= 1:
        @pl.when((n_far - done) % (2 * width) >= width)
        def _(done=done, width=width):
            group(far_tiles(done, width))
        done = done + jnp.where((n_far - done) % (2 * width) >= width, width, 0)
        width //= 2

    @pl.when(i >= 1)
    def _():
        group([(0, i - 1, 1), (1, i - 1, 1), (0, i, 0), (1, i, 0)])

    @pl.when(i == 0)
    def _():
        group([(0, i, 0), (1, i, 0)])

    out_t = jnp.concatenate([acc_ref[0] / l_ref[0], acc_ref[1] / l_ref[1]], axis=0).astype(BF16)
    o_ref[...] = _dg(out_t, eye_ref[...], TN).astype(o_ref.dtype)


def _prompt_attention(qpt, kp, vt, btab_t):
    T = kp.shape[0]
    nb = T // MOBA_BLOCK
    return pl.pallas_call(
        _attn_kernel, name="moba_prompt_attention", grid=(MOBA_HEADS // 2, nb),
        in_specs=[BS((256, MOBA_BLOCK), lambda hp, i: (hp, i)),
                  BS((T, 256), lambda hp, i: (0, hp)),
                  BS((128, T), lambda hp, i: (hp, 0)),
                  BS((2, 2, MOBA_BLOCK, MOBA_BLOCK), lambda hp, i: (hp, 0, 0, 0)),
                  BS((128, 128), lambda hp, i: (0, 0))],
        out_specs=BS((MOBA_BLOCK, 128), lambda hp, i: (i, hp)),
        out_shape=SDS((T, MOBA_W), BF16),
        scratch_shapes=[pltpu.VMEM((2, 1, MOBA_BLOCK), F32), pltpu.VMEM((2, 1, MOBA_BLOCK), F32),
                        pltpu.VMEM((2, MOBA_HD, MOBA_BLOCK), F32),
                        pltpu.VMEM((2 * ATTN_UNROLL, MOBA_BLOCK, MOBA_BLOCK), F32)],
        compiler_params=_cparams("arbitrary", "arbitrary"))(qpt, kp, vt, btab_t, jnp.eye(128, dtype=BF16))


def _block_mean_kernel(pt_ref, *refs):
    npg = PAGES_PER_STEP
    pages, km_ref = refs[:npg], refs[npg]
    for r in range(npg):
        s = jnp.sum(pages[r][...], axis=0)
        if r % 2 == 0:
            prev = s
        else:
            km_ref[0, r // 2] = (prev + s) * (1.0 / MOBA_BLOCK)


def _block_means(page_table, cache, l):
    nseq, n_pages = page_table.shape
    page = cache.shape[2]
    npg = PAGES_PER_STEP
    page_specs = [BS((None, None, page, MOBA_HEADS, MOBA_HD), functools.partial(
        lambda b, g, pt, r: (l, pt[b, g * npg + r], 0, 0, 0), r=r)) for r in range(npg)]
    grid_spec = pltpu.PrefetchScalarGridSpec(
        num_scalar_prefetch=1, grid=(nseq, n_pages // npg), in_specs=page_specs,
        out_specs=BS((1, npg // 2, MOBA_HEADS, MOBA_HD), lambda b, g, pt: (b, g, 0, 0)))
    return pl.pallas_call(
        _block_mean_kernel, name="decode_block_means", grid_spec=grid_spec,
        out_shape=SDS((nseq, n_pages // 2, MOBA_HEADS, MOBA_HD), F32),
        compiler_params=_cparams("arbitrary", "arbitrary"))(page_table, *([cache] * npg))


def _decode_select_kernel(q_ref, km_ref, o_ref):
    gate = _dg(q_ref[0].astype(BF16), km_ref[0].astype(BF16), NT)
    rows, nblk = gate.shape
    lane = lax.broadcasted_iota(jnp.int32, (rows, nblk), 1)
    out_lane = lax.broadcasted_iota(jnp.int32, (rows, 128), 1)
    out = jnp.zeros((rows, 128), jnp.int32)
    for t in range(MOBA_TOPK):
        m = jnp.max(gate, axis=-1, keepdims=True)
        idx = jnp.min(jnp.where(gate >= m, lane, nblk), axis=-1, keepdims=True)
        out = jnp.where(out_lane == t, idx, out)
        gate = jnp.where(lane == idx, NEG, gate)
    o_ref[0] = out


def _decode_select(qbd, kmean_flat):
    nseq, rows, _ = qbd.shape
    nblk = kmean_flat.shape[1]
    return pl.pallas_call(
        _decode_select_kernel, name="decode_select", grid=(nseq,),
        in_specs=[BS((1, rows, MOBA_W), lambda b: (b, 0, 0)), BS((1, nblk, MOBA_W), lambda b: (b, 0, 0))],
        out_specs=BS((1, rows, 128), lambda b: (b, 0, 0)),
        out_shape=SDS((nseq, rows, 128), jnp.int32),
        compiler_params=_cparams("arbitrary"))(qbd, kmean_flat)


def _bf16_round(x):
    return x.astype(BF16).astype(F32)


def _decode_attend_kernel(pt_ref, sel_ref, ck_ref, cv_ref, q_ref, kn_ref, vn_ref, bpast_ref, bown_ref, o_ref,
                          kbuf, vbuf, sem, *, l, nd, nblk, page):
    b = pl.program_id(0)
    rows = MOBA_HEADS * nd
    trips = rows // DECODE_ROWS
    pages_per_block = MOBA_BLOCK // page

    def copies(i, slot):
        out = []
        for u in range(DECODE_ROWS):
            r = DECODE_ROWS * i + u
            h = r // nd
            for t in range(MOBA_TOPK):
                n = sel_ref[b, MOBA_TOPK * r + t]
                for g in range(pages_per_block):
                    pg = pt_ref[b, pages_per_block * n + g]
                    dst = slice(page * g, page * (g + 1))
                    out.append(pltpu.make_async_copy(ck_ref.at[l, pg, :, h, :], kbuf.at[slot, u, t, dst, :],
                                                     sem.at[slot, 0, u, t, g]))
                    out.append(pltpu.make_async_copy(cv_ref.at[l, pg, :, h, :], vbuf.at[slot, u, t, dst, :],
                                                     sem.at[slot, 1, u, t, g]))
        return out

    for c in copies(0, 0):
        c.start()

    def body(i, carry):
        slot = i % 2

        @pl.when(i + 1 < trips)
        def _():
            for c in copies(i + 1, 1 - slot):
                c.start()

        for c in copies(i, slot):
            c.wait()

        steps = [attend_row(DECODE_ROWS * i + u, kbuf.at[slot, u], vbuf.at[slot, u]) for u in range(DECODE_ROWS)]
        for _ in range(3):
            for st in steps:
                next(st, None)
        return carry

    def attend_row(r, kbuf, vbuf):
        h = r // nd
        qv = q_ref[0, r]
        qmat = jnp.broadcast_to(qv, (8, MOBA_HD)).astype(BF16)
        logits = []
        for t in range(MOBA_TOPK):
            s = _dg(qmat, kbuf[t].astype(BF16), NT)
            is_last = sel_ref[b, MOBA_TOPK * r + t] == nblk - 1
            logits.append(s + jnp.where(is_last, bpast_ref[pl.ds(r, 1), :], 0.0))
        yield
        s_own = jnp.sum(_bf16_round(kn_ref[0, h]) * _bf16_round(qv), axis=-1, keepdims=True) + bown_ref[r]
        m = jnp.max(s_own, axis=0, keepdims=True)
        for s in logits:
            m = jnp.maximum(m, jnp.max(s[0:1], axis=-1, keepdims=True))
        p_own = jnp.exp(s_own - m)
        denom = jnp.sum(p_own, axis=0, keepdims=True)
        probs = [jnp.exp(s - m) for s in logits]
        for p in probs:
            denom = denom + jnp.sum(p[0:1], axis=-1, keepdims=True)
        inv = 1.0 / denom
        yield
        acc = jnp.zeros((8, MOBA_HD), F32)
        for t in range(MOBA_TOPK):
            acc = acc + jnp.dot((probs[t] * inv).astype(BF16), vbuf[t].astype(BF16),
                                preferred_element_type=F32)
        own = jnp.sum(_bf16_round(p_own * inv) * _bf16_round(vn_ref[0, h]), axis=0, keepdims=True)
        o_ref[0, r] = acc[0:1] + own

    lax.fori_loop(0, trips, body, 0)


def _decode_attend(page_table, sel, cache_k, cache_v, q_rows, kn_new, vn_new, b_past, b_own, l, nd):
    nseq, n_pages = page_table.shape
    page = cache_k.shape[2]
    rows = q_rows.shape[1]
    per_seq = lambda shape: BS((1,) + shape, lambda b, pt, sl: (b,) + (0,) * len(shape))
    const = lambda shape: BS(shape, lambda b, pt, sl: (0,) * len(shape))
    assert rows % DECODE_ROWS == 0
    slab = pltpu.VMEM((2, DECODE_ROWS, MOBA_TOPK, MOBA_BLOCK, MOBA_HD), F32)
    grid_spec = pltpu.PrefetchScalarGridSpec(
        num_scalar_prefetch=2, grid=(nseq,),
        in_specs=[BS(memory_space=pl.ANY), BS(memory_space=pl.ANY),
                  per_seq((rows, 1, MOBA_HD)), per_seq((MOBA_HEADS, 8, MOBA_HD)), per_seq((MOBA_HEADS, 8, MOBA_HD)),
                  const((rows, MOBA_BLOCK)), const((rows, 8, 1))],
        out_specs=per_seq((rows, 1, MOBA_HD)),
        scratch_shapes=[slab, slab, pltpu.SemaphoreType.DMA((2, 2, DECODE_ROWS, MOBA_TOPK, MOBA_BLOCK // page))])
    kern = functools.partial(_decode_attend_kernel, l=l, nd=nd, nblk=n_pages * page // MOBA_BLOCK, page=page)
    return pl.pallas_call(
        kern, name="decode_attend", grid_spec=grid_spec,
        out_shape=SDS((nseq, rows, 1, MOBA_HD), F32),
        compiler_params=_cparams("arbitrary"))(page_table, sel, cache_k, cache_v, q_rows, kn_new, vn_new, b_past, b_own)


def _merge_kernel(oa_ref, ob_ref, oc_ref, mg_ref, x_ref, g1_ref, wa_ref, wb_ref, wc_ref, wo_ref, o_ref):
    mg = mg_ref[...]
    ya = jnp.dot(oa_ref[...], wa_ref[...], preferred_element_type=F32)
    yb = jnp.dot(ob_ref[...], wb_ref[...], preferred_element_type=F32)
    yc = jnp.dot(oc_ref[...], wc_ref[...], preferred_element_type=F32)
    merged = (_sigmoid(mg[:, 0:D_MODEL]) * ya + _sigmoid(mg[:, D_MODEL:2 * D_MODEL]) * yb
              + _sigmoid(mg[:, 2 * D_MODEL:]) * yc)
    mix = jnp.dot(merged.astype(BF16), wo_ref[...], preferred_element_type=F32)
    o_ref[...] = x_ref[...] + g1_ref[...] * mix


def _merge(oa, ob, oc, P, x, g1, wa, wb, wc, wo, l, tm):
    M = x.shape[0]
    row = lambda w: BS((tm, w), lambda i: (i, 0))
    wspec = lambda k: BS((None, k, D_MODEL), lambda i: (l, 0, 0))
    return pl.pallas_call(
        _merge_kernel, grid=(M // tm,),
        in_specs=[row(V_W), row(V_W), row(MOBA_W), row(3 * D_MODEL), row(D_MODEL),
                  _mod_spec(g1.shape[0], tm, 1),
                  wspec(V_W), wspec(V_W), wspec(MOBA_W), wspec(D_MODEL)],
        out_specs=row(D_MODEL),
        out_shape=SDS((M, D_MODEL), F32),
        compiler_params=_cparams("arbitrary"))(oa, ob, oc, P, x, g1, wa, wb, wc, wo)


def _ffn_kernel(x_ref, nw_ref, sc_ref, sh_ref, g2_ref, w1_ref, w3_ref, w2_ref, o_ref, h_ref, acc_ref):
    j = pl.program_id(1)

    @pl.when(j == 0)
    def _():
        h_ref[...] = _modulated_norm(x_ref[...], nw_ref[...], sc_ref[...], sh_ref[...]).astype(BF16)
        acc_ref[...] = jnp.zeros_like(acc_ref)

    h = h_ref[...]
    a = jnp.dot(h, w1_ref[...], preferred_element_type=F32)
    b = jnp.dot(h, w3_ref[...], preferred_element_type=F32)
    acc_ref[...] += jnp.dot((_silu(a) * b).astype(BF16), w2_ref[...], preferred_element_type=F32)

    @pl.when(j == pl.num_programs(1) - 1)
    def _():
        o_ref[...] = x_ref[...] + g2_ref[...] * acc_ref[...]


def _ffn(x, nw3, sc, sh, g2, w1, w3, w2, l, li, tm):
    M = x.shape[0]
    tf = 256
    ms = _mod_spec(sc.shape[0], tm, 2)
    return pl.pallas_call(
        _ffn_kernel, grid=(M // tm, D_FF // tf),
        in_specs=[BS((tm, D_MODEL), lambda i, j: (i, 0)),
                  BS((None, 1, D_MODEL), lambda i, j: (l, 0, 0)), ms, ms, ms,
                  BS((None, D_MODEL, tf), lambda i, j: (li, 0, j)),
                  BS((None, D_MODEL, tf), lambda i, j: (li, 0, j)),
                  BS((None, tf, D_MODEL), lambda i, j: (li, j, 0))],
        out_specs=BS((tm, D_MODEL), lambda i, j: (i, 0)),
        out_shape=SDS((M, D_MODEL), F32),
        scratch_shapes=[pltpu.VMEM((tm, D_MODEL), BF16), pltpu.VMEM((tm, D_MODEL), F32)],
        compiler_params=_cparams("arbitrary", "arbitrary"))(x, nw3, sc, sh, g2, w1, w3, w2)


def _moe_kernel(x_ref, nw_ref, sc_ref, sh_ref, g2_ref, wr_ref, w1_ref, w3_ref, w2_ref, o_ref,
                h_ref, gate_ref, acc_ref):
    e = pl.program_id(1)
    tm = x_ref.shape[0]
    lane = lax.broadcasted_iota(jnp.int32, (tm, 128), 1)

    @pl.when(e == 0)
    def _():
        h = _modulated_norm(x_ref[...], nw_ref[...], sc_ref[...], sh_ref[...])
        h_ref[...] = h.astype(BF16)
        router = jnp.dot(h.astype(BF16), wr_ref[...].astype(BF16), preferred_element_type=F32)
        logits = jnp.where(lane < N_EXPERTS, router, NEG)
        m1 = jnp.max(logits, axis=-1, keepdims=True)
        i1 = jnp.min(jnp.where(logits >= m1, lane, 128), axis=-1, keepdims=True)
        rest = jnp.where(lane == i1, NEG, logits)
        m2 = jnp.max(rest, axis=-1, keepdims=True)
        i2 = jnp.min(jnp.where(rest >= m2, lane, 128), axis=-1, keepdims=True)
        e2 = jnp.exp(m2 - m1)
        w_first = 1.0 / (1.0 + e2)
        gate_ref[...] = jnp.where(lane == i1, w_first, 0.0) + jnp.where(lane == i2, e2 * w_first, 0.0)
        acc_ref[...] = jnp.zeros_like(acc_ref)

    h = h_ref[...]
    a = jnp.dot(h, w1_ref[...], preferred_element_type=F32)
    b = jnp.dot(h, w3_ref[...], preferred_element_type=F32)
    y = jnp.dot((_silu(a) * b).astype(BF16), w2_ref[...], preferred_element_type=F32)
    ge = jnp.sum(jnp.where(lane == e, gate_ref[...], 0.0), axis=-1, keepdims=True)
    acc_ref[...] += ge * y

    @pl.when(e == pl.num_programs(1) - 1)
    def _():
        o_ref[...] = x_ref[...] + g2_ref[...] * acc_ref[...]


def _moe(x, nw3, sc, sh, g2, wr, w1, w3, w2, l, li, tm):
    M = x.shape[0]
    ms = _mod_spec(sc.shape[0], tm, 2)
    return pl.pallas_call(
        _moe_kernel, grid=(M // tm, N_EXPERTS),
        in_specs=[BS((tm, D_MODEL), lambda i, e: (i, 0)),
                  BS((None, 1, D_MODEL), lambda i, e: (l, 0, 0)), ms, ms, ms,
                  BS((None, D_MODEL, 128), lambda i, e: (li, 0, 0)),
                  BS((None, None, D_MODEL, D_FF_EXPERT), lambda i, e: (li, e, 0, 0)),
                  BS((None, None, D_MODEL, D_FF_EXPERT), lambda i, e: (li, e, 0, 0)),
                  BS((None, None, D_FF_EXPERT, D_MODEL), lambda i, e: (li, e, 0, 0))],
        out_specs=BS((tm, D_MODEL), lambda i, e: (i, 0)),
        out_shape=SDS((M, D_MODEL), F32),
        scratch_shapes=[pltpu.VMEM((tm, D_MODEL), BF16), pltpu.VMEM((tm, 128), F32),
                        pltpu.VMEM((tm, D_MODEL), F32)],
        compiler_params=_cparams("arbitrary", "arbitrary"))(x, nw3, sc, sh, g2, wr, w1, w3, w2)


def kernel(x_prompt, x_sample, cache_k, cache_v, state_gla, state_ret, page_table, c_prompt, c_sample,
           ada_w, ada_b, norm1, norm2, w_in, gla_gk_w2, gla_gk_b, gla_norm, ret_norm, moba_qnorm,
           moba_knorm, rel_bias, w_br_gla, w_br_ret, w_br_moba, w_out, ffn_w1, ffn_w3, ffn_w2,
           moe_router, moe_w1, moe_w3, moe_w2):
    depth = w_in.shape[0]
    bp, T, _ = x_prompt.shape
    nseq, nd, _ = x_sample.shape
    n_pages, page = page_table.shape[1], cache_k.shape[2]
    past_len = n_pages * page
    nb = T // MOBA_BLOCK
    assert bp == 1 and T % MOBA_BLOCK == 0 and nb <= MOBA_HD and page == 128
    assert past_len % MOBA_BLOCK == 0 and n_pages % PAGES_PER_STEP == 0 and nd <= 8
    ms_rows = nseq * nd
    nq = nd * MOBA_HEADS

    sizes = np.cumsum([QK_W, QK_W, V_W, GLA_LOWRANK, V_W, QK_W, QK_W, V_W, V_W, MOBA_W, MOBA_W, MOBA_W])
    glr0, glr1, mg0 = int(sizes[2]), int(sizes[3]), int(sizes[-1])
    w_main = jnp.concatenate([w_in[:, :, mg0:], w_in[:, :, :glr0], w_in[:, :, glr1:mg0]], axis=2).astype(BF16)
    w_glr = jnp.pad(w_in[:, :, glr0:glr1], ((0, 0), (0, 0), (0, 128 - GLA_LOWRANK))).astype(BF16)
    w2p = jnp.pad(gla_gk_w2, ((0, 0), (0, 128 - GLA_LOWRANK), (0, 0)))
    gkb3 = gla_gk_b[:, None, :]
    gla_nw3 = jnp.tile(gla_norm, (1, GLA_HEADS))[:, None, :]
    ret_nw3 = jnp.tile(ret_norm, (1, RET_HEADS))[:, None, :]
    qw3 = jnp.tile(moba_qnorm, (1, MOBA_HEADS))[:, None, :]
    kw3 = jnp.tile(moba_knorm, (1, MOBA_HEADS))[:, None, :]
    n1_3, n2_3 = norm1[:, None, :], norm2[:, None, :]
    ada_b3 = ada_b[:, None, :]
    wa, wb, wc, wo = (w.astype(BF16) for w in (w_br_gla, w_br_ret, w_br_moba, w_out))
    f1, f3, f2 = ffn_w1.astype(BF16), ffn_w3.astype(BF16), ffn_w2.astype(BF16)
    e1, e3, e2 = moe_w1.astype(BF16), moe_w3.astype(BF16), moe_w2.astype(BF16)
    wr = jnp.pad(moe_router, ((0, 0), (0, 0), (0, 128 - N_EXPERTS)))

    cos_p, sin_p = _rope_tables(0, T)
    cos_s, sin_s = _rope_tables(past_len, CHUNK)
    tq = np.arange(MOBA_BLOCK)
    d_own = tq[:, None] - tq[None, :]
    btab = _bias_tables(rel_bias, np.stack([d_own.T, d_own.T + MOBA_BLOCK]), LOG2E)
    qi = np.arange(8)
    d_past = (MOBA_BLOCK + qi[:, None] - tq[None, :])
    d_new = np.where((qi[None, :] < nd) & (qi[:, None] < nd), qi[:, None] - qi[None, :], -1)
    bt_past = _bias_tables(rel_bias, d_past)[:, :nd].reshape(nq, MOBA_BLOCK)
    bt_own = _bias_tables(rel_bias, np.pad(d_new, ((0, 0), (0, 120)), constant_values=-1))
    bt_own = bt_own[:, :nd, :8].reshape(nq, 8, 1)

    c_all = jnp.concatenate([c_prompt, c_sample, jnp.zeros((-(bp + nseq) % 8, D_MODEL), F32)], axis=0)
    zero_gla = jnp.zeros((bp, GLA_HEADS, GLA_DK, GLA_DV), F32)
    zero_ret = jnp.zeros((bp, RET_HEADS, RET_DK, RET_DV), F32)

    xp = x_prompt.reshape(T, D_MODEL)
    xs = x_sample.reshape(ms_rows, D_MODEL)
    outs = {k: [] for k in ("kp", "vp", "gp", "rp", "ks", "vs", "gs", "rs")}
    for l in range(depth):
        mod = _ada_mod(c_all, ada_w, ada_b3, l)
        mp = [mod[0:1, j * D_MODEL:(j + 1) * D_MODEL] for j in range(6)]
        msm = [jnp.repeat(mod[bp:bp + nseq, j * D_MODEL:(j + 1) * D_MODEL], nd, axis=0) for j in range(6)]
        li = l // 2

        def channel(x, m, tm):
            if l % 2 == 0:
                return _ffn(x, n2_3, m[4], m[3], m[5], f1, f3, f2, l, li, tm)
            return _moe(x, n2_3, m[4], m[3], m[5], wr, e1, e3, e2, l, li, tm)

        P, G = _inproj(xp, n1_3, mp[1], mp[0], w_main, w_glr, l, 1024 if T % 1024 == 0 else MOBA_BLOCK)
        P3, G3 = P.reshape(1, T, N_MAIN), G.reshape(1, T, 128)
        oa, sg = _gla(P3, G3, w2p, gkb3, gla_nw3, zero_gla, l, CHUNK, CHUNK)
        ob, sr = _ret(P3, cos_p, sin_p, ret_nw3, zero_ret, l, CHUNK, CHUNK)
        qn, kn, kmean, kp, vt = _prompt_prep(P, qw3, kw3, l)
        qpt = _gate(qn, _placed_block_means(kmean.reshape(nb, MOBA_W)).T)
        oc = _prompt_attention(qpt, kp, vt, btab)
        xm = _merge(oa.reshape(T, V_W), ob.reshape(T, V_W), oc, P, xp, mp[2], wa, wb, wc, wo, l, 512 if T % 512 == 0 else MOBA_BLOCK)
        xp = channel(xm, mp, 512 if T % 512 == 0 else MOBA_BLOCK)
        outs["kp"].append(kn.reshape(bp, T, MOBA_HEADS, MOBA_HD))
        outs["vp"].append(P[:, COL_MV * MOBA_W:(COL_MV + 1) * MOBA_W].reshape(bp, T, MOBA_HEADS, MOBA_HD))
        outs["gp"].append(sg)
        outs["rp"].append(sr)

        Ps, Gs = _inproj(xs, n1_3, msm[1], msm[0], w_main, w_glr, l, ms_rows)
        Ps3, Gs3 = Ps.reshape(nseq, nd, N_MAIN), Gs.reshape(nseq, nd, 128)
        oas, sgs = _gla(Ps3, Gs3, w2p, gkb3, gla_nw3, state_gla[l], l, CHUNK, nd)
        obs, srs = _ret(Ps3, cos_s, sin_s, ret_nw3, state_ret[l], l, CHUNK, nd)
        qns, kns = _qknorm(Ps, qw3, kw3, l, ms_rows)
        vns = Ps[:, COL_MV * MOBA_W:(COL_MV + 1) * MOBA_W]
        qs4 = (qns * (MOBA_HD ** -0.5)).reshape(nseq, nd, MOBA_HEADS, MOBA_HD).transpose(0, 2, 1, 3)
        qbd = (qs4.reshape(nseq, MOBA_HEADS, nd, 1, MOBA_HD)
               * jnp.eye(MOBA_HEADS, dtype=F32)[None, :, None, :, None]).reshape(nseq, nq, MOBA_W)
        kmean_s = _block_means(page_table, cache_k, l)
        sel = _decode_select(qbd, kmean_s.reshape(nseq, -1, MOBA_W))[:, :, :MOBA_TOPK].reshape(nseq, nq * MOBA_TOPK)
        head_major = lambda a: jnp.pad(a.reshape(nseq, nd, MOBA_HEADS, MOBA_HD).transpose(0, 2, 1, 3),
                                       ((0, 0), (0, 0), (0, 8 - nd), (0, 0)))
        ocs4 = _decode_attend(page_table, sel, cache_k, cache_v, qs4.reshape(nseq, nq, 1, MOBA_HD),
                              head_major(kns), head_major(vns), bt_past, bt_own, l, nd)
        ocs = ocs4.reshape(nseq, MOBA_HEADS, nd, MOBA_HD).transpose(0, 2, 1, 3)
        xms = _merge(oas.reshape(ms_rows, V_W), obs.reshape(ms_rows, V_W), ocs.reshape(ms_rows, MOBA_W).astype(BF16),
                     Ps, xs, msm[2], wa, wb, wc, wo, l, ms_rows)
        xs = channel(xms, msm, ms_rows)
        outs["ks"].append(kns.reshape(nseq, nd, MOBA_HEADS, MOBA_HD))
        outs["vs"].append(vns.reshape(nseq, nd, MOBA_HEADS, MOBA_HD))
        outs["gs"].append(sgs)
        outs["rs"].append(srs)

    st = lambda k: jnp.stack(outs[k])
    return (xp.reshape(bp, T, D_MODEL), xs.reshape(nseq, nd, D_MODEL), st("kp"), st("vp"), st("gp"), st("rp"),
            st("ks"), st("vs"), st("gs"), st("rs"))
```

```python
import functools
import math

import numpy as np
import jax
import jax.numpy as jnp
from jax import lax
from jax.experimental import pallas as pl
from jax.experimental.pallas import tpu as pltpu

F32 = jnp.float32
BF16 = jnp.bfloat16
BS = pl.BlockSpec
SDS = jax.ShapeDtypeStruct

D_MODEL = 1024
GLA_HEADS, GLA_DK, GLA_DV, GLA_LOWRANK, GLA_GATE_TEMP = 4, 64, 128, 16, 16.0
RET_HEADS, RET_DK, RET_DV = 4, 64, 128
ROPE_BASE = 10000.0
MOBA_HEADS, MOBA_HD, MOBA_BLOCK, MOBA_TOPK = 8, 64, 256, 3
N_BUCKETS, MAX_DISTANCE = 32, 128
D_FF, N_EXPERTS, TOP_K, D_FF_EXPERT = 2816, 8, 2, 1408
EPS = 1e-6
BIG = 1e30
NEG = -3.0e38
LOG2E = math.log2(math.e)

QK_W = GLA_HEADS * GLA_DK
V_W = GLA_HEADS * GLA_DV
MOBA_W = MOBA_HEADS * MOBA_HD
N_MAIN = 7680
COL_GQ, COL_GK, COL_RQ, COL_RK = 12, 13, 18, 19
COL_GV, COL_GR, COL_RV, COL_RG, COL_MQ, COL_MK, COL_MV = 7, 8, 10, 11, 12, 13, 14

VMEM_LIMIT_BYTES = 56 * 1024 * 1024
CHUNK = 128
ATTN_COLS = 128
ATTN_UNROLL = 8
PAGES_PER_STEP = 16
DECODE_ROWS = 4


def _cparams(*sem):
    return pltpu.CompilerParams(dimension_semantics=sem, vmem_limit_bytes=VMEM_LIMIT_BYTES)


def _dg(a, b, dims):
    return lax.dot_general(a, b, (dims, ((), ())), preferred_element_type=F32)


NN = ((1,), (0,))
NT = ((1,), (1,))
TN = ((0,), (0,))


def _split2(x):
    h = x.astype(BF16)
    m = (x - h.astype(F32)).astype(BF16)
    return h, m


def _split3(x):
    h = x.astype(BF16)
    r = x - h.astype(F32)
    m = r.astype(BF16)
    l = (r - m.astype(F32)).astype(BF16)
    return h, m, l


def _dot_x3(a, b, dims=NN):
    ah, am = _split2(a)
    bh, bm = _split2(b)
    return _dg(ah, bh, dims) + _dg(ah, bm, dims) + _dg(am, bh, dims)


def _sigmoid(x):
    return 1.0 / (1.0 + jnp.exp(-x))


def _silu(x):
    return x * _sigmoid(x)


def _mod_kernel(c_ref, w_ref, b_ref, o_ref):
    o_ref[...] = jnp.dot(c_ref[...].astype(BF16), w_ref[...].astype(BF16),
                         preferred_element_type=F32) + b_ref[...]


def _ada_mod(c_all, ada_w, ada_b3, l):
    R = c_all.shape[0]
    TN_ = 1536
    n = 6 * D_MODEL
    return pl.pallas_call(
        _mod_kernel, grid=(n // TN_,),
        in_specs=[BS((R, D_MODEL), lambda j: (0, 0)),
                  BS((None, D_MODEL, TN_), lambda j: (l, 0, j)),
                  BS((None, 1, TN_), lambda j: (l, 0, j))],
        out_specs=BS((R, TN_), lambda j: (0, j)),
        out_shape=SDS((R, n), F32),
        compiler_params=_cparams("arbitrary"))(c_all, ada_w, ada_b3)


def _modulated_norm(x, nw, sc, sh):
    ms = jnp.mean(x * x, axis=-1, keepdims=True)
    return (x * lax.rsqrt(ms + EPS) * nw) * (1.0 + sc) + sh


def _inproj_kernel(x_ref, nw_ref, sc_ref, sh_ref, w_ref, wg_ref, o_ref, og_ref, h_ref):
    @pl.when(pl.program_id(1) == 0)
    def _():
        h = _modulated_norm(x_ref[...], nw_ref[...], sc_ref[...], sh_ref[...]).astype(BF16)
        h_ref[...] = h
        og_ref[...] = jnp.dot(h, wg_ref[...], preferred_element_type=F32)

    o_ref[...] = jnp.dot(h_ref[...], w_ref[...], preferred_element_type=F32)


def _mod_spec(mod_rows, tm, ngrid):
    if mod_rows == 1:
        return BS((1, D_MODEL), (lambda i, j: (0, 0)) if ngrid == 2 else (lambda i: (0, 0)))
    return BS((tm, D_MODEL), (lambda i, j: (i, 0)) if ngrid == 2 else (lambda i: (i, 0)))


def _inproj(x, nw3, sc, sh, w_main, w_glr, l, tm):
    M = x.shape[0]
    TN_ = 512
    ms = _mod_spec(sc.shape[0], tm, 2)
    return pl.pallas_call(
        _inproj_kernel, grid=(M // tm, N_MAIN // TN_),
        in_specs=[BS((tm, D_MODEL), lambda i, j: (i, 0)),
                  BS((None, 1, D_MODEL), lambda i, j: (l, 0, 0)),
                  ms, ms,
                  BS((None, D_MODEL, TN_), lambda i, j: (l, 0, j)),
                  BS((None, D_MODEL, 128), lambda i, j: (l, 0, 0))],
        out_specs=[BS((tm, TN_), lambda i, j: (i, j)),
                   BS((tm, 128), lambda i, j: (i, 0))],
        out_shape=[SDS((M, N_MAIN), F32), SDS((M, 128), F32)],
        scratch_shapes=[pltpu.VMEM((tm, D_MODEL), BF16)],
        compiler_params=_cparams("arbitrary", "arbitrary"))(x, nw3, sc, sh, w_main, w_glr)


def _load_rows(ref, pad_ref, rows, C):
    if rows == C:
        return ref[0]
    pad_ref[...] = jnp.zeros_like(pad_ref)
    pad_ref[0:rows, :] = ref[0]
    return pad_ref[...]


def _state_init(S_ref, s0_ref, nh, dk, dv):
    S_ref[...] = jnp.zeros_like(S_ref)
    for h in range(nh):
        S_ref[dk * h:dk * (h + 1), dv * h:dv * (h + 1)] = s0_ref[0, h]


def _state_out(so_ref, S_ref, nh, dk, dv):
    for h in range(nh):
        so_ref[0, h] = S_ref[dk * h:dk * (h + 1), dv * h:dv * (h + 1)]


def _block_diag_mask():
    r = lax.broadcasted_iota(jnp.int32, (QK_W, V_W), 0) // GLA_DK
    c = lax.broadcasted_iota(jnp.int32, (QK_W, V_W), 1) // GLA_DV
    return r == c


def _gla_kernel(q_ref, k_ref, v_ref, r_ref, g_ref, w2_ref, b_ref, nw_ref, s0_ref,
                o_ref, so_ref, S_ref, *pads, C, rows, nc):
    c = pl.program_id(1)

    @pl.when(c == 0)
    def _():
        _state_init(S_ref, s0_ref, GLA_HEADS, GLA_DK, GLA_DV)

    pads = list(pads) + [None] * 5
    q = _load_rows(q_ref, pads[0], rows, C)
    k = _load_rows(k_ref, pads[1], rows, C)
    v = _load_rows(v_ref, pads[2], rows, C)
    r = _load_rows(r_ref, pads[3], rows, C)
    g = _load_rows(g_ref, pads[4], rows, C)

    row = lax.broadcasted_iota(jnp.int32, (C, QK_W), 0)
    x = jnp.dot(g.astype(BF16), w2_ref[...].astype(BF16), preferred_element_type=F32) + b_ref[...]
    log_a = (jnp.minimum(x, 0.0) - jnp.log(1.0 + jnp.exp(-jnp.abs(x)))) * (1.0 / GLA_GATE_TEMP)
    if rows < C:
        log_a = jnp.where(row < rows, log_a, 0.0)

    ri = lax.broadcasted_iota(jnp.int32, (C, C), 0)
    ci = lax.broadcasted_iota(jnp.int32, (C, C), 1)
    causal = ri >= ci
    ltri = jnp.where(causal, 1.0, 0.0).astype(BF16)
    ones_c = jnp.ones((C, 128), BF16)
    pieces = _split3(log_a)
    b = sum(_dg(ltri, p, NN) for p in pieces)
    b_last_col = sum(_dg(p, ones_c, TN) for p in pieces)
    mid = max(min(rows, C) // 2, 1)
    b_ref_row = b[mid - 1:mid, :]
    b_last = b[C - 1:C, :]

    qs = q * (GLA_DK ** -0.5)
    qt_h, qt_m = _split2(qs * jnp.exp(b - b_ref_row))
    kt_h, kt_m = _split2(k * jnp.exp(b_ref_row - b))
    q_state = (qs * jnp.exp(b)).astype(BF16)
    k_state = (k * jnp.exp(b_last - b)).astype(BF16)
    vb = v.astype(BF16)

    S = S_ref[...]
    o_state = _dg(q_state, S.astype(BF16), NN)
    lane_head = lax.broadcasted_iota(jnp.int32, (C, QK_W), 1) // GLA_DK
    nw = nw_ref[...]
    for h in range(GLA_HEADS):
        mine = lane_head == h
        qh_h = jnp.where(mine, qt_h, jnp.zeros_like(qt_h))
        qh_m = jnp.where(mine, qt_m, jnp.zeros_like(qt_m))
        a = _dg(qh_h, kt_h, NT) + _dg(qh_h, kt_m, NT) + _dg(qh_m, kt_h, NT)
        a = jnp.where(causal, a, 0.0).astype(BF16)
        sl = slice(GLA_DV * h, GLA_DV * (h + 1))
        oh = _dg(a, vb[:, sl], NN) + o_state[:, sl]
        ms = jnp.mean(oh * oh, axis=-1, keepdims=True)
        y = oh * lax.rsqrt(ms + EPS) * nw[:, sl] * _silu(r[:, sl])
        o_ref[0, :, sl] = y[0:rows].astype(o_ref.dtype)

    u = _dg(k_state, vb, TN)
    e_col = jnp.exp(b_last_col)
    e_full = jnp.concatenate([e_col] * (V_W // 128), axis=1)
    S_ref[...] = e_full * S + jnp.where(_block_diag_mask(), u, 0.0)

    @pl.when(c == nc - 1)
    def _():
        _state_out(so_ref, S_ref, GLA_HEADS, GLA_DK, GLA_DV)


def _mixer_specs(rows):
    def col(width, idx):
        return BS((1, rows, width), lambda b, c: (b, c, idx))
    return col


def _gla(P3, G3, w2p, bias3, nw3, s0, l, C, rows):
    B, T, _ = P3.shape
    nc = T // rows
    col = _mixer_specs(rows)
    kern = functools.partial(_gla_kernel, C=C, rows=rows, nc=nc)
    scratch = [pltpu.VMEM((QK_W, V_W), F32)]
    if rows < C:
        scratch += [pltpu.VMEM((C, w), F32) for w in (QK_W, QK_W, V_W, V_W, 128)]
    return pl.pallas_call(
        kern, grid=(B, nc),
        in_specs=[col(QK_W, COL_GQ), col(QK_W, COL_GK), col(V_W, COL_GV), col(V_W, COL_GR),
                  BS((1, rows, 128), lambda b, c: (b, c, 0)),
                  BS((None, 128, QK_W), lambda b, c: (l, 0, 0)),
                  BS((None, 1, QK_W), lambda b, c: (l, 0, 0)),
                  BS((None, 1, V_W), lambda b, c: (l, 0, 0)),
                  BS((1, GLA_HEADS, GLA_DK, GLA_DV), lambda b, c: (b, 0, 0, 0))],
        out_specs=[BS((1, rows, V_W), lambda b, c: (b, c, 0)),
                   BS((1, GLA_HEADS, GLA_DK, GLA_DV), lambda b, c: (b, 0, 0, 0))],
        out_shape=[SDS((B, T, V_W), BF16), SDS((B, GLA_HEADS, GLA_DK, GLA_DV), F32)],
        scratch_shapes=scratch,
        compiler_params=_cparams("arbitrary", "arbitrary"))(P3, P3, P3, P3, G3, w2p, bias3, nw3, s0)


def _ret_kernel(q_ref, k_ref, v_ref, g_ref, cos_ref, sin_ref, dm_ref, rs_ref, ks_ref, cd_ref,
                nw_ref, s0_ref, o_ref, so_ref, S_ref, *pads, C, rows, nc):
    c = pl.program_id(1)

    @pl.when(c == 0)
    def _():
        _state_init(S_ref, s0_ref, RET_HEADS, RET_DK, RET_DV)

    pads = list(pads) + [None] * 4
    q = _load_rows(q_ref, pads[0], rows, C)
    k = _load_rows(k_ref, pads[1], rows, C)
    v = _load_rows(v_ref, pads[2], rows, C)
    g = _load_rows(g_ref, pads[3], rows, C)

    cos = cos_ref[...]
    sin = sin_ref[...]
    first_half = (lax.broadcasted_iota(jnp.int32, (C, QK_W), 1) % RET_DK) < (RET_DK // 2)

    def rope(x):
        partner = jnp.where(first_half, pltpu.roll(x, QK_W - RET_DK // 2, 1),
                            pltpu.roll(x, RET_DK // 2, 1))
        return x * cos + partner * sin

    qr = rope(q)
    kr = rope(k) * (RET_DK ** -0.5)
    qb = qr.astype(BF16)
    kb = kr.astype(BF16)
    vb = v.astype(BF16)
    k_state = (kr * ks_ref[...]).astype(BF16)

    S = S_ref[...]
    o_state = _dg(qb, S.astype(BF16), NN) * rs_ref[...]
    lane_head = lax.broadcasted_iota(jnp.int32, (C, QK_W), 1) // RET_DK
    nw = nw_ref[...]
    for h in range(RET_HEADS):
        a = _dg(jnp.where(lane_head == h, qb, jnp.zeros_like(qb)), kb, NT) * dm_ref[h]
        sl = slice(RET_DV * h, RET_DV * (h + 1))
        oh = _dg(a.astype(BF16), vb[:, sl], NN) + o_state[:, sl]
        oh = oh - jnp.mean(oh, axis=-1, keepdims=True)
        var = jnp.mean(oh * oh, axis=-1, keepdims=True)
        y = oh * lax.rsqrt(var + EPS) * nw[:, sl] * _silu(g[:, sl])
        o_ref[0, :, sl] = y[0:rows].astype(o_ref.dtype)

    u = _dg(k_state, vb, TN)
    S_ref[...] = cd_ref[...] * S + jnp.where(_block_diag_mask(), u, 0.0)

    @pl.when(c == nc - 1)
    def _():
        _state_out(so_ref, S_ref, RET_HEADS, RET_DK, RET_DV)


def _ret_constants(C, rows):
    log_gamma = np.log(1.0 - 2.0 ** (-5.0 - np.arange(RET_HEADS, dtype=np.float64)))
    bt = np.minimum(np.arange(C) + 1, rows).astype(np.float64)
    diff = bt[:, None] - bt[None, :]
    causal = np.arange(C)[:, None] >= np.arange(C)[None, :]
    dm = np.where(causal[None], np.exp(diff[None] * log_gamma[:, None, None]), 0.0)
    rs = np.exp(bt[:, None] * np.repeat(log_gamma, RET_DV)[None, :])
    ks = np.exp((bt[-1] - bt)[:, None] * np.repeat(log_gamma, RET_DK)[None, :])
    cd = np.exp(bt[-1] * np.repeat(log_gamma, RET_DV))[None, :]
    return (jnp.asarray(dm, F32), jnp.asarray(rs, F32), jnp.asarray(ks, F32), jnp.asarray(cd, F32))


def _ret(P3, cos_t, sin_t, nw3, s0, l, C, rows):
    B, T, _ = P3.shape
    nc = T // rows
    col = _mixer_specs(rows)
    dm, rs, ks, cd = _ret_constants(C, rows)
    kern = functools.partial(_ret_kernel, C=C, rows=rows, nc=nc)
    scratch = [pltpu.VMEM((QK_W, V_W), F32)]
    if rows < C:
        scratch += [pltpu.VMEM((C, w), F32) for w in (QK_W, QK_W, V_W, V_W)]
    const = lambda shape: BS(shape, lambda b, c: (0,) * len(shape))
    return pl.pallas_call(
        kern, grid=(B, nc),
        in_specs=[col(QK_W, COL_RQ), col(QK_W, COL_RK), col(V_W, COL_RV), col(V_W, COL_RG),
                  BS((C, QK_W), lambda b, c: (c, 0)), BS((C, QK_W), lambda b, c: (c, 0)),
                  const((RET_HEADS, C, C)), const((C, V_W)), const((C, QK_W)), const((1, V_W)),
                  BS((None, 1, V_W), lambda b, c: (l, 0, 0)),
                  BS((1, RET_HEADS, RET_DK, RET_DV), lambda b, c: (b, 0, 0, 0))],
        out_specs=[BS((1, rows, V_W), lambda b, c: (b, c, 0)),
                   BS((1, RET_HEADS, RET_DK, RET_DV), lambda b, c: (b, 0, 0, 0))],
        out_shape=[SDS((B, T, V_W), BF16), SDS((B, RET_HEADS, RET_DK, RET_DV), F32)],
        scratch_shapes=scratch,
        compiler_params=_cparams("arbitrary", "arbitrary"))(
            P3, P3, P3, P3, cos_t, sin_t, dm, rs, ks, cd, nw3, s0)


def _rope_tables(pos0, n):
    half = RET_DK // 2
    inv = ROPE_BASE ** (-jnp.arange(half, dtype=F32) / half)
    pos = (pos0 + jnp.arange(n, dtype=jnp.int32)).astype(F32)
    ang = pos[:, None] * inv[None, :]
    cos, sin = jnp.cos(ang), jnp.sin(ang)
    cos_t = jnp.tile(jnp.concatenate([cos, cos], -1), (1, RET_HEADS))
    sin_t = jnp.tile(jnp.concatenate([-sin, sin], -1), (1, RET_HEADS))
    return cos_t, sin_t


def _head_rms64(x, g_ref, w):
    xh, xm = _split2(x * x)
    ms = jnp.dot(xh, g_ref[...], preferred_element_type=F32) + jnp.dot(xm, g_ref[...], preferred_element_type=F32)
    return x * lax.rsqrt(ms + EPS) * w


def _qknorm_kernel(q_ref, k_ref, g_ref, qw_ref, kw_ref, qn_ref, kn_ref):
    qn_ref[...] = _head_rms64(q_ref[...], g_ref, qw_ref[...])
    kn_ref[...] = _head_rms64(k_ref[...], g_ref, kw_ref[...])


def _prompt_prep_kernel(q_ref, k_ref, v_ref, g_ref, pl_ref, eye_ref, qw_ref, kw_ref,
                        qn_ref, kn_ref, km_ref, kp_ref, vt_ref):
    i = pl.program_id(0)
    qn_ref[...] = _head_rms64(q_ref[...], g_ref, qw_ref[...])
    kn = _head_rms64(k_ref[...], g_ref, kw_ref[...])
    kn_ref[...] = kn
    km_ref[0] = jnp.mean(kn, axis=0, keepdims=True)
    lane = lax.broadcasted_iota(jnp.int32, (MOBA_BLOCK, 2 * MOBA_W), 1) % 128
    placed = jnp.dot(kn.astype(BF16), pl_ref[...], preferred_element_type=F32)
    kp_ref[...] = (placed + jnp.where(lane == MOBA_HD + i, BIG, 0.0)).astype(BF16)
    vt_ref[...] = _dg(eye_ref[...], v_ref[...].astype(BF16), NT).astype(BF16)


def _group_mean_matrix():
    r = np.arange(MOBA_W)
    return jnp.asarray((r[:, None] // MOBA_HD == r[None, :] // MOBA_HD) / MOBA_HD, BF16)


def _placement_matrix():
    r = np.arange(MOBA_W)
    c = np.arange(2 * MOBA_W)
    tgt = (r // MOBA_HD) * 128 + r % MOBA_HD
    return jnp.asarray(tgt[:, None] == c[None, :], BF16)


def _qknorm(P, qw3, kw3, l, tm):
    M = P.shape[0]
    return pl.pallas_call(
        _qknorm_kernel, grid=(M // tm,),
        in_specs=[BS((tm, MOBA_W), lambda i: (i, COL_MQ)), BS((tm, MOBA_W), lambda i: (i, COL_MK)),
                  BS((MOBA_W, MOBA_W), lambda i: (0, 0)),
                  BS((None, 1, MOBA_W), lambda i: (l, 0, 0)), BS((None, 1, MOBA_W), lambda i: (l, 0, 0))],
        out_specs=[BS((tm, MOBA_W), lambda i: (i, 0)), BS((tm, MOBA_W), lambda i: (i, 0))],
        out_shape=[SDS((M, MOBA_W), F32), SDS((M, MOBA_W), F32)],
        compiler_params=_cparams("arbitrary"))(P, P, _group_mean_matrix(), qw3, kw3)


def _prompt_prep(P, qw3, kw3, l):
    T = P.shape[0]
    nb = T // MOBA_BLOCK
    tm = MOBA_BLOCK
    row = lambda w, idx: BS((tm, w), lambda i: (i, idx))
    return pl.pallas_call(
        _prompt_prep_kernel, grid=(nb,),
        in_specs=[row(MOBA_W, COL_MQ), row(MOBA_W, COL_MK), row(MOBA_W, COL_MV),
                  BS((MOBA_W, MOBA_W), lambda i: (0, 0)),
                  BS((MOBA_W, 2 * MOBA_W), lambda i: (0, 0)),
                  BS((MOBA_W, MOBA_W), lambda i: (0, 0)),
                  BS((None, 1, MOBA_W), lambda i: (l, 0, 0)), BS((None, 1, MOBA_W), lambda i: (l, 0, 0))],
        out_specs=[row(MOBA_W, 0), row(MOBA_W, 0), BS((1, 1, MOBA_W), lambda i: (i, 0, 0)),
                   row(2 * MOBA_W, 0), BS((MOBA_W, tm), lambda i: (0, i))],
        out_shape=[SDS((T, MOBA_W), F32), SDS((T, MOBA_W), F32), SDS((nb, 1, MOBA_W), F32),
                   SDS((T, 2 * MOBA_W), BF16), SDS((MOBA_W, T), BF16)],
        compiler_params=_cparams("arbitrary"))(
            P, P, P, _group_mean_matrix(), _placement_matrix(), jnp.eye(MOBA_W, dtype=BF16), qw3, kw3)


def _placed_block_means(kmean):
    nb = kmean.shape[0]
    km = kmean.reshape(nb, MOBA_HEADS, MOBA_HD).transpose(1, 2, 0)
    km = jnp.pad(km, ((0, 0), (0, 0), (MOBA_HD, MOBA_HD - nb)))
    full = km[:, :, None, :] * jnp.eye(MOBA_HEADS, dtype=F32)[:, None, :, None]
    return full.reshape(MOBA_W, 2 * MOBA_W)


def _select_top3(gate, valid, axis=-1):
    g1 = jnp.where(valid, gate, NEG)
    m1 = jnp.max(g1, axis=axis, keepdims=True)
    g2 = jnp.where(g1 >= m1, NEG, g1)
    m2 = jnp.max(g2, axis=axis, keepdims=True)
    g3 = jnp.where(g2 >= m2, NEG, g2)
    m3 = jnp.max(g3, axis=axis, keepdims=True)
    return valid & (g1 >= m3)


def _gate_kernel(qn_ref, kmt_ref, plt_ref, qpt_ref):
    i = pl.program_id(0)
    qn = qn_ref[...]
    gate = _dg(kmt_ref[...].astype(BF16), qn.astype(BF16), NT)
    qpart = _dg(plt_ref[...], (qn * (MOBA_HD ** -0.5 * LOG2E)).astype(BF16), NT)
    blk = lax.broadcasted_iota(jnp.int32, (128, MOBA_BLOCK), 0) - MOBA_HD
    valid = (blk >= 0) & (blk < i)
    for h in range(MOBA_HEADS):
        sl = slice(128 * h, 128 * (h + 1))
        sel = _select_top3(gate[sl], valid, axis=0) | (blk == i)
        maskpart = jnp.where((blk >= 0) & jnp.logical_not(sel), -1.0, 0.0)
        qpt_ref[sl, :] = (qpart[sl] + maskpart).astype(BF16)


def _gate(qn, km_placed_t):
    T = qn.shape[0]
    nb = T // MOBA_BLOCK
    return pl.pallas_call(
        _gate_kernel, name="moba_gate", grid=(nb,),
        in_specs=[BS((MOBA_BLOCK, MOBA_W), lambda i: (i, 0)),
                  BS((2 * MOBA_W, MOBA_W), lambda i: (0, 0)),
                  BS((2 * MOBA_W, MOBA_W), lambda i: (0, 0))],
        out_specs=BS((2 * MOBA_W, MOBA_BLOCK), lambda i: (0, i)),
        out_shape=SDS((2 * MOBA_W, T), BF16),
        compiler_params=_cparams("arbitrary"))(qn, km_placed_t, _placement_matrix().T)


def _bucket_np(dist):
    dist = np.maximum(dist, 0)
    max_exact = N_BUCKETS // 2
    large = max_exact + (np.log(np.maximum(dist, max_exact).astype(np.float64) / max_exact)
                         / math.log(MAX_DISTANCE / max_exact) * (N_BUCKETS - max_exact)).astype(np.int64)
    large = np.minimum(large, N_BUCKETS - 1)
    return np.where(dist < max_exact, dist, large).astype(np.int32)


def _bias_kernel(rb_ref, idx_ref, ok_ref, o_ref, *, scale):
    h = pl.program_id(0)
    idx = idx_ref[...]
    far = rb_ref[N_BUCKETS - 1, h]
    acc = jnp.zeros(idx.shape, F32)
    for b in range(N_BUCKETS - 1):
        acc = jnp.where(idx == b, (rb_ref[b, h] - far) * scale, acc)
    o_ref[0] = jnp.where(ok_ref[...] > 0, acc, -BIG)


def _bias_tables(rel_bias, dist, scale=1.0):
    idx = jnp.asarray(_bucket_np(dist))
    ok = jnp.asarray((dist >= 0).astype(np.int32))
    shp = dist.shape
    zeros = (0,) * len(shp)
    return pl.pallas_call(
        functools.partial(_bias_kernel, scale=scale), name="moba_bias_tables", grid=(MOBA_HEADS,),
        in_specs=[BS(memory_space=pltpu.SMEM), BS(shp, lambda h: zeros), BS(shp, lambda h: zeros)],
        out_specs=BS((1,) + shp, lambda h: (h,) + zeros),
        out_shape=SDS((MOBA_HEADS,) + shp, F32),
        compiler_params=_cparams("arbitrary"))(rel_bias, idx, ok)


def _attn_kernel(qpt_ref, kp_ref, vt_ref, bt_ref, eye_ref, o_ref, m_ref, l_ref, acc_ref, s_ref):
    i = pl.program_id(1)
    tq = MOBA_BLOCK
    m_ref[...] = jnp.full(m_ref.shape, NEG, F32)
    l_ref[...] = jnp.zeros_like(l_ref)
    acc_ref[...] = jnp.zeros_like(acc_ref)

    def logits(slot, hh, n, t):
        qt = qpt_ref[128 * hh:128 * (hh + 1), :]
        start = pl.multiple_of(n * MOBA_BLOCK, MOBA_BLOCK)
        kblk = kp_ref[pl.ds(start, MOBA_BLOCK), 128 * hh:128 * (hh + 1)]
        s = jnp.dot(kblk, qt, preferred_element_type=F32)
        if t is not None:
            s = s + bt_ref[hh, t]
        s_ref[slot] = s

    def softmax_pv(slot, hh, n):
        ps, alphas = [], []
        for c in range(tq // ATTN_COLS):
            cols = slice(ATTN_COLS * c, ATTN_COLS * (c + 1))
            sc = s_ref[slot, :, cols]
            m_old = m_ref[hh, :, cols]
            m_new = jnp.maximum(m_old, jnp.max(sc, axis=0, keepdims=True))
            alpha = jnp.exp2(m_old - m_new)
            p = jnp.exp2(sc - m_new)
            l_ref[hh, :, cols] = alpha * l_ref[hh, :, cols] + jnp.sum(p, axis=0, keepdims=True)
            m_ref[hh, :, cols] = m_new
            ps.append(p.astype(BF16))
            alphas.append(alpha)
        start = pl.multiple_of(n * MOBA_BLOCK, MOBA_BLOCK)
        vt = vt_ref[MOBA_HD * hh:MOBA_HD * (hh + 1), pl.ds(start, MOBA_BLOCK)]
        pv = jnp.dot(vt, jnp.concatenate(ps, axis=1), preferred_element_type=F32)
        acc_ref[hh] = jnp.concatenate(alphas, axis=1) * acc_ref[hh] + pv

    def group(tiles):
        for slot, (hh, n, t) in enumerate(tiles):
            logits(slot, hh, n, t)
        for slot, (hh, n, t) in enumerate(tiles):
            softmax_pv(slot, hh, n)

    n_far = jnp.maximum(i - 1, 0)

    def far_tiles(first, count):
        return [(hh, first + b, None) for b in range(count) for hh in range(2)]

    def far_body(j, carry):
        group(far_tiles(ATTN_UNROLL * j, ATTN_UNROLL))
        return carry

    lax.fori_loop(0, n_far // ATTN_UNROLL, far_body, 0)
    done = (n_far // ATTN_UNROLL) * ATTN_UNROLL
    width = ATTN_UNROLL // 2
    while width >= 1:
        @pl.when((n_far - done) % (2 * width) >= width)
        def _(done=done, width=width):
            group(far_tiles(done, width))
        done = done + jnp.where((n_far - done) % (2 * width) >= width, width, 0)
        width //= 2

    @pl.when(i >= 1)
    def _():
        group([(0, i - 1, 1), (1, i - 1, 1), (0, i, 0), (1, i, 0)])

    @pl.when(i == 0)
    def _():
        group([(0, i, 0), (1, i, 0)])

    out_t = jnp.concatenate([acc_ref[0] / l_ref[0], acc_ref[1] / l_ref[1]], axis=0).astype(BF16)
    o_ref[...] = _dg(out_t, eye_ref[...], TN).astype(o_ref.dtype)


def _prompt_attention(qpt, kp, vt, btab_t):
    T = kp.shape[0]
    nb = T // MOBA_BLOCK
    return pl.pallas_call(
        _attn_kernel, name="moba_prompt_attention", grid=(MOBA_HEADS // 2, nb),
        in_specs=[BS((256, MOBA_BLOCK), lambda hp, i: (hp, i)),
                  BS((T, 256), lambda hp, i: (0, hp)),
                  BS((128, T), lambda hp, i: (hp, 0)),
                  BS((2, 2, MOBA_BLOCK, MOBA_BLOCK), lambda hp, i: (hp, 0, 0, 0)),
                  BS((128, 128), lambda hp, i: (0, 0))],
        out_specs=BS((MOBA_BLOCK, 128), lambda hp, i: (i, hp)),
        out_shape=SDS((T, MOBA_W), BF16),
        scratch_shapes=[pltpu.VMEM((2, 1, MOBA_BLOCK), F32), pltpu.VMEM((2, 1, MOBA_BLOCK), F32),
                        pltpu.VMEM((2, MOBA_HD, MOBA_BLOCK), F32),
                        pltpu.VMEM((2 * ATTN_UNROLL, MOBA_BLOCK, MOBA_BLOCK), F32)],
        compiler_params=_cparams("arbitrary", "arbitrary"))(qpt, kp, vt, btab_t, jnp.eye(128, dtype=BF16))


def _decode_gate_kernel(pt_ref, *refs):
    npg = PAGES_PER_STEP
    pages, (q_ref, g_ref) = refs[:npg], refs[npg:]
    qb = _bf16_round(q_ref[0])
    pages_per_block = MOBA_BLOCK // pages[0].shape[-1]
    for blk in range(npg // pages_per_block):
        tokens = pages[pages_per_block * blk][...]
        for g in range(1, pages_per_block):
            tokens = tokens + pages[pages_per_block * blk + g][...]
        km = jnp.sum(tokens, axis=-1, keepdims=True) * (1.0 / MOBA_BLOCK)
        g_ref[0, blk] = jnp.sum(_bf16_round(km) * qb, axis=1)


def _decode_gates(page_table, cache_t, q_t, l):
    nseq, n_pages = page_table.shape
    page = cache_t.shape[-1]
    npg = PAGES_PER_STEP
    ppb = MOBA_BLOCK // page
    page_specs = [BS((None, None, MOBA_HEADS, MOBA_HD, page), functools.partial(
        lambda b, g, pt, r: (l, pt[b, g * npg + r], 0, 0, 0), r=r)) for r in range(npg)]
    grid_spec = pltpu.PrefetchScalarGridSpec(
        num_scalar_prefetch=1, grid=(nseq, n_pages // npg),
        in_specs=page_specs + [BS((1, MOBA_HEADS, MOBA_HD, 128), lambda b, g, pt: (b, 0, 0, 0))],
        out_specs=BS((1, npg // ppb, MOBA_HEADS, 128), lambda b, g, pt: (b, g, 0, 0)))
    return pl.pallas_call(
        _decode_gate_kernel, name="decode_gates", grid_spec=grid_spec,
        out_shape=SDS((nseq, n_pages // ppb, MOBA_HEADS, 128), F32),
        compiler_params=_cparams("arbitrary", "arbitrary"))(page_table, *([cache_t] * npg), q_t)


def _decode_select_kernel(g_ref, o_ref):
    gate = g_ref[0]
    nblk = gate.shape[0]
    blk = lax.broadcasted_iota(jnp.int32, gate.shape, 0)
    for t in range(MOBA_TOPK):
        m = jnp.max(gate, axis=0, keepdims=True)
        idx = jnp.min(jnp.where(gate >= m, blk, nblk), axis=0, keepdims=True)
        o_ref[0, t] = idx[0]
        gate = jnp.where(blk == idx, NEG, gate)


def _decode_select(gates):
    nseq, nblk = gates.shape[:2]
    return pl.pallas_call(
        _decode_select_kernel, name="decode_select", grid=(nseq,),
        in_specs=[BS((1, nblk, MOBA_HEADS, 128), lambda b: (b, 0, 0, 0))],
        out_specs=BS((1, MOBA_TOPK, MOBA_HEADS, 128), lambda b: (b, 0, 0, 0)),
        out_shape=SDS((nseq, MOBA_TOPK, MOBA_HEADS, 128), jnp.int32),
        compiler_params=_cparams("arbitrary"))(gates)


def _bf16_round(x):
    return x.astype(BF16).astype(F32)


def _decode_attend_kernel(pt_ref, sel_ref, ck_ref, cv_ref, q_ref, kn_ref, vn_ref, bpast_ref, bown_ref, o_ref,
                          kbuf, vbuf, sem, *, l, nd, nblk, page):
    b = pl.program_id(0)
    rows = MOBA_HEADS * nd
    trips = rows // DECODE_ROWS
    pages_per_block = MOBA_BLOCK // page

    def copies(i, slot):
        out = []
        for u in range(DECODE_ROWS):
            r = DECODE_ROWS * i + u
            h = r // nd
            for t in range(MOBA_TOPK):
                n = sel_ref[b, MOBA_TOPK * r + t]
                for g in range(pages_per_block):
                    pg = pt_ref[b, pages_per_block * n + g]
                    dst = slice(page * g, page * (g + 1))
                    out.append(pltpu.make_async_copy(ck_ref.at[l, pg, h], kbuf.at[slot, u, t, :, dst],
                                                     sem.at[slot, 0, u, t, g]))
                    out.append(pltpu.make_async_copy(cv_ref.at[l, pg, h], vbuf.at[slot, u, t, :, dst],
                                                     sem.at[slot, 1, u, t, g]))
        return out

    for c in copies(0, 0):
        c.start()

    def body(i, carry):
        slot = i % 2

        @pl.when(i + 1 < trips)
        def _():
            for c in copies(i + 1, 1 - slot):
                c.start()

        for c in copies(i, slot):
            c.wait()

        steps = [attend_row(DECODE_ROWS * i + u, kbuf.at[slot, u], vbuf.at[slot, u]) for u in range(DECODE_ROWS)]
        for _ in range(3):
            for st in steps:
                next(st, None)
        return carry

    def attend_row(r, kbuf, vbuf):
        h = r // nd
        qv = q_ref[0, r]
        qmat = jnp.broadcast_to(qv, (8, MOBA_HD)).astype(BF16)
        logits = []
        for t in range(MOBA_TOPK):
            s = jnp.dot(qmat, kbuf[t].astype(BF16), preferred_element_type=F32)
            is_last = sel_ref[b, MOBA_TOPK * r + t] == nblk - 1
            logits.append(s + jnp.where(is_last, bpast_ref[pl.ds(r, 1), :], 0.0))
        yield
        s_own = jnp.sum(_bf16_round(kn_ref[0, h]) * _bf16_round(qv), axis=-1, keepdims=True) + bown_ref[r]
        m = jnp.max(s_own, axis=0, keepdims=True)
        for s in logits:
            m = jnp.maximum(m, jnp.max(s[0:1], axis=-1, keepdims=True))
        p_own = jnp.exp(s_own - m)
        denom = jnp.sum(p_own, axis=0, keepdims=True)
        probs = [jnp.exp(s - m) for s in logits]
        for p in probs:
            denom = denom + jnp.sum(p[0:1], axis=-1, keepdims=True)
        inv = 1.0 / denom
        yield
        acc = jnp.zeros((8, MOBA_HD), F32)
        for t in range(MOBA_TOPK):
            acc = acc + _dg((probs[t] * inv).astype(BF16), vbuf[t].astype(BF16), NT)
        own = jnp.sum(_bf16_round(p_own * inv) * _bf16_round(vn_ref[0, h]), axis=0, keepdims=True)
        o_ref[0, r] = acc[0:1] + own

    lax.fori_loop(0, trips, body, 0)


def _decode_attend(page_table, sel, cache_k, cache_v, q_rows, kn_new, vn_new, b_past, b_own, l, nd):
    nseq, n_pages = page_table.shape
    page = cache_k.shape[-1]
    rows = q_rows.shape[1]
    per_seq = lambda shape: BS((1,) + shape, lambda b, pt, sl: (b,) + (0,) * len(shape))
    const = lambda shape: BS(shape, lambda b, pt, sl: (0,) * len(shape))
    assert rows % DECODE_ROWS == 0
    slab = pltpu.VMEM((2, DECODE_ROWS, MOBA_TOPK, MOBA_HD, MOBA_BLOCK), F32)
    grid_spec = pltpu.PrefetchScalarGridSpec(
        num_scalar_prefetch=2, grid=(nseq,),
        in_specs=[BS(memory_space=pl.ANY), BS(memory_space=pl.ANY),
                  per_seq((rows, 1, MOBA_HD)), per_seq((MOBA_HEADS, 8, MOBA_HD)), per_seq((MOBA_HEADS, 8, MOBA_HD)),
                  const((rows, MOBA_BLOCK)), const((rows, 8, 1))],
        out_specs=per_seq((rows, 1, MOBA_HD)),
        scratch_shapes=[slab, slab, pltpu.SemaphoreType.DMA((2, 2, DECODE_ROWS, MOBA_TOPK, MOBA_BLOCK // page))])
    kern = functools.partial(_decode_attend_kernel, l=l, nd=nd, nblk=n_pages * page // MOBA_BLOCK, page=page)
    return pl.pallas_call(
        kern, name="decode_attend", grid_spec=grid_spec,
        out_shape=SDS((nseq, rows, 1, MOBA_HD), F32),
        compiler_params=_cparams("arbitrary"))(page_table, sel, cache_k, cache_v, q_rows, kn_new, vn_new, b_past, b_own)


def _merge_kernel(oa_ref, ob_ref, oc_ref, mg_ref, x_ref, g1_ref, wa_ref, wb_ref, wc_ref, wo_ref, o_ref):
    mg = mg_ref[...]
    ya = jnp.dot(oa_ref[...], wa_ref[...], preferred_element_type=F32)
    yb = jnp.dot(ob_ref[...], wb_ref[...], preferred_element_type=F32)
    yc = jnp.dot(oc_ref[...], wc_ref[...], preferred_element_type=F32)
    merged = (_sigmoid(mg[:, 0:D_MODEL]) * ya + _sigmoid(mg[:, D_MODEL:2 * D_MODEL]) * yb
              + _sigmoid(mg[:, 2 * D_MODEL:]) * yc)
    mix = jnp.dot(merged.astype(BF16), wo_ref[...], preferred_element_type=F32)
    o_ref[...] = x_ref[...] + g1_ref[...] * mix


def _merge(oa, ob, oc, P, x, g1, wa, wb, wc, wo, l, tm):
    M = x.shape[0]
    row = lambda w: BS((tm, w), lambda i: (i, 0))
    wspec = lambda k: BS((None, k, D_MODEL), lambda i: (l, 0, 0))
    return pl.pallas_call(
        _merge_kernel, grid=(M // tm,),
        in_specs=[row(V_W), row(V_W), row(MOBA_W), row(3 * D_MODEL), row(D_MODEL),
                  _mod_spec(g1.shape[0], tm, 1),
                  wspec(V_W), wspec(V_W), wspec(MOBA_W), wspec(D_MODEL)],
        out_specs=row(D_MODEL),
        out_shape=SDS((M, D_MODEL), F32),
        compiler_params=_cparams("arbitrary"))(oa, ob, oc, P, x, g1, wa, wb, wc, wo)


def _ffn_kernel(x_ref, nw_ref, sc_ref, sh_ref, g2_ref, w1_ref, w3_ref, w2_ref, o_ref, h_ref, acc_ref):
    j = pl.program_id(1)

    @pl.when(j == 0)
    def _():
        h_ref[...] = _modulated_norm(x_ref[...], nw_ref[...], sc_ref[...], sh_ref[...]).astype(BF16)
        acc_ref[...] = jnp.zeros_like(acc_ref)

    h = h_ref[...]
    a = jnp.dot(h, w1_ref[...], preferred_element_type=F32)
    b = jnp.dot(h, w3_ref[...], preferred_element_type=F32)
    acc_ref[...] += jnp.dot((_silu(a) * b).astype(BF16), w2_ref[...], preferred_element_type=F32)

    @pl.when(j == pl.num_programs(1) - 1)
    def _():
        o_ref[...] = x_ref[...] + g2_ref[...] * acc_ref[...]


def _ffn(x, nw3, sc, sh, g2, w1, w3, w2, l, li, tm):
    M = x.shape[0]
    tf = 256
    ms = _mod_spec(sc.shape[0], tm, 2)
    return pl.pallas_call(
        _ffn_kernel, grid=(M // tm, D_FF // tf),
        in_specs=[BS((tm, D_MODEL), lambda i, j: (i, 0)),
                  BS((None, 1, D_MODEL), lambda i, j: (l, 0, 0)), ms, ms, ms,
                  BS((None, D_MODEL, tf), lambda i, j: (li, 0, j)),
                  BS((None, D_MODEL, tf), lambda i, j: (li, 0, j)),
                  BS((None, tf, D_MODEL), lambda i, j: (li, j, 0))],
        out_specs=BS((tm, D_MODEL), lambda i, j: (i, 0)),
        out_shape=SDS((M, D_MODEL), F32),
        scratch_shapes=[pltpu.VMEM((tm, D_MODEL), BF16), pltpu.VMEM((tm, D_MODEL), F32)],
        compiler_params=_cparams("arbitrary", "arbitrary"))(x, nw3, sc, sh, g2, w1, w3, w2)


def _moe_kernel(x_ref, nw_ref, sc_ref, sh_ref, g2_ref, wr_ref, w1_ref, w3_ref, w2_ref, o_ref,
                h_ref, gate_ref, acc_ref):
    e = pl.program_id(1)
    tm = x_ref.shape[0]
    lane = lax.broadcasted_iota(jnp.int32, (tm, 128), 1)

    @pl.when(e == 0)
    def _():
        h = _modulated_norm(x_ref[...], nw_ref[...], sc_ref[...], sh_ref[...])
        h_ref[...] = h.astype(BF16)
        router = jnp.dot(h.astype(BF16), wr_ref[...].astype(BF16), preferred_element_type=F32)
        logits = jnp.where(lane < N_EXPERTS, router, NEG)
        m1 = jnp.max(logits, axis=-1, keepdims=True)
        i1 = jnp.min(jnp.where(logits >= m1, lane, 128), axis=-1, keepdims=True)
        rest = jnp.where(lane == i1, NEG, logits)
        m2 = jnp.max(rest, axis=-1, keepdims=True)
        i2 = jnp.min(jnp.where(rest >= m2, lane, 128), axis=-1, keepdims=True)
        e2 = jnp.exp(m2 - m1)
        w_first = 1.0 / (1.0 + e2)
        gate_ref[...] = jnp.where(lane == i1, w_first, 0.0) + jnp.where(lane == i2, e2 * w_first, 0.0)
        acc_ref[...] = jnp.zeros_like(acc_ref)

    h = h_ref[...]
    a = jnp.dot(h, w1_ref[...], preferred_element_type=F32)
    b = jnp.dot(h, w3_ref[...], preferred_element_type=F32)
    y = jnp.dot((_silu(a) * b).astype(BF16), w2_ref[...], preferred_element_type=F32)
    ge = jnp.sum(jnp.where(lane == e, gate_ref[...], 0.0), axis=-1, keepdims=True)
    acc_ref[...] += ge * y

    @pl.when(e == pl.num_programs(1) - 1)
    def _():
        o_ref[...] = x_ref[...] + g2_ref[...] * acc_ref[...]


def _moe(x, nw3, sc, sh, g2, wr, w1, w3, w2, l, li, tm):
    M = x.shape[0]
    ms = _mod_spec(sc.shape[0], tm, 2)
    return pl.pallas_call(
        _moe_kernel, grid=(M // tm, N_EXPERTS),
        in_specs=[BS((tm, D_MODEL), lambda i, e: (i, 0)),
                  BS((None, 1, D_MODEL), lambda i, e: (l, 0, 0)), ms, ms, ms,
                  BS((None, D_MODEL, 128), lambda i, e: (li, 0, 0)),
                  BS((None, None, D_MODEL, D_FF_EXPERT), lambda i, e: (li, e, 0, 0)),
                  BS((None, None, D_MODEL, D_FF_EXPERT), lambda i, e: (li, e, 0, 0)),
                  BS((None, None, D_FF_EXPERT, D_MODEL), lambda i, e: (li, e, 0, 0))],
        out_specs=BS((tm, D_MODEL), lambda i, e: (i, 0)),
        out_shape=SDS((M, D_MODEL), F32),
        scratch_shapes=[pltpu.VMEM((tm, D_MODEL), BF16), pltpu.VMEM((tm, 128), F32),
                        pltpu.VMEM((tm, D_MODEL), F32)],
        compiler_params=_cparams("arbitrary", "arbitrary"))(x, nw3, sc, sh, g2, wr, w1, w3, w2)


def kernel(x_prompt, x_sample, cache_k, cache_v, state_gla, state_ret, page_table, c_prompt, c_sample,
           ada_w, ada_b, norm1, norm2, w_in, gla_gk_w2, gla_gk_b, gla_norm, ret_norm, moba_qnorm,
           moba_knorm, rel_bias, w_br_gla, w_br_ret, w_br_moba, w_out, ffn_w1, ffn_w3, ffn_w2,
           moe_router, moe_w1, moe_w3, moe_w2):
    depth = w_in.shape[0]
    bp, T, _ = x_prompt.shape
    nseq, nd, _ = x_sample.shape
    n_pages, page = page_table.shape[1], cache_k.shape[2]
    past_len = n_pages * page
    nb = T // MOBA_BLOCK
    assert bp == 1 and T % MOBA_BLOCK == 0 and nb <= MOBA_HD and page == 128
    assert past_len % MOBA_BLOCK == 0 and n_pages % PAGES_PER_STEP == 0 and nd <= 8
    ms_rows = nseq * nd
    nq = nd * MOBA_HEADS

    sizes = np.cumsum([QK_W, QK_W, V_W, GLA_LOWRANK, V_W, QK_W, QK_W, V_W, V_W, MOBA_W, MOBA_W, MOBA_W])
    glr0, glr1, mg0 = int(sizes[2]), int(sizes[3]), int(sizes[-1])
    w_main = jnp.concatenate([w_in[:, :, mg0:], w_in[:, :, :glr0], w_in[:, :, glr1:mg0]], axis=2).astype(BF16)
    w_glr = jnp.pad(w_in[:, :, glr0:glr1], ((0, 0), (0, 0), (0, 128 - GLA_LOWRANK))).astype(BF16)
    w2p = jnp.pad(gla_gk_w2, ((0, 0), (0, 128 - GLA_LOWRANK), (0, 0)))
    gkb3 = gla_gk_b[:, None, :]
    gla_nw3 = jnp.tile(gla_norm, (1, GLA_HEADS))[:, None, :]
    ret_nw3 = jnp.tile(ret_norm, (1, RET_HEADS))[:, None, :]
    qw3 = jnp.tile(moba_qnorm, (1, MOBA_HEADS))[:, None, :]
    kw3 = jnp.tile(moba_knorm, (1, MOBA_HEADS))[:, None, :]
    n1_3, n2_3 = norm1[:, None, :], norm2[:, None, :]
    ada_b3 = ada_b[:, None, :]
    wa, wb, wc, wo = (w.astype(BF16) for w in (w_br_gla, w_br_ret, w_br_moba, w_out))
    f1, f3, f2 = ffn_w1.astype(BF16), ffn_w3.astype(BF16), ffn_w2.astype(BF16)
    e1, e3, e2 = moe_w1.astype(BF16), moe_w3.astype(BF16), moe_w2.astype(BF16)
    wr = jnp.pad(moe_router, ((0, 0), (0, 0), (0, 128 - N_EXPERTS)))

    cos_p, sin_p = _rope_tables(0, T)
    cos_s, sin_s = _rope_tables(past_len, CHUNK)
    tq = np.arange(MOBA_BLOCK)
    d_own = tq[:, None] - tq[None, :]
    btab = _bias_tables(rel_bias, np.stack([d_own.T, d_own.T + MOBA_BLOCK]), LOG2E)
    qi = np.arange(8)
    d_past = (MOBA_BLOCK + qi[:, None] - tq[None, :])
    d_new = np.where((qi[None, :] < nd) & (qi[:, None] < nd), qi[:, None] - qi[None, :], -1)
    bt_past = _bias_tables(rel_bias, d_past)[:, :nd].reshape(nq, MOBA_BLOCK)
    bt_own = _bias_tables(rel_bias, np.pad(d_new, ((0, 0), (0, 120)), constant_values=-1))
    bt_own = bt_own[:, :nd, :8].reshape(nq, 8, 1)
    ck_t = cache_k.transpose(0, 1, 3, 4, 2)
    cv_t = cache_v.transpose(0, 1, 3, 4, 2)

    c_all = jnp.concatenate([c_prompt, c_sample, jnp.zeros((-(bp + nseq) % 8, D_MODEL), F32)], axis=0)
    zero_gla = jnp.zeros((bp, GLA_HEADS, GLA_DK, GLA_DV), F32)
    zero_ret = jnp.zeros((bp, RET_HEADS, RET_DK, RET_DV), F32)

    xp = x_prompt.reshape(T, D_MODEL)
    xs = x_sample.reshape(ms_rows, D_MODEL)
    outs = {k: [] for k in ("kp", "vp", "gp", "rp", "ks", "vs", "gs", "rs")}
    for l in range(depth):
        mod = _ada_mod(c_all, ada_w, ada_b3, l)
        mp = [mod[0:1, j * D_MODEL:(j + 1) * D_MODEL] for j in range(6)]
        msm = [jnp.repeat(mod[bp:bp + nseq, j * D_MODEL:(j + 1) * D_MODEL], nd, axis=0) for j in range(6)]
        li = l // 2

        def channel(x, m, tm):
            if l % 2 == 0:
                return _ffn(x, n2_3, m[4], m[3], m[5], f1, f3, f2, l, li, tm)
            return _moe(x, n2_3, m[4], m[3], m[5], wr, e1, e3, e2, l, li, tm)

        P, G = _inproj(xp, n1_3, mp[1], mp[0], w_main, w_glr, l, 1024 if T % 1024 == 0 else MOBA_BLOCK)
        P3, G3 = P.reshape(1, T, N_MAIN), G.reshape(1, T, 128)
        oa, sg = _gla(P3, G3, w2p, gkb3, gla_nw3, zero_gla, l, CHUNK, CHUNK)
        ob, sr = _ret(P3, cos_p, sin_p, ret_nw3, zero_ret, l, CHUNK, CHUNK)
        qn, kn, kmean, kp, vt = _prompt_prep(P, qw3, kw3, l)
        qpt = _gate(qn, _placed_block_means(kmean.reshape(nb, MOBA_W)).T)
        oc = _prompt_attention(qpt, kp, vt, btab)
        xm = _merge(oa.reshape(T, V_W), ob.reshape(T, V_W), oc, P, xp, mp[2], wa, wb, wc, wo, l, 512 if T % 512 == 0 else MOBA_BLOCK)
        xp = channel(xm, mp, 512 if T % 512 == 0 else MOBA_BLOCK)
        outs["kp"].append(kn.reshape(bp, T, MOBA_HEADS, MOBA_HD))
        outs["vp"].append(P[:, COL_MV * MOBA_W:(COL_MV + 1) * MOBA_W].reshape(bp, T, MOBA_HEADS, MOBA_HD))
        outs["gp"].append(sg)
        outs["rp"].append(sr)

        Ps, Gs = _inproj(xs, n1_3, msm[1], msm[0], w_main, w_glr, l, ms_rows)
        Ps3, Gs3 = Ps.reshape(nseq, nd, N_MAIN), Gs.reshape(nseq, nd, 128)
        oas, sgs = _gla(Ps3, Gs3, w2p, gkb3, gla_nw3, state_gla[l], l, CHUNK, nd)
        obs, srs = _ret(Ps3, cos_s, sin_s, ret_nw3, state_ret[l], l, CHUNK, nd)
        qns, kns = _qknorm(Ps, qw3, kw3, l, ms_rows)
        vns = Ps[:, COL_MV * MOBA_W:(COL_MV + 1) * MOBA_W]
        qs4 = (qns * (MOBA_HD ** -0.5)).reshape(nseq, nd, MOBA_HEADS, MOBA_HD).transpose(0, 2, 1, 3)
        q_t = jnp.pad(qs4.transpose(0, 1, 3, 2), ((0, 0), (0, 0), (0, 0), (0, 128 - nd)))
        top = _decode_select(_decode_gates(page_table, ck_t, q_t, l))
        sel = top[:, :, :, :nd].transpose(0, 2, 3, 1).reshape(nseq, nq * MOBA_TOPK)
        head_major = lambda a: jnp.pad(a.reshape(nseq, nd, MOBA_HEADS, MOBA_HD).transpose(0, 2, 1, 3),
                                       ((0, 0), (0, 0), (0, 8 - nd), (0, 0)))
        ocs4 = _decode_attend(page_table, sel, ck_t, cv_t, qs4.reshape(nseq, nq, 1, MOBA_HD),
                              head_major(kns), head_major(vns), bt_past, bt_own, l, nd)
        ocs = ocs4.reshape(nseq, MOBA_HEADS, nd, MOBA_HD).transpose(0, 2, 1, 3)
        xms = _merge(oas.reshape(ms_rows, V_W), obs.reshape(ms_rows, V_W), ocs.reshape(ms_rows, MOBA_W).astype(BF16),
                     Ps, xs, msm[2], wa, wb, wc, wo, l, ms_rows)
        xs = channel(xms, msm, ms_rows)
        outs["ks"].append(kns.reshape(nseq, nd, MOBA_HEADS, MOBA_HD))
        outs["vs"].append(vns.reshape(nseq, nd, MOBA_HEADS, MOBA_HD))
        outs["gs"].append(sgs)
        outs["rs"].append(srs)

    st = lambda k: jnp.stack(outs[k])
    return (xp.reshape(bp, T, D_MODEL), xs.reshape(nseq, nd, D_MODEL), st("kp"), st("vp"), st("gp"), st("rp"),
            st("ks"), st("vs"), st("gs"), st("rs"))
```

```python
import functools
import math

import numpy as np
import jax
import jax.numpy as jnp
from jax import lax
from jax.experimental import pallas as pl
from jax.experimental.pallas import tpu as pltpu

F32 = jnp.float32
BF16 = jnp.bfloat16
BS = pl.BlockSpec
SDS = jax.ShapeDtypeStruct

D_MODEL = 1024
GLA_HEADS, GLA_DK, GLA_DV, GLA_LOWRANK, GLA_GATE_TEMP = 4, 64, 128, 16, 16.0
RET_HEADS, RET_DK, RET_DV = 4, 64, 128
ROPE_BASE = 10000.0
MOBA_HEADS, MOBA_HD, MOBA_BLOCK, MOBA_TOPK = 8, 64, 256, 3
N_BUCKETS, MAX_DISTANCE = 32, 128
D_FF, N_EXPERTS, TOP_K, D_FF_EXPERT = 2816, 8, 2, 1408
EPS = 1e-6
BIG = 1e30
NEG = -3.0e38
LOG2E = math.log2(math.e)

QK_W = GLA_HEADS * GLA_DK
V_W = GLA_HEADS * GLA_DV
MOBA_W = MOBA_HEADS * MOBA_HD
N_MAIN = 7680
COL_GQ, COL_GK, COL_RQ, COL_RK = 12, 13, 18, 19
COL_GV, COL_GR, COL_RV, COL_RG, COL_MQ, COL_MK, COL_MV = 7, 8, 10, 11, 12, 13, 14

VMEM_LIMIT_BYTES = 56 * 1024 * 1024
CHUNK = 128
ATTN_COLS = 128
ATTN_DENOM_ROWS = 16
ATTN_UNROLL = 8
PAGES_PER_STEP = 16
DECODE_ROWS = 4


def _cparams(*sem):
    return pltpu.CompilerParams(dimension_semantics=sem, vmem_limit_bytes=VMEM_LIMIT_BYTES)


def _dg(a, b, dims):
    return lax.dot_general(a, b, (dims, ((), ())), preferred_element_type=F32)


NN = ((1,), (0,))
NT = ((1,), (1,))
TN = ((0,), (0,))


def _split2(x):
    h = x.astype(BF16)
    m = (x - h.astype(F32)).astype(BF16)
    return h, m


def _split3(x):
    h = x.astype(BF16)
    r = x - h.astype(F32)
    m = r.astype(BF16)
    l = (r - m.astype(F32)).astype(BF16)
    return h, m, l


def _dot_x3(a, b, dims=NN):
    ah, am = _split2(a)
    bh, bm = _split2(b)
    return _dg(ah, bh, dims) + _dg(ah, bm, dims) + _dg(am, bh, dims)


def _sigmoid(x):
    return 1.0 / (1.0 + jnp.exp(-x))


def _silu(x):
    return x * _sigmoid(x)


def _mod_kernel(c_ref, w_ref, b_ref, o_ref):
    o_ref[...] = jnp.dot(c_ref[...].astype(BF16), w_ref[...].astype(BF16),
                         preferred_element_type=F32) + b_ref[...]


def _ada_mod(c_all, ada_w, ada_b3, l):
    R = c_all.shape[0]
    TN_ = 1536
    n = 6 * D_MODEL
    return pl.pallas_call(
        _mod_kernel, grid=(n // TN_,),
        in_specs=[BS((R, D_MODEL), lambda j: (0, 0)),
                  BS((None, D_MODEL, TN_), lambda j: (l, 0, j)),
                  BS((None, 1, TN_), lambda j: (l, 0, j))],
        out_specs=BS((R, TN_), lambda j: (0, j)),
        out_shape=SDS((R, n), F32),
        compiler_params=_cparams("arbitrary"))(c_all, ada_w, ada_b3)


def _modulated_norm(x, nw, sc, sh):
    ms = jnp.mean(x * x, axis=-1, keepdims=True)
    return (x * lax.rsqrt(ms + EPS) * nw) * (1.0 + sc) + sh


def _inproj_kernel(x_ref, nw_ref, sc_ref, sh_ref, w_ref, wg_ref, o_ref, og_ref, h_ref):
    @pl.when(pl.program_id(1) == 0)
    def _():
        h = _modulated_norm(x_ref[...], nw_ref[...], sc_ref[...], sh_ref[...]).astype(BF16)
        h_ref[...] = h
        og_ref[...] = jnp.dot(h, wg_ref[...], preferred_element_type=F32)

    o_ref[...] = jnp.dot(h_ref[...], w_ref[...], preferred_element_type=F32)


def _mod_spec(mod_rows, tm, ngrid):
    if mod_rows == 1:
        return BS((1, D_MODEL), (lambda i, j: (0, 0)) if ngrid == 2 else (lambda i: (0, 0)))
    return BS((tm, D_MODEL), (lambda i, j: (i, 0)) if ngrid == 2 else (lambda i: (i, 0)))


def _inproj(x, nw3, sc, sh, w_main, w_glr, l, tm):
    M = x.shape[0]
    TN_ = 1536
    ms = _mod_spec(sc.shape[0], tm, 2)
    return pl.pallas_call(
        _inproj_kernel, grid=(M // tm, N_MAIN // TN_),
        in_specs=[BS((tm, D_MODEL), lambda i, j: (i, 0)),
                  BS((None, 1, D_MODEL), lambda i, j: (l, 0, 0)),
                  ms, ms,
                  BS((None, D_MODEL, TN_), lambda i, j: (l, 0, j)),
                  BS((None, D_MODEL, 128), lambda i, j: (l, 0, 0))],
        out_specs=[BS((tm, TN_), lambda i, j: (i, j)),
                   BS((tm, 128), lambda i, j: (i, 0))],
        out_shape=[SDS((M, N_MAIN), F32), SDS((M, 128), F32)],
        scratch_shapes=[pltpu.VMEM((tm, D_MODEL), BF16)],
        compiler_params=_cparams("arbitrary", "arbitrary"))(x, nw3, sc, sh, w_main, w_glr)


def _load_rows(ref, pad_ref, rows, C):
    if rows == C:
        return ref[0]
    pad_ref[...] = jnp.zeros_like(pad_ref)
    pad_ref[0:rows, :] = ref[0]
    return pad_ref[...]


def _state_init(S_ref, s0_ref, nh, dk, dv):
    S_ref[...] = jnp.zeros_like(S_ref)
    for h in range(nh):
        S_ref[dk * h:dk * (h + 1), dv * h:dv * (h + 1)] = s0_ref[0, h]


def _state_out(so_ref, S_ref, nh, dk, dv):
    for h in range(nh):
        so_ref[0, h] = S_ref[dk * h:dk * (h + 1), dv * h:dv * (h + 1)]


def _block_diag_mask():
    r = lax.broadcasted_iota(jnp.int32, (QK_W, V_W), 0) // GLA_DK
    c = lax.broadcasted_iota(jnp.int32, (QK_W, V_W), 1) // GLA_DV
    return r == c


def _gla_kernel(q_ref, k_ref, v_ref, r_ref, g_ref, w2_ref, b_ref, nw_ref, s0_ref,
                o_ref, so_ref, S_ref, *pads, C, rows, nc):
    c = pl.program_id(1)

    @pl.when(c == 0)
    def _():
        _state_init(S_ref, s0_ref, GLA_HEADS, GLA_DK, GLA_DV)

    pads = list(pads) + [None] * 5
    q = _load_rows(q_ref, pads[0], rows, C)
    k = _load_rows(k_ref, pads[1], rows, C)
    v = _load_rows(v_ref, pads[2], rows, C)
    r = _load_rows(r_ref, pads[3], rows, C)
    g = _load_rows(g_ref, pads[4], rows, C)

    row = lax.broadcasted_iota(jnp.int32, (C, QK_W), 0)
    x = jnp.dot(g.astype(BF16), w2_ref[...].astype(BF16), preferred_element_type=F32) + b_ref[...]
    log_a = (jnp.minimum(x, 0.0) - jnp.log(1.0 + jnp.exp(-jnp.abs(x)))) * (1.0 / GLA_GATE_TEMP)
    if rows < C:
        log_a = jnp.where(row < rows, log_a, 0.0)

    ri = lax.broadcasted_iota(jnp.int32, (C, C), 0)
    ci = lax.broadcasted_iota(jnp.int32, (C, C), 1)
    causal = ri >= ci
    ltri = jnp.where(causal, 1.0, 0.0).astype(BF16)
    ones_c = jnp.ones((C, 128), BF16)
    pieces = _split3(log_a)
    b = sum(_dg(ltri, p, NN) for p in pieces)
    b_last_col = sum(_dg(p, ones_c, TN) for p in pieces)
    mid = max(min(rows, C) // 2, 1)
    b_ref_row = b[mid - 1:mid, :]
    b_last = b[C - 1:C, :]

    qs = q * (GLA_DK ** -0.5)
    qt_h, qt_m = _split2(qs * jnp.exp(b - b_ref_row))
    kt_h, kt_m = _split2(k * jnp.exp(b_ref_row - b))
    q_state = (qs * jnp.exp(b)).astype(BF16)
    k_state = (k * jnp.exp(b_last - b)).astype(BF16)
    vb = v.astype(BF16)

    S = S_ref[...]
    o_state = _dg(q_state, S.astype(BF16), NN)
    lane_head = lax.broadcasted_iota(jnp.int32, (C, QK_W), 1) // GLA_DK
    nw = nw_ref[...]
    for h in range(GLA_HEADS):
        mine = lane_head == h
        qh_h = jnp.where(mine, qt_h, jnp.zeros_like(qt_h))
        qh_m = jnp.where(mine, qt_m, jnp.zeros_like(qt_m))
        a = _dg(qh_h, kt_h, NT) + _dg(qh_h, kt_m, NT) + _dg(qh_m, kt_h, NT)
        a = jnp.where(causal, a, 0.0).astype(BF16)
        sl = slice(GLA_DV * h, GLA_DV * (h + 1))
        oh = _dg(a, vb[:, sl], NN) + o_state[:, sl]
        ms = jnp.mean(oh * oh, axis=-1, keepdims=True)
        y = oh * lax.rsqrt(ms + EPS) * nw[:, sl] * _silu(r[:, sl])
        o_ref[0, :, sl] = y[0:rows].astype(o_ref.dtype)

    u = _dg(k_state, vb, TN)
    e_col = jnp.exp(b_last_col)
    e_full = jnp.concatenate([e_col] * (V_W // 128), axis=1)
    S_ref[...] = e_full * S + jnp.where(_block_diag_mask(), u, 0.0)

    @pl.when(c == nc - 1)
    def _():
        _state_out(so_ref, S_ref, GLA_HEADS, GLA_DK, GLA_DV)


def _mixer_specs(rows):
    def col(width, idx):
        return BS((1, rows, width), lambda b, c: (b, c, idx))
    return col


def _gla(P3, G3, w2p, bias3, nw3, s0, l, C, rows):
    B, T, _ = P3.shape
    nc = T // rows
    col = _mixer_specs(rows)
    kern = functools.partial(_gla_kernel, C=C, rows=rows, nc=nc)
    scratch = [pltpu.VMEM((QK_W, V_W), F32)]
    if rows < C:
        scratch += [pltpu.VMEM((C, w), F32) for w in (QK_W, QK_W, V_W, V_W, 128)]
    return pl.pallas_call(
        kern, grid=(B, nc),
        in_specs=[col(QK_W, COL_GQ), col(QK_W, COL_GK), col(V_W, COL_GV), col(V_W, COL_GR),
                  BS((1, rows, 128), lambda b, c: (b, c, 0)),
                  BS((None, 128, QK_W), lambda b, c: (l, 0, 0)),
                  BS((None, 1, QK_W), lambda b, c: (l, 0, 0)),
                  BS((None, 1, V_W), lambda b, c: (l, 0, 0)),
                  BS((1, GLA_HEADS, GLA_DK, GLA_DV), lambda b, c: (b, 0, 0, 0))],
        out_specs=[BS((1, rows, V_W), lambda b, c: (b, c, 0)),
                   BS((1, GLA_HEADS, GLA_DK, GLA_DV), lambda b, c: (b, 0, 0, 0))],
        out_shape=[SDS((B, T, V_W), BF16), SDS((B, GLA_HEADS, GLA_DK, GLA_DV), F32)],
        scratch_shapes=scratch,
        compiler_params=_cparams("arbitrary", "arbitrary"))(P3, P3, P3, P3, G3, w2p, bias3, nw3, s0)


def _ret_kernel(q_ref, k_ref, v_ref, g_ref, cos_ref, sin_ref, dm_ref, rs_ref, ks_ref, cd_ref,
                nw_ref, s0_ref, o_ref, so_ref, S_ref, *pads, C, rows, nc):
    c = pl.program_id(1)

    @pl.when(c == 0)
    def _():
        _state_init(S_ref, s0_ref, RET_HEADS, RET_DK, RET_DV)

    pads = list(pads) + [None] * 4
    q = _load_rows(q_ref, pads[0], rows, C)
    k = _load_rows(k_ref, pads[1], rows, C)
    v = _load_rows(v_ref, pads[2], rows, C)
    g = _load_rows(g_ref, pads[3], rows, C)

    cos = cos_ref[...]
    sin = sin_ref[...]
    first_half = (lax.broadcasted_iota(jnp.int32, (C, QK_W), 1) % RET_DK) < (RET_DK // 2)

    def rope(x):
        partner = jnp.where(first_half, pltpu.roll(x, QK_W - RET_DK // 2, 1),
                            pltpu.roll(x, RET_DK // 2, 1))
        return x * cos + partner * sin

    qr = rope(q)
    kr = rope(k) * (RET_DK ** -0.5)
    qb = qr.astype(BF16)
    kb = kr.astype(BF16)
    vb = v.astype(BF16)
    k_state = (kr * ks_ref[...]).astype(BF16)

    S = S_ref[...]
    o_state = _dg(qb, S.astype(BF16), NN) * rs_ref[...]
    lane_head = lax.broadcasted_iota(jnp.int32, (C, QK_W), 1) // RET_DK
    nw = nw_ref[...]
    for h in range(RET_HEADS):
        a = _dg(jnp.where(lane_head == h, qb, jnp.zeros_like(qb)), kb, NT) * dm_ref[h]
        sl = slice(RET_DV * h, RET_DV * (h + 1))
        oh = _dg(a.astype(BF16), vb[:, sl], NN) + o_state[:, sl]
        oh = oh - jnp.mean(oh, axis=-1, keepdims=True)
        var = jnp.mean(oh * oh, axis=-1, keepdims=True)
        y = oh * lax.rsqrt(var + EPS) * nw[:, sl] * _silu(g[:, sl])
        o_ref[0, :, sl] = y[0:rows].astype(o_ref.dtype)

    u = _dg(k_state, vb, TN)
    S_ref[...] = cd_ref[...] * S + jnp.where(_block_diag_mask(), u, 0.0)

    @pl.when(c == nc - 1)
    def _():
        _state_out(so_ref, S_ref, RET_HEADS, RET_DK, RET_DV)


def _ret_constants(C, rows):
    log_gamma = np.log(1.0 - 2.0 ** (-5.0 - np.arange(RET_HEADS, dtype=np.float64)))
    bt = np.minimum(np.arange(C) + 1, rows).astype(np.float64)
    diff = bt[:, None] - bt[None, :]
    causal = np.arange(C)[:, None] >= np.arange(C)[None, :]
    dm = np.where(causal[None], np.exp(diff[None] * log_gamma[:, None, None]), 0.0)
    rs = np.exp(bt[:, None] * np.repeat(log_gamma, RET_DV)[None, :])
    ks = np.exp((bt[-1] - bt)[:, None] * np.repeat(log_gamma, RET_DK)[None, :])
    cd = np.exp(bt[-1] * np.repeat(log_gamma, RET_DV))[None, :]
    return (jnp.asarray(dm, F32), jnp.asarray(rs, F32), jnp.asarray(ks, F32), jnp.asarray(cd, F32))


def _ret(P3, cos_t, sin_t, nw3, s0, l, C, rows):
    B, T, _ = P3.shape
    nc = T // rows
    col = _mixer_specs(rows)
    dm, rs, ks, cd = _ret_constants(C, rows)
    kern = functools.partial(_ret_kernel, C=C, rows=rows, nc=nc)
    scratch = [pltpu.VMEM((QK_W, V_W), F32)]
    if rows < C:
        scratch += [pltpu.VMEM((C, w), F32) for w in (QK_W, QK_W, V_W, V_W)]
    const = lambda shape: BS(shape, lambda b, c: (0,) * len(shape))
    return pl.pallas_call(
        kern, grid=(B, nc),
        in_specs=[col(QK_W, COL_RQ), col(QK_W, COL_RK), col(V_W, COL_RV), col(V_W, COL_RG),
                  BS((C, QK_W), lambda b, c: (c, 0)), BS((C, QK_W), lambda b, c: (c, 0)),
                  const((RET_HEADS, C, C)), const((C, V_W)), const((C, QK_W)), const((1, V_W)),
                  BS((None, 1, V_W), lambda b, c: (l, 0, 0)),
                  BS((1, RET_HEADS, RET_DK, RET_DV), lambda b, c: (b, 0, 0, 0))],
        out_specs=[BS((1, rows, V_W), lambda b, c: (b, c, 0)),
                   BS((1, RET_HEADS, RET_DK, RET_DV), lambda b, c: (b, 0, 0, 0))],
        out_shape=[SDS((B, T, V_W), BF16), SDS((B, RET_HEADS, RET_DK, RET_DV), F32)],
        scratch_shapes=scratch,
        compiler_params=_cparams("arbitrary", "arbitrary"))(
            P3, P3, P3, P3, cos_t, sin_t, dm, rs, ks, cd, nw3, s0)


def _rope_tables(pos0, n):
    half = RET_DK // 2
    inv = ROPE_BASE ** (-jnp.arange(half, dtype=F32) / half)
    pos = (pos0 + jnp.arange(n, dtype=jnp.int32)).astype(F32)
    ang = pos[:, None] * inv[None, :]
    cos, sin = jnp.cos(ang), jnp.sin(ang)
    cos_t = jnp.tile(jnp.concatenate([cos, cos], -1), (1, RET_HEADS))
    sin_t = jnp.tile(jnp.concatenate([-sin, sin], -1), (1, RET_HEADS))
    return cos_t, sin_t


def _head_rms64(x, g_ref, w):
    xh, xm = _split2(x * x)
    ms = jnp.dot(xh, g_ref[...], preferred_element_type=F32) + jnp.dot(xm, g_ref[...], preferred_element_type=F32)
    return x * lax.rsqrt(ms + EPS) * w


def _qknorm_kernel(q_ref, k_ref, g_ref, qw_ref, kw_ref, qn_ref, kn_ref):
    qn_ref[...] = _head_rms64(q_ref[...], g_ref, qw_ref[...])
    kn_ref[...] = _head_rms64(k_ref[...], g_ref, kw_ref[...])


def _prompt_prep_kernel(q_ref, k_ref, v_ref, g_ref, pl_ref, eye_ref, qw_ref, kw_ref,
                        qn_ref, kn_ref, km_ref, kp_ref, vt_ref):
    i = pl.program_id(0)
    qn_ref[...] = _head_rms64(q_ref[...], g_ref, qw_ref[...])
    kn = _head_rms64(k_ref[...], g_ref, kw_ref[...])
    kn_ref[...] = kn
    km_ref[0] = jnp.mean(kn, axis=0, keepdims=True)
    lane = lax.broadcasted_iota(jnp.int32, (MOBA_BLOCK, 2 * MOBA_W), 1) % 128
    placed = jnp.dot(kn.astype(BF16), pl_ref[...], preferred_element_type=F32)
    kp_ref[...] = (placed + jnp.where(lane == MOBA_HD + i, BIG, 0.0)).astype(BF16)
    vt_ref[...] = _dg(eye_ref[...], v_ref[...].astype(BF16), NT).astype(BF16)


def _group_mean_matrix():
    r = np.arange(MOBA_W)
    return jnp.asarray((r[:, None] // MOBA_HD == r[None, :] // MOBA_HD) / MOBA_HD, BF16)


def _placement_matrix():
    r = np.arange(MOBA_W)
    c = np.arange(2 * MOBA_W)
    tgt = (r // MOBA_HD) * 128 + r % MOBA_HD
    return jnp.asarray(tgt[:, None] == c[None, :], BF16)


def _qknorm(P, qw3, kw3, l, tm):
    M = P.shape[0]
    return pl.pallas_call(
        _qknorm_kernel, grid=(M // tm,),
        in_specs=[BS((tm, MOBA_W), lambda i: (i, COL_MQ)), BS((tm, MOBA_W), lambda i: (i, COL_MK)),
                  BS((MOBA_W, MOBA_W), lambda i: (0, 0)),
                  BS((None, 1, MOBA_W), lambda i: (l, 0, 0)), BS((None, 1, MOBA_W), lambda i: (l, 0, 0))],
        out_specs=[BS((tm, MOBA_W), lambda i: (i, 0)), BS((tm, MOBA_W), lambda i: (i, 0))],
        out_shape=[SDS((M, MOBA_W), F32), SDS((M, MOBA_W), F32)],
        compiler_params=_cparams("arbitrary"))(P, P, _group_mean_matrix(), qw3, kw3)


def _prompt_prep(P, qw3, kw3, l):
    T = P.shape[0]
    nb = T // MOBA_BLOCK
    tm = MOBA_BLOCK
    row = lambda w, idx: BS((tm, w), lambda i: (i, idx))
    return pl.pallas_call(
        _prompt_prep_kernel, grid=(nb,),
        in_specs=[row(MOBA_W, COL_MQ), row(MOBA_W, COL_MK), row(MOBA_W, COL_MV),
                  BS((MOBA_W, MOBA_W), lambda i: (0, 0)),
                  BS((MOBA_W, 2 * MOBA_W), lambda i: (0, 0)),
                  BS((MOBA_W, MOBA_W), lambda i: (0, 0)),
                  BS((None, 1, MOBA_W), lambda i: (l, 0, 0)), BS((None, 1, MOBA_W), lambda i: (l, 0, 0))],
        out_specs=[row(MOBA_W, 0), row(MOBA_W, 0), BS((1, 1, MOBA_W), lambda i: (i, 0, 0)),
                   row(2 * MOBA_W, 0), BS((MOBA_W, tm), lambda i: (0, i))],
        out_shape=[SDS((T, MOBA_W), F32), SDS((T, MOBA_W), F32), SDS((nb, 1, MOBA_W), F32),
                   SDS((T, 2 * MOBA_W), BF16), SDS((MOBA_W, T), BF16)],
        compiler_params=_cparams("arbitrary"))(
            P, P, P, _group_mean_matrix(), _placement_matrix(), jnp.eye(MOBA_W, dtype=BF16), qw3, kw3)


def _placed_block_means(kmean):
    nb = kmean.shape[0]
    km = kmean.reshape(nb, MOBA_HEADS, MOBA_HD).transpose(1, 2, 0)
    km = jnp.pad(km, ((0, 0), (0, 0), (MOBA_HD, MOBA_HD - nb)))
    full = km[:, :, None, :] * jnp.eye(MOBA_HEADS, dtype=F32)[:, None, :, None]
    return full.reshape(MOBA_W, 2 * MOBA_W)


def _select_top3(gate, valid, axis=-1):
    g1 = jnp.where(valid, gate, NEG)
    m1 = jnp.max(g1, axis=axis, keepdims=True)
    g2 = jnp.where(g1 >= m1, NEG, g1)
    m2 = jnp.max(g2, axis=axis, keepdims=True)
    g3 = jnp.where(g2 >= m2, NEG, g2)
    m3 = jnp.max(g3, axis=axis, keepdims=True)
    return valid & (g1 >= m3)


def _gate_kernel(qn_ref, kmt_ref, plt_ref, qpt_ref):
    i = pl.program_id(0)
    qn = qn_ref[...]
    gate = _dg(kmt_ref[...].astype(BF16), qn.astype(BF16), NT)
    qpart = _dg(plt_ref[...], (qn * (MOBA_HD ** -0.5 * LOG2E)).astype(BF16), NT)
    blk = lax.broadcasted_iota(jnp.int32, (128, MOBA_BLOCK), 0) - MOBA_HD
    valid = (blk >= 0) & (blk < i)
    for h in range(MOBA_HEADS):
        sl = slice(128 * h, 128 * (h + 1))
        sel = _select_top3(gate[sl], valid, axis=0) | (blk == i)
        maskpart = jnp.where((blk >= 0) & jnp.logical_not(sel), -1.0, 0.0)
        qpt_ref[sl, :] = (qpart[sl] + maskpart).astype(BF16)


def _gate(qn, km_placed_t):
    T = qn.shape[0]
    nb = T // MOBA_BLOCK
    return pl.pallas_call(
        _gate_kernel, name="moba_gate", grid=(nb,),
        in_specs=[BS((MOBA_BLOCK, MOBA_W), lambda i: (i, 0)),
                  BS((2 * MOBA_W, MOBA_W), lambda i: (0, 0)),
                  BS((2 * MOBA_W, MOBA_W), lambda i: (0, 0))],
        out_specs=BS((2 * MOBA_W, MOBA_BLOCK), lambda i: (0, i)),
        out_shape=SDS((2 * MOBA_W, T), BF16),
        compiler_params=_cparams("arbitrary"))(qn, km_placed_t, _placement_matrix().T)


def _bucket_np(dist):
    dist = np.maximum(dist, 0)
    max_exact = N_BUCKETS // 2
    large = max_exact + (np.log(np.maximum(dist, max_exact).astype(np.float64) / max_exact)
                         / math.log(MAX_DISTANCE / max_exact) * (N_BUCKETS - max_exact)).astype(np.int64)
    large = np.minimum(large, N_BUCKETS - 1)
    return np.where(dist < max_exact, dist, large).astype(np.int32)


def _bias_kernel(rb_ref, idx_ref, ok_ref, o_ref, *, scale):
    h = pl.program_id(0)
    idx = idx_ref[...]
    far = rb_ref[N_BUCKETS - 1, h]
    acc = jnp.zeros(idx.shape, F32)
    for b in range(N_BUCKETS - 1):
        acc = jnp.where(idx == b, (rb_ref[b, h] - far) * scale, acc)
    o_ref[0] = jnp.where(ok_ref[...] > 0, acc, -BIG)


def _bias_tables(rel_bias, dist, scale=1.0):
    idx = jnp.asarray(_bucket_np(dist))
    ok = jnp.asarray((dist >= 0).astype(np.int32))
    shp = dist.shape
    zeros = (0,) * len(shp)
    return pl.pallas_call(
        functools.partial(_bias_kernel, scale=scale), name="moba_bias_tables", grid=(MOBA_HEADS,),
        in_specs=[BS(memory_space=pltpu.SMEM), BS(shp, lambda h: zeros), BS(shp, lambda h: zeros)],
        out_specs=BS((1,) + shp, lambda h: (h,) + zeros),
        out_shape=SDS((MOBA_HEADS,) + shp, F32),
        compiler_params=_cparams("arbitrary"))(rel_bias, idx, ok)


def _attn_kernel(qpt_ref, kp_ref, vt_ref, bt_ref, eye_ref, o_ref, m_ref, acc_ref, s_ref):
    i = pl.program_id(1)
    tq = MOBA_BLOCK
    m_ref[...] = jnp.full(m_ref.shape, NEG, F32)
    acc_ref[...] = jnp.zeros_like(acc_ref)

    def logits(slot, hh, n, t):
        qt = qpt_ref[128 * hh:128 * (hh + 1), :]
        start = pl.multiple_of(n * MOBA_BLOCK, MOBA_BLOCK)
        kblk = kp_ref[pl.ds(start, MOBA_BLOCK), 128 * hh:128 * (hh + 1)]
        s = jnp.dot(kblk, qt, preferred_element_type=F32)
        if t is not None:
            s = s + bt_ref[hh, t]
        s_ref[slot] = s

    def softmax_pv(slot, hh, n):
        ps, alphas = [], []
        for c in range(tq // ATTN_COLS):
            cols = slice(ATTN_COLS * c, ATTN_COLS * (c + 1))
            sc = s_ref[slot, :, cols]
            m_old = m_ref[hh, :, cols]
            m_new = jnp.maximum(m_old, jnp.max(sc, axis=0, keepdims=True))
            alpha = jnp.exp2(m_old - m_new)
            p = jnp.exp2(sc - m_new)
            m_ref[hh, :, cols] = m_new
            ps.append(p.astype(BF16))
            alphas.append(alpha)
        start = pl.multiple_of(n * MOBA_BLOCK, MOBA_BLOCK)
        vt = vt_ref[MOBA_HD * hh:MOBA_HD * (hh + 1), pl.ds(start, MOBA_BLOCK)]
        vt_ones = jnp.concatenate([vt, jnp.ones((ATTN_DENOM_ROWS, MOBA_BLOCK), BF16)], axis=0)
        pv = jnp.dot(vt_ones, jnp.concatenate(ps, axis=1), preferred_element_type=F32)
        acc_ref[hh] = jnp.concatenate(alphas, axis=1) * acc_ref[hh] + pv

    def group(tiles):
        for slot, (hh, n, t) in enumerate(tiles):
            logits(slot, hh, n, t)
        for slot, (hh, n, t) in enumerate(tiles):
            softmax_pv(slot, hh, n)

    n_far = jnp.maximum(i - 1, 0)

    def far_tiles(first, count):
        return [(hh, first + b, None) for b in range(count) for hh in range(2)]

    def far_body(j, carry):
        group(far_tiles(ATTN_UNROLL * j, ATTN_UNROLL))
        return carry

    lax.fori_loop(0, n_far // ATTN_UNROLL, far_body, 0)
    done = (n_far // ATTN_UNROLL) * ATTN_UNROLL
    width = ATTN_UNROLL // 2
    while width >= 1:
        @pl.when((n_far - done) % (2 * width) >= width)
        def _(done=done, width=width):
            group(far_tiles(done, width))
        done = done + jnp.where((n_far - done) % (2 * width) >= width, width, 0)
        width //= 2

    @pl.when(i >= 1)
    def _():
        group([(0, i - 1, 1), (1, i - 1, 1), (0, i, 0), (1, i, 0)])

    @pl.when(i == 0)
    def _():
        group([(0, i, 0), (1, i, 0)])

    out_t = jnp.concatenate([acc_ref[hh, 0:MOBA_HD] / acc_ref[hh, MOBA_HD:MOBA_HD + 1] for hh in range(2)],
                            axis=0).astype(BF16)
    o_ref[...] = _dg(out_t, eye_ref[...], TN).astype(o_ref.dtype)


def _prompt_attention(qpt, kp, vt, btab_t):
    T = kp.shape[0]
    nb = T // MOBA_BLOCK
    return pl.pallas_call(
        _attn_kernel, name="moba_prompt_attention", grid=(MOBA_HEADS // 2, nb),
        in_specs=[BS((256, MOBA_BLOCK), lambda hp, i: (hp, i)),
                  BS((T, 256), lambda hp, i: (0, hp)),
                  BS((128, T), lambda hp, i: (hp, 0)),
                  BS((2, 2, MOBA_BLOCK, MOBA_BLOCK), lambda hp, i: (hp, 0, 0, 0)),
                  BS((128, 128), lambda hp, i: (0, 0))],
        out_specs=BS((MOBA_BLOCK, 128), lambda hp, i: (i, hp)),
        out_shape=SDS((T, MOBA_W), BF16),
        scratch_shapes=[pltpu.VMEM((2, 1, MOBA_BLOCK), F32),
                        pltpu.VMEM((2, MOBA_HD + ATTN_DENOM_ROWS, MOBA_BLOCK), F32),
                        pltpu.VMEM((2 * ATTN_UNROLL, MOBA_BLOCK, MOBA_BLOCK), F32)],
        compiler_params=_cparams("arbitrary", "arbitrary"))(qpt, kp, vt, btab_t, jnp.eye(128, dtype=BF16))


def _decode_gate_kernel(pt_ref, *refs):
    npg = PAGES_PER_STEP
    pages, (q_ref, g_ref) = refs[:npg], refs[npg:]
    qb = _bf16_round(q_ref[0])
    pages_per_block = MOBA_BLOCK // pages[0].shape[-1]
    for blk in range(npg // pages_per_block):
        tokens = pages[pages_per_block * blk][...]
        for g in range(1, pages_per_block):
            tokens = tokens + pages[pages_per_block * blk + g][...]
        km = jnp.sum(tokens, axis=-1, keepdims=True) * (1.0 / MOBA_BLOCK)
        g_ref[0, blk] = jnp.sum(_bf16_round(km) * qb, axis=1)


def _decode_gates(page_table, cache_t, q_t, l):
    nseq, n_pages = page_table.shape
    page = cache_t.shape[-1]
    npg = PAGES_PER_STEP
    ppb = MOBA_BLOCK // page
    page_specs = [BS((None, None, MOBA_HEADS, MOBA_HD, page), functools.partial(
        lambda b, g, pt, r: (l, pt[b, g * npg + r], 0, 0, 0), r=r)) for r in range(npg)]
    grid_spec = pltpu.PrefetchScalarGridSpec(
        num_scalar_prefetch=1, grid=(nseq, n_pages // npg),
        in_specs=page_specs + [BS((1, MOBA_HEADS, MOBA_HD, 128), lambda b, g, pt: (b, 0, 0, 0))],
        out_specs=BS((1, npg // ppb, MOBA_HEADS, 128), lambda b, g, pt: (b, g, 0, 0)))
    return pl.pallas_call(
        _decode_gate_kernel, name="decode_gates", grid_spec=grid_spec,
        out_shape=SDS((nseq, n_pages // ppb, MOBA_HEADS, 128), F32),
        compiler_params=_cparams("arbitrary", "arbitrary"))(page_table, *([cache_t] * npg), q_t)


def _decode_select_kernel(g_ref, o_ref):
    gate = g_ref[0]
    nblk = gate.shape[0]
    blk = lax.broadcasted_iota(jnp.int32, gate.shape, 0)
    for t in range(MOBA_TOPK):
        m = jnp.max(gate, axis=0, keepdims=True)
        idx = jnp.min(jnp.where(gate >= m, blk, nblk), axis=0, keepdims=True)
        o_ref[0, t] = idx[0]
        gate = jnp.where(blk == idx, NEG, gate)


def _decode_select(gates):
    nseq, nblk = gates.shape[:2]
    return pl.pallas_call(
        _decode_select_kernel, name="decode_select", grid=(nseq,),
        in_specs=[BS((1, nblk, MOBA_HEADS, 128), lambda b: (b, 0, 0, 0))],
        out_specs=BS((1, MOBA_TOPK, MOBA_HEADS, 128), lambda b: (b, 0, 0, 0)),
        out_shape=SDS((nseq, MOBA_TOPK, MOBA_HEADS, 128), jnp.int32),
        compiler_params=_cparams("arbitrary"))(gates)


def _bf16_round(x):
    return x.astype(BF16).astype(F32)


def _decode_attend_kernel(pt_ref, sel_ref, ck_ref, cv_ref, q_ref, kn_ref, vn_ref, bpast_ref, bown_ref, o_ref,
                          kbuf, vbuf, sem, *, l, nd, nblk, page):
    b = pl.program_id(0)
    rows = MOBA_HEADS * nd
    trips = rows // DECODE_ROWS
    pages_per_block = MOBA_BLOCK // page

    def copies(i, slot):
        out = []
        for u in range(DECODE_ROWS):
            r = DECODE_ROWS * i + u
            h = r // nd
            for t in range(MOBA_TOPK):
                n = sel_ref[b, MOBA_TOPK * r + t]
                for g in range(pages_per_block):
                    pg = pt_ref[b, pages_per_block * n + g]
                    dst = slice(page * g, page * (g + 1))
                    out.append(pltpu.make_async_copy(ck_ref.at[l, pg, h], kbuf.at[slot, u, t, :, dst],
                                                     sem.at[slot, 0, u, t, g]))
                    out.append(pltpu.make_async_copy(cv_ref.at[l, pg, h], vbuf.at[slot, u, t, :, dst],
                                                     sem.at[slot, 1, u, t, g]))
        return out

    for c in copies(0, 0):
        c.start()

    def body(i, carry):
        slot = i % 2

        @pl.when(i + 1 < trips)
        def _():
            for c in copies(i + 1, 1 - slot):
                c.start()

        for c in copies(i, slot):
            c.wait()

        steps = [attend_row(DECODE_ROWS * i + u, kbuf.at[slot, u], vbuf.at[slot, u]) for u in range(DECODE_ROWS)]
        for _ in range(3):
            for st in steps:
                next(st, None)
        return carry

    def attend_row(r, kbuf, vbuf):
        h = r // nd
        qv = q_ref[0, r]
        qmat = jnp.broadcast_to(qv, (8, MOBA_HD)).astype(BF16)
        logits = []
        for t in range(MOBA_TOPK):
            s = jnp.dot(qmat, kbuf[t].astype(BF16), preferred_element_type=F32)
            is_last = sel_ref[b, MOBA_TOPK * r + t] == nblk - 1
            logits.append(s + jnp.where(is_last, bpast_ref[pl.ds(r, 1), :], 0.0))
        yield
        s_own = jnp.sum(_bf16_round(kn_ref[0, h]) * _bf16_round(qv), axis=-1, keepdims=True) + bown_ref[r]
        m = jnp.max(s_own, axis=0, keepdims=True)
        for s in logits:
            m = jnp.maximum(m, jnp.max(s[0:1], axis=-1, keepdims=True))
        p_own = jnp.exp(s_own - m)
        denom = jnp.sum(p_own, axis=0, keepdims=True)
        probs = [jnp.exp(s - m) for s in logits]
        for p in probs:
            denom = denom + jnp.sum(p[0:1], axis=-1, keepdims=True)
        inv = 1.0 / denom
        yield
        acc = jnp.zeros((8, MOBA_HD), F32)
        for t in range(MOBA_TOPK):
            acc = acc + _dg((probs[t] * inv).astype(BF16), vbuf[t].astype(BF16), NT)
        own = jnp.sum(_bf16_round(p_own * inv) * _bf16_round(vn_ref[0, h]), axis=0, keepdims=True)
        o_ref[0, r] = acc[0:1] + own

    lax.fori_loop(0, trips, body, 0)


def _decode_attend(page_table, sel, cache_k, cache_v, q_rows, kn_new, vn_new, b_past, b_own, l, nd):
    nseq, n_pages = page_table.shape
    page = cache_k.shape[-1]
    rows = q_rows.shape[1]
    per_seq = lambda shape: BS((1,) + shape, lambda b, pt, sl: (b,) + (0,) * len(shape))
    const = lambda shape: BS(shape, lambda b, pt, sl: (0,) * len(shape))
    assert rows % DECODE_ROWS == 0
    slab = pltpu.VMEM((2, DECODE_ROWS, MOBA_TOPK, MOBA_HD, MOBA_BLOCK), F32)
    grid_spec = pltpu.PrefetchScalarGridSpec(
        num_scalar_prefetch=2, grid=(nseq,),
        in_specs=[BS(memory_space=pl.ANY), BS(memory_space=pl.ANY),
                  per_seq((rows, 1, MOBA_HD)), per_seq((MOBA_HEADS, 8, MOBA_HD)), per_seq((MOBA_HEADS, 8, MOBA_HD)),
                  const((rows, MOBA_BLOCK)), const((rows, 8, 1))],
        out_specs=per_seq((rows, 1, MOBA_HD)),
        scratch_shapes=[slab, slab, pltpu.SemaphoreType.DMA((2, 2, DECODE_ROWS, MOBA_TOPK, MOBA_BLOCK // page))])
    kern = functools.partial(_decode_attend_kernel, l=l, nd=nd, nblk=n_pages * page // MOBA_BLOCK, page=page)
    return pl.pallas_call(
        kern, name="decode_attend", grid_spec=grid_spec,
        out_shape=SDS((nseq, rows, 1, MOBA_HD), F32),
        compiler_params=_cparams("arbitrary"))(page_table, sel, cache_k, cache_v, q_rows, kn_new, vn_new, b_past, b_own)


def _merge_kernel(oa_ref, ob_ref, oc_ref, mg_ref, x_ref, g1_ref, wa_ref, wb_ref, wc_ref, wo_ref, o_ref):
    mg = mg_ref[...]
    ya = jnp.dot(oa_ref[...], wa_ref[...], preferred_element_type=F32)
    yb = jnp.dot(ob_ref[...], wb_ref[...], preferred_element_type=F32)
    yc = jnp.dot(oc_ref[...], wc_ref[...], preferred_element_type=F32)
    merged = (_sigmoid(mg[:, 0:D_MODEL]) * ya + _sigmoid(mg[:, D_MODEL:2 * D_MODEL]) * yb
              + _sigmoid(mg[:, 2 * D_MODEL:]) * yc)
    mix = jnp.dot(merged.astype(BF16), wo_ref[...], preferred_element_type=F32)
    o_ref[...] = x_ref[...] + g1_ref[...] * mix


def _merge(oa, ob, oc, P, x, g1, wa, wb, wc, wo, l, tm):
    M = x.shape[0]
    row = lambda w: BS((tm, w), lambda i: (i, 0))
    wspec = lambda k: BS((None, k, D_MODEL), lambda i: (l, 0, 0))
    return pl.pallas_call(
        _merge_kernel, grid=(M // tm,),
        in_specs=[row(V_W), row(V_W), row(MOBA_W), row(3 * D_MODEL), row(D_MODEL),
                  _mod_spec(g1.shape[0], tm, 1),
                  wspec(V_W), wspec(V_W), wspec(MOBA_W), wspec(D_MODEL)],
        out_specs=row(D_MODEL),
        out_shape=SDS((M, D_MODEL), F32),
        compiler_params=_cparams("arbitrary"))(oa, ob, oc, P, x, g1, wa, wb, wc, wo)


def _ffn_kernel(x_ref, nw_ref, sc_ref, sh_ref, g2_ref, w1_ref, w3_ref, w2_ref, o_ref, h_ref, acc_ref):
    j = pl.program_id(1)

    @pl.when(j == 0)
    def _():
        h_ref[...] = _modulated_norm(x_ref[...], nw_ref[...], sc_ref[...], sh_ref[...]).astype(BF16)
        acc_ref[...] = jnp.zeros_like(acc_ref)

    h = h_ref[...]
    a = jnp.dot(h, w1_ref[...], preferred_element_type=F32)
    b = jnp.dot(h, w3_ref[...], preferred_element_type=F32)
    acc_ref[...] += jnp.dot((_silu(a) * b).astype(BF16), w2_ref[...], preferred_element_type=F32)

    @pl.when(j == pl.num_programs(1) - 1)
    def _():
        o_ref[...] = x_ref[...] + g2_ref[...] * acc_ref[...]


def _ffn(x, nw3, sc, sh, g2, w1, w3, w2, l, li, tm):
    M = x.shape[0]
    tf = D_FF // 2
    ms = _mod_spec(sc.shape[0], tm, 2)
    return pl.pallas_call(
        _ffn_kernel, grid=(M // tm, D_FF // tf),
        in_specs=[BS((tm, D_MODEL), lambda i, j: (i, 0)),
                  BS((None, 1, D_MODEL), lambda i, j: (l, 0, 0)), ms, ms, ms,
                  BS((None, D_MODEL, tf), lambda i, j: (li, 0, j)),
                  BS((None, D_MODEL, tf), lambda i, j: (li, 0, j)),
                  BS((None, tf, D_MODEL), lambda i, j: (li, j, 0))],
        out_specs=BS((tm, D_MODEL), lambda i, j: (i, 0)),
        out_shape=SDS((M, D_MODEL), F32),
        scratch_shapes=[pltpu.VMEM((tm, D_MODEL), BF16), pltpu.VMEM((tm, D_MODEL), F32)],
        compiler_params=_cparams("arbitrary", "arbitrary"))(x, nw3, sc, sh, g2, w1, w3, w2)


def _moe_kernel(x_ref, nw_ref, sc_ref, sh_ref, g2_ref, wr_ref, w1_ref, w3_ref, w2_ref, o_ref,
                h_ref, gate_ref, acc_ref):
    e = pl.program_id(1)
    tm = x_ref.shape[0]
    lane = lax.broadcasted_iota(jnp.int32, (tm, 128), 1)

    @pl.when(e == 0)
    def _():
        h = _modulated_norm(x_ref[...], nw_ref[...], sc_ref[...], sh_ref[...])
        h_ref[...] = h.astype(BF16)
        router = jnp.dot(h.astype(BF16), wr_ref[...].astype(BF16), preferred_element_type=F32)
        logits = jnp.where(lane < N_EXPERTS, router, NEG)
        m1 = jnp.max(logits, axis=-1, keepdims=True)
        i1 = jnp.min(jnp.where(logits >= m1, lane, 128), axis=-1, keepdims=True)
        rest = jnp.where(lane == i1, NEG, logits)
        m2 = jnp.max(rest, axis=-1, keepdims=True)
        i2 = jnp.min(jnp.where(rest >= m2, lane, 128), axis=-1, keepdims=True)
        e2 = jnp.exp(m2 - m1)
        w_first = 1.0 / (1.0 + e2)
        gate_ref[...] = jnp.where(lane == i1, w_first, 0.0) + jnp.where(lane == i2, e2 * w_first, 0.0)
        acc_ref[...] = jnp.zeros_like(acc_ref)

    h = h_ref[...]
    a = jnp.dot(h, w1_ref[...], preferred_element_type=F32)
    b = jnp.dot(h, w3_ref[...], preferred_element_type=F32)
    y = jnp.dot((_silu(a) * b).astype(BF16), w2_ref[...], preferred_element_type=F32)
    ge = jnp.sum(jnp.where(lane == e, gate_ref[...], 0.0), axis=-1, keepdims=True)
    acc_ref[...] += ge * y

    @pl.when(e == pl.num_programs(1) - 1)
    def _():
        o_ref[...] = x_ref[...] + g2_ref[...] * acc_ref[...]


def _moe(x, nw3, sc, sh, g2, wr, w1, w3, w2, l, li, tm):
    M = x.shape[0]
    ms = _mod_spec(sc.shape[0], tm, 2)
    return pl.pallas_call(
        _moe_kernel, grid=(M // tm, N_EXPERTS),
        in_specs=[BS((tm, D_MODEL), lambda i, e: (i, 0)),
                  BS((None, 1, D_MODEL), lambda i, e: (l, 0, 0)), ms, ms, ms,
                  BS((None, D_MODEL, 128), lambda i, e: (li, 0, 0)),
                  BS((None, None, D_MODEL, D_FF_EXPERT), lambda i, e: (li, e, 0, 0)),
                  BS((None, None, D_MODEL, D_FF_EXPERT), lambda i, e: (li, e, 0, 0)),
                  BS((None, None, D_FF_EXPERT, D_MODEL), lambda i, e: (li, e, 0, 0))],
        out_specs=BS((tm, D_MODEL), lambda i, e: (i, 0)),
        out_shape=SDS((M, D_MODEL), F32),
        scratch_shapes=[pltpu.VMEM((tm, D_MODEL), BF16), pltpu.VMEM((tm, 128), F32),
                        pltpu.VMEM((tm, D_MODEL), F32)],
        compiler_params=_cparams("arbitrary", "arbitrary"))(x, nw3, sc, sh, g2, wr, w1, w3, w2)


def kernel(x_prompt, x_sample, cache_k, cache_v, state_gla, state_ret, page_table, c_prompt, c_sample,
           ada_w, ada_b, norm1, norm2, w_in, gla_gk_w2, gla_gk_b, gla_norm, ret_norm, moba_qnorm,
           moba_knorm, rel_bias, w_br_gla, w_br_ret, w_br_moba, w_out, ffn_w1, ffn_w3, ffn_w2,
           moe_router, moe_w1, moe_w3, moe_w2):
    depth = w_in.shape[0]
    bp, T, _ = x_prompt.shape
    nseq, nd, _ = x_sample.shape
    n_pages, page = page_table.shape[1], cache_k.shape[2]
    past_len = n_pages * page
    nb = T // MOBA_BLOCK
    assert bp == 1 and T % MOBA_BLOCK == 0 and nb <= MOBA_HD and page == 128
    assert past_len % MOBA_BLOCK == 0 and n_pages % PAGES_PER_STEP == 0 and nd <= 8
    ms_rows = nseq * nd
    nq = nd * MOBA_HEADS

    sizes = np.cumsum([QK_W, QK_W, V_W, GLA_LOWRANK, V_W, QK_W, QK_W, V_W, V_W, MOBA_W, MOBA_W, MOBA_W])
    glr0, glr1, mg0 = int(sizes[2]), int(sizes[3]), int(sizes[-1])
    w_main = jnp.concatenate([w_in[:, :, mg0:], w_in[:, :, :glr0], w_in[:, :, glr1:mg0]], axis=2).astype(BF16)
    w_glr = jnp.pad(w_in[:, :, glr0:glr1], ((0, 0), (0, 0), (0, 128 - GLA_LOWRANK))).astype(BF16)
    w2p = jnp.pad(gla_gk_w2, ((0, 0), (0, 128 - GLA_LOWRANK), (0, 0)))
    gkb3 = gla_gk_b[:, None, :]
    gla_nw3 = jnp.tile(gla_norm, (1, GLA_HEADS))[:, None, :]
    ret_nw3 = jnp.tile(ret_norm, (1, RET_HEADS))[:, None, :]
    qw3 = jnp.tile(moba_qnorm, (1, MOBA_HEADS))[:, None, :]
    kw3 = jnp.tile(moba_knorm, (1, MOBA_HEADS))[:, None, :]
    n1_3, n2_3 = norm1[:, None, :], norm2[:, None, :]
    ada_b3 = ada_b[:, None, :]
    wa, wb, wc, wo = (w.astype(BF16) for w in (w_br_gla, w_br_ret, w_br_moba, w_out))
    f1, f3, f2 = ffn_w1.astype(BF16), ffn_w3.astype(BF16), ffn_w2.astype(BF16)
    e1, e3, e2 = moe_w1.astype(BF16), moe_w3.astype(BF16), moe_w2.astype(BF16)
    wr = jnp.pad(moe_router, ((0, 0), (0, 0), (0, 128 - N_EXPERTS)))

    cos_p, sin_p = _rope_tables(0, T)
    cos_s, sin_s = _rope_tables(past_len, CHUNK)
    tq = np.arange(MOBA_BLOCK)
    d_own = tq[:, None] - tq[None, :]
    btab = _bias_tables(rel_bias, np.stack([d_own.T, d_own.T + MOBA_BLOCK]), LOG2E)
    qi = np.arange(8)
    d_past = (MOBA_BLOCK + qi[:, None] - tq[None, :])
    d_new = np.where((qi[None, :] < nd) & (qi[:, None] < nd), qi[:, None] - qi[None, :], -1)
    bt_past = _bias_tables(rel_bias, d_past)[:, :nd].reshape(nq, MOBA_BLOCK)
    bt_own = _bias_tables(rel_bias, np.pad(d_new, ((0, 0), (0, 120)), constant_values=-1))
    bt_own = bt_own[:, :nd, :8].reshape(nq, 8, 1)
    ck_t = cache_k.transpose(0, 1, 3, 4, 2)
    cv_t = cache_v.transpose(0, 1, 3, 4, 2)

    c_all = jnp.concatenate([c_prompt, c_sample, jnp.zeros((-(bp + nseq) % 8, D_MODEL), F32)], axis=0)
    zero_gla = jnp.zeros((bp, GLA_HEADS, GLA_DK, GLA_DV), F32)
    zero_ret = jnp.zeros((bp, RET_HEADS, RET_DK, RET_DV), F32)

    xp = x_prompt.reshape(T, D_MODEL)
    xs = x_sample.reshape(ms_rows, D_MODEL)
    outs = {k: [] for k in ("kp", "vp", "gp", "rp", "ks", "vs", "gs", "rs")}
    for l in range(depth):
        mod = _ada_mod(c_all, ada_w, ada_b3, l)
        mp = [mod[0:1, j * D_MODEL:(j + 1) * D_MODEL] for j in range(6)]
        msm = [jnp.repeat(mod[bp:bp + nseq, j * D_MODEL:(j + 1) * D_MODEL], nd, axis=0) for j in range(6)]
        li = l // 2

        def channel(x, m, tm):
            if l % 2 == 0:
                return _ffn(x, n2_3, m[4], m[3], m[5], f1, f3, f2, l, li, tm)
            return _moe(x, n2_3, m[4], m[3], m[5], wr, e1, e3, e2, l, li, tm)

        P, G = _inproj(xp, n1_3, mp[1], mp[0], w_main, w_glr, l, 1024 if T % 1024 == 0 else MOBA_BLOCK)
        P3, G3 = P.reshape(1, T, N_MAIN), G.reshape(1, T, 128)
        oa, sg = _gla(P3, G3, w2p, gkb3, gla_nw3, zero_gla, l, CHUNK, CHUNK)
        ob, sr = _ret(P3, cos_p, sin_p, ret_nw3, zero_ret, l, CHUNK, CHUNK)
        qn, kn, kmean, kp, vt = _prompt_prep(P, qw3, kw3, l)
        qpt = _gate(qn, _placed_block_means(kmean.reshape(nb, MOBA_W)).T)
        oc = _prompt_attention(qpt, kp, vt, btab)
        xm = _merge(oa.reshape(T, V_W), ob.reshape(T, V_W), oc, P, xp, mp[2], wa, wb, wc, wo, l, 512 if T % 512 == 0 else MOBA_BLOCK)
        xp = channel(xm, mp, 512 if T % 512 == 0 else MOBA_BLOCK)
        outs["kp"].append(kn.reshape(bp, T, MOBA_HEADS, MOBA_HD))
        outs["vp"].append(P[:, COL_MV * MOBA_W:(COL_MV + 1) * MOBA_W].reshape(bp, T, MOBA_HEADS, MOBA_HD))
        outs["gp"].append(sg)
        outs["rp"].append(sr)

        Ps, Gs = _inproj(xs, n1_3, msm[1], msm[0], w_main, w_glr, l, ms_rows)
        Ps3, Gs3 = Ps.reshape(nseq, nd, N_MAIN), Gs.reshape(nseq, nd, 128)
        oas, sgs = _gla(Ps3, Gs3, w2p, gkb3, gla_nw3, state_gla[l], l, CHUNK, nd)
        obs, srs = _ret(Ps3, cos_s, sin_s, ret_nw3, state_ret[l], l, CHUNK, nd)
        qns, kns = _qknorm(Ps, qw3, kw3, l, ms_rows)
        vns = Ps[:, COL_MV * MOBA_W:(COL_MV + 1) * MOBA_W]
        qs4 = (qns * (MOBA_HD ** -0.5)).reshape(nseq, nd, MOBA_HEADS, MOBA_HD).transpose(0, 2, 1, 3)
        q_t = jnp.pad(qs4.transpose(0, 1, 3, 2), ((0, 0), (0, 0), (0, 0), (0, 128 - nd)))
        top = _decode_select(_decode_gates(page_table, ck_t, q_t, l))
        sel = top[:, :, :, :nd].transpose(0, 2, 3, 1).reshape(nseq, nq * MOBA_TOPK)
        head_major = lambda a: jnp.pad(a.reshape(nseq, nd, MOBA_HEADS, MOBA_HD).transpose(0, 2, 1, 3),
                                       ((0, 0), (0, 0), (0, 8 - nd), (0, 0)))
        ocs4 = _decode_attend(page_table, sel, ck_t, cv_t, qs4.reshape(nseq, nq, 1, MOBA_HD),
                              head_major(kns), head_major(vns), bt_past, bt_own, l, nd)
        ocs = ocs4.reshape(nseq, MOBA_HEADS, nd, MOBA_HD).transpose(0, 2, 1, 3)
        xms = _merge(oas.reshape(ms_rows, V_W), obs.reshape(ms_rows, V_W), ocs.reshape(ms_rows, MOBA_W).astype(BF16),
                     Ps, xs, msm[2], wa, wb, wc, wo, l, ms_rows)
        xs = channel(xms, msm, ms_rows)
        outs["ks"].append(kns.reshape(nseq, nd, MOBA_HEADS, MOBA_HD))
        outs["vs"].append(vns.reshape(nseq, nd, MOBA_HEADS, MOBA_HD))
        outs["gs"].append(sgs)
        outs["rs"].append(srs)

    st = lambda k: jnp.stack(outs[k])
    return (xp.reshape(bp, T, D_MODEL), xs.reshape(nseq, nd, D_MODEL), st("kp"), st("vp"), st("gp"), st("rp"),
            st("ks"), st("vs"), st("gs"), st("rs"))
```

```python
import functools
import math

import numpy as np
import jax
import jax.numpy as jnp
from jax import lax
from jax.experimental import pallas as pl
from jax.experimental.pallas import tpu as pltpu

F32 = jnp.float32
BF16 = jnp.bfloat16
BS = pl.BlockSpec
SDS = jax.ShapeDtypeStruct

D_MODEL = 1024
GLA_HEADS, GLA_DK, GLA_DV, GLA_LOWRANK, GLA_GATE_TEMP = 4, 64, 128, 16, 16.0
RET_HEADS, RET_DK, RET_DV = 4, 64, 128
ROPE_BASE = 10000.0
MOBA_HEADS, MOBA_HD, MOBA_BLOCK, MOBA_TOPK = 8, 64, 256, 3
N_BUCKETS, MAX_DISTANCE = 32, 128
D_FF, N_EXPERTS, TOP_K, D_FF_EXPERT = 2816, 8, 2, 1408
EPS = 1e-6
BIG = 1e30
NEG = -3.0e38
LOG2E = math.log2(math.e)

QK_W = GLA_HEADS * GLA_DK
V_W = GLA_HEADS * GLA_DV
MOBA_W = MOBA_HEADS * MOBA_HD
N_MAIN = 7680
COL_GQ, COL_GK, COL_RQ, COL_RK = 12, 13, 18, 19
COL_GV, COL_GR, COL_RV, COL_RG, COL_MQ, COL_MK, COL_MV = 7, 8, 10, 11, 12, 13, 14

VMEM_LIMIT_BYTES = 56 * 1024 * 1024
CHUNK = 128
ATTN_COLS = 256
ATTN_DENOM_ROWS = 16
ATTN_UNROLL = 8
PAGES_PER_STEP = 16
DECODE_ROWS = 8


def _cparams(*sem):
    return pltpu.CompilerParams(dimension_semantics=sem, vmem_limit_bytes=VMEM_LIMIT_BYTES)


def _dg(a, b, dims):
    return lax.dot_general(a, b, (dims, ((), ())), preferred_element_type=F32)


NN = ((1,), (0,))
NT = ((1,), (1,))
TN = ((0,), (0,))


def _split2(x):
    h = x.astype(BF16)
    m = (x - h.astype(F32)).astype(BF16)
    return h, m


def _split3(x):
    h = x.astype(BF16)
    r = x - h.astype(F32)
    m = r.astype(BF16)
    l = (r - m.astype(F32)).astype(BF16)
    return h, m, l


def _dot_x3(a, b, dims=NN):
    ah, am = _split2(a)
    bh, bm = _split2(b)
    return _dg(ah, bh, dims) + _dg(ah, bm, dims) + _dg(am, bh, dims)


def _sigmoid(x):
    return 1.0 / (1.0 + jnp.exp(-x))


def _silu(x):
    return x * _sigmoid(x)


def _mod_kernel(c_ref, w_ref, b_ref, o_ref):
    o_ref[...] = jnp.dot(c_ref[...].astype(BF16), w_ref[...].astype(BF16),
                         preferred_element_type=F32) + b_ref[...]


def _ada_mod(c_all, ada_w, ada_b3, l):
    R = c_all.shape[0]
    TN_ = 1536
    n = 6 * D_MODEL
    return pl.pallas_call(
        _mod_kernel, grid=(n // TN_,),
        in_specs=[BS((R, D_MODEL), lambda j: (0, 0)),
                  BS((None, D_MODEL, TN_), lambda j: (l, 0, j)),
                  BS((None, 1, TN_), lambda j: (l, 0, j))],
        out_specs=BS((R, TN_), lambda j: (0, j)),
        out_shape=SDS((R, n), F32),
        compiler_params=_cparams("arbitrary"))(c_all, ada_w, ada_b3)


def _modulated_norm(x, nw, sc, sh):
    ms = jnp.mean(x * x, axis=-1, keepdims=True)
    return (x * lax.rsqrt(ms + EPS) * nw) * (1.0 + sc) + sh


def _inproj_kernel(x_ref, nw_ref, sc_ref, sh_ref, w_ref, wg_ref, o_ref, og_ref, h_ref):
    @pl.when(pl.program_id(1) == 0)
    def _():
        h = _modulated_norm(x_ref[...], nw_ref[...], sc_ref[...], sh_ref[...]).astype(BF16)
        h_ref[...] = h
        og_ref[...] = jnp.dot(h, wg_ref[...], preferred_element_type=F32)

    o_ref[...] = jnp.dot(h_ref[...], w_ref[...], preferred_element_type=F32)


def _mod_spec(mod_rows, tm, ngrid):
    if mod_rows == 1:
        return BS((1, D_MODEL), (lambda i, j: (0, 0)) if ngrid == 2 else (lambda i: (0, 0)))
    return BS((tm, D_MODEL), (lambda i, j: (i, 0)) if ngrid == 2 else (lambda i: (i, 0)))


def _inproj(x, nw3, sc, sh, w_main, w_glr, l, tm):
    M = x.shape[0]
    TN_ = 1536
    ms = _mod_spec(sc.shape[0], tm, 2)
    return pl.pallas_call(
        _inproj_kernel, grid=(M // tm, N_MAIN // TN_),
        in_specs=[BS((tm, D_MODEL), lambda i, j: (i, 0)),
                  BS((None, 1, D_MODEL), lambda i, j: (l, 0, 0)),
                  ms, ms,
                  BS((None, D_MODEL, TN_), lambda i, j: (l, 0, j)),
                  BS((None, D_MODEL, 128), lambda i, j: (l, 0, 0))],
        out_specs=[BS((tm, TN_), lambda i, j: (i, j)),
                   BS((tm, 128), lambda i, j: (i, 0))],
        out_shape=[SDS((M, N_MAIN), F32), SDS((M, 128), F32)],
        scratch_shapes=[pltpu.VMEM((tm, D_MODEL), BF16)],
        compiler_params=_cparams("arbitrary", "arbitrary"))(x, nw3, sc, sh, w_main, w_glr)


def _load_rows(ref, pad_ref, rows, C):
    if rows == C:
        return ref[0]
    pad_ref[...] = jnp.zeros_like(pad_ref)
    pad_ref[0:rows, :] = ref[0]
    return pad_ref[...]


def _state_init(S_ref, s0_ref, nh, dk, dv):
    S_ref[...] = jnp.zeros_like(S_ref)
    for h in range(nh):
        S_ref[dk * h:dk * (h + 1), dv * h:dv * (h + 1)] = s0_ref[0, h]


def _state_out(so_ref, S_ref, nh, dk, dv):
    for h in range(nh):
        so_ref[0, h] = S_ref[dk * h:dk * (h + 1), dv * h:dv * (h + 1)]


def _block_diag_mask():
    r = lax.broadcasted_iota(jnp.int32, (QK_W, V_W), 0) // GLA_DK
    c = lax.broadcasted_iota(jnp.int32, (QK_W, V_W), 1) // GLA_DV
    return r == c


def _gla_kernel(q_ref, k_ref, v_ref, r_ref, g_ref, w2_ref, b_ref, nw_ref, s0_ref,
                o_ref, so_ref, S_ref, *pads, C, rows, nc):
    c = pl.program_id(1)

    @pl.when(c == 0)
    def _():
        _state_init(S_ref, s0_ref, GLA_HEADS, GLA_DK, GLA_DV)

    pads = list(pads) + [None] * 5
    q = _load_rows(q_ref, pads[0], rows, C)
    k = _load_rows(k_ref, pads[1], rows, C)
    v = _load_rows(v_ref, pads[2], rows, C)
    r = _load_rows(r_ref, pads[3], rows, C)
    g = _load_rows(g_ref, pads[4], rows, C)

    row = lax.broadcasted_iota(jnp.int32, (C, QK_W), 0)
    x = jnp.dot(g.astype(BF16), w2_ref[...].astype(BF16), preferred_element_type=F32) + b_ref[...]
    log_a = (jnp.minimum(x, 0.0) - jnp.log(1.0 + jnp.exp(-jnp.abs(x)))) * (1.0 / GLA_GATE_TEMP)
    if rows < C:
        log_a = jnp.where(row < rows, log_a, 0.0)

    ri = lax.broadcasted_iota(jnp.int32, (C, C), 0)
    ci = lax.broadcasted_iota(jnp.int32, (C, C), 1)
    causal = ri >= ci
    ltri = jnp.where(causal, 1.0, 0.0).astype(BF16)
    ones_c = jnp.ones((C, 128), BF16)
    pieces = _split3(log_a)
    b = sum(_dg(ltri, p, NN) for p in pieces)
    b_last_col = sum(_dg(p, ones_c, TN) for p in pieces)
    mid = max(min(rows, C) // 2, 1)
    b_ref_row = b[mid - 1:mid, :]
    b_last = b[C - 1:C, :]

    qs = q * (GLA_DK ** -0.5)
    qt_h, qt_m = _split2(qs * jnp.exp(b - b_ref_row))
    kt_h, kt_m = _split2(k * jnp.exp(b_ref_row - b))
    q_state = (qs * jnp.exp(b)).astype(BF16)
    k_state = (k * jnp.exp(b_last - b)).astype(BF16)
    vb = v.astype(BF16)

    S = S_ref[...]
    o_state = _dg(q_state, S.astype(BF16), NN)
    lane_head = lax.broadcasted_iota(jnp.int32, (C, QK_W), 1) // GLA_DK
    nw = nw_ref[...]
    for h in range(GLA_HEADS):
        mine = lane_head == h
        qh_h = jnp.where(mine, qt_h, jnp.zeros_like(qt_h))
        qh_m = jnp.where(mine, qt_m, jnp.zeros_like(qt_m))
        a = _dg(qh_h, kt_h, NT) + _dg(qh_h, kt_m, NT) + _dg(qh_m, kt_h, NT)
        a = jnp.where(causal, a, 0.0).astype(BF16)
        sl = slice(GLA_DV * h, GLA_DV * (h + 1))
        oh = _dg(a, vb[:, sl], NN) + o_state[:, sl]
        ms = jnp.mean(oh * oh, axis=-1, keepdims=True)
        y = oh * lax.rsqrt(ms + EPS) * nw[:, sl] * _silu(r[:, sl])
        o_ref[0, :, sl] = y[0:rows].astype(o_ref.dtype)

    u = _dg(k_state, vb, TN)
    e_col = jnp.exp(b_last_col)
    e_full = jnp.concatenate([e_col] * (V_W // 128), axis=1)
    S_ref[...] = e_full * S + jnp.where(_block_diag_mask(), u, 0.0)

    @pl.when(c == nc - 1)
    def _():
        _state_out(so_ref, S_ref, GLA_HEADS, GLA_DK, GLA_DV)


def _mixer_specs(rows):
    def col(width, idx):
        return BS((1, rows, width), lambda b, c: (b, c, idx))
    return col


def _gla(P3, G3, w2p, bias3, nw3, s0, l, C, rows):
    B, T, _ = P3.shape
    nc = T // rows
    col = _mixer_specs(rows)
    kern = functools.partial(_gla_kernel, C=C, rows=rows, nc=nc)
    scratch = [pltpu.VMEM((QK_W, V_W), F32)]
    if rows < C:
        scratch += [pltpu.VMEM((C, w), F32) for w in (QK_W, QK_W, V_W, V_W, 128)]
    return pl.pallas_call(
        kern, grid=(B, nc),
        in_specs=[col(QK_W, COL_GQ), col(QK_W, COL_GK), col(V_W, COL_GV), col(V_W, COL_GR),
                  BS((1, rows, 128), lambda b, c: (b, c, 0)),
                  BS((None, 128, QK_W), lambda b, c: (l, 0, 0)),
                  BS((None, 1, QK_W), lambda b, c: (l, 0, 0)),
                  BS((None, 1, V_W), lambda b, c: (l, 0, 0)),
                  BS((1, GLA_HEADS, GLA_DK, GLA_DV), lambda b, c: (b, 0, 0, 0))],
        out_specs=[BS((1, rows, V_W), lambda b, c: (b, c, 0)),
                   BS((1, GLA_HEADS, GLA_DK, GLA_DV), lambda b, c: (b, 0, 0, 0))],
        out_shape=[SDS((B, T, V_W), BF16), SDS((B, GLA_HEADS, GLA_DK, GLA_DV), F32)],
        scratch_shapes=scratch,
        compiler_params=_cparams("arbitrary", "arbitrary"))(P3, P3, P3, P3, G3, w2p, bias3, nw3, s0)


def _ret_kernel(q_ref, k_ref, v_ref, g_ref, cos_ref, sin_ref, dm_ref, rs_ref, ks_ref, cd_ref,
                nw_ref, s0_ref, o_ref, so_ref, S_ref, *pads, C, rows, nc):
    c = pl.program_id(1)

    @pl.when(c == 0)
    def _():
        _state_init(S_ref, s0_ref, RET_HEADS, RET_DK, RET_DV)

    pads = list(pads) + [None] * 4
    q = _load_rows(q_ref, pads[0], rows, C)
    k = _load_rows(k_ref, pads[1], rows, C)
    v = _load_rows(v_ref, pads[2], rows, C)
    g = _load_rows(g_ref, pads[3], rows, C)

    cos = cos_ref[...]
    sin = sin_ref[...]
    first_half = (lax.broadcasted_iota(jnp.int32, (C, QK_W), 1) % RET_DK) < (RET_DK // 2)

    def rope(x):
        partner = jnp.where(first_half, pltpu.roll(x, QK_W - RET_DK // 2, 1),
                            pltpu.roll(x, RET_DK // 2, 1))
        return x * cos + partner * sin

    qr = rope(q)
    kr = rope(k) * (RET_DK ** -0.5)
    qb = qr.astype(BF16)
    kb = kr.astype(BF16)
    vb = v.astype(BF16)
    k_state = (kr * ks_ref[...]).astype(BF16)

    S = S_ref[...]
    o_state = _dg(qb, S.astype(BF16), NN) * rs_ref[...]
    lane_head = lax.broadcasted_iota(jnp.int32, (C, QK_W), 1) // RET_DK
    nw = nw_ref[...]
    for h in range(RET_HEADS):
        a = _dg(jnp.where(lane_head == h, qb, jnp.zeros_like(qb)), kb, NT) * dm_ref[h]
        sl = slice(RET_DV * h, RET_DV * (h + 1))
        oh = _dg(a.astype(BF16), vb[:, sl], NN) + o_state[:, sl]
        oh = oh - jnp.mean(oh, axis=-1, keepdims=True)
        var = jnp.mean(oh * oh, axis=-1, keepdims=True)
        y = oh * lax.rsqrt(var + EPS) * nw[:, sl] * _silu(g[:, sl])
        o_ref[0, :, sl] = y[0:rows].astype(o_ref.dtype)

    u = _dg(k_state, vb, TN)
    S_ref[...] = cd_ref[...] * S + jnp.where(_block_diag_mask(), u, 0.0)

    @pl.when(c == nc - 1)
    def _():
        _state_out(so_ref, S_ref, RET_HEADS, RET_DK, RET_DV)


def _ret_constants(C, rows):
    log_gamma = np.log(1.0 - 2.0 ** (-5.0 - np.arange(RET_HEADS, dtype=np.float64)))
    bt = np.minimum(np.arange(C) + 1, rows).astype(np.float64)
    diff = bt[:, None] - bt[None, :]
    causal = np.arange(C)[:, None] >= np.arange(C)[None, :]
    dm = np.where(causal[None], np.exp(diff[None] * log_gamma[:, None, None]), 0.0)
    rs = np.exp(bt[:, None] * np.repeat(log_gamma, RET_DV)[None, :])
    ks = np.exp((bt[-1] - bt)[:, None] * np.repeat(log_gamma, RET_DK)[None, :])
    cd = np.exp(bt[-1] * np.repeat(log_gamma, RET_DV))[None, :]
    return (jnp.asarray(dm, F32), jnp.asarray(rs, F32), jnp.asarray(ks, F32), jnp.asarray(cd, F32))


def _ret(P3, cos_t, sin_t, nw3, s0, l, C, rows):
    B, T, _ = P3.shape
    nc = T // rows
    col = _mixer_specs(rows)
    dm, rs, ks, cd = _ret_constants(C, rows)
    kern = functools.partial(_ret_kernel, C=C, rows=rows, nc=nc)
    scratch = [pltpu.VMEM((QK_W, V_W), F32)]
    if rows < C:
        scratch += [pltpu.VMEM((C, w), F32) for w in (QK_W, QK_W, V_W, V_W)]
    const = lambda shape: BS(shape, lambda b, c: (0,) * len(shape))
    return pl.pallas_call(
        kern, grid=(B, nc),
        in_specs=[col(QK_W, COL_RQ), col(QK_W, COL_RK), col(V_W, COL_RV), col(V_W, COL_RG),
                  BS((C, QK_W), lambda b, c: (c, 0)), BS((C, QK_W), lambda b, c: (c, 0)),
                  const((RET_HEADS, C, C)), const((C, V_W)), const((C, QK_W)), const((1, V_W)),
                  BS((None, 1, V_W), lambda b, c: (l, 0, 0)),
                  BS((1, RET_HEADS, RET_DK, RET_DV), lambda b, c: (b, 0, 0, 0))],
        out_specs=[BS((1, rows, V_W), lambda b, c: (b, c, 0)),
                   BS((1, RET_HEADS, RET_DK, RET_DV), lambda b, c: (b, 0, 0, 0))],
        out_shape=[SDS((B, T, V_W), BF16), SDS((B, RET_HEADS, RET_DK, RET_DV), F32)],
        scratch_shapes=scratch,
        compiler_params=_cparams("arbitrary", "arbitrary"))(
            P3, P3, P3, P3, cos_t, sin_t, dm, rs, ks, cd, nw3, s0)


def _rope_tables(pos0, n):
    half = RET_DK // 2
    inv = ROPE_BASE ** (-jnp.arange(half, dtype=F32) / half)
    pos = (pos0 + jnp.arange(n, dtype=jnp.int32)).astype(F32)
    ang = pos[:, None] * inv[None, :]
    cos, sin = jnp.cos(ang), jnp.sin(ang)
    cos_t = jnp.tile(jnp.concatenate([cos, cos], -1), (1, RET_HEADS))
    sin_t = jnp.tile(jnp.concatenate([-sin, sin], -1), (1, RET_HEADS))
    return cos_t, sin_t


def _head_rms64(x, g_ref, w):
    xh, xm = _split2(x * x)
    ms = jnp.dot(xh, g_ref[...], preferred_element_type=F32) + jnp.dot(xm, g_ref[...], preferred_element_type=F32)
    return x * lax.rsqrt(ms + EPS) * w


def _qknorm_kernel(q_ref, k_ref, g_ref, qw_ref, kw_ref, qn_ref, kn_ref):
    qn_ref[...] = _head_rms64(q_ref[...], g_ref, qw_ref[...])
    kn_ref[...] = _head_rms64(k_ref[...], g_ref, kw_ref[...])


def _prompt_prep_kernel(q_ref, k_ref, v_ref, g_ref, pl_ref, eye_ref, qw_ref, kw_ref,
                        qn_ref, kn_ref, km_ref, kp_ref, vt_ref):
    i = pl.program_id(0)
    qn_ref[...] = _head_rms64(q_ref[...], g_ref, qw_ref[...])
    kn = _head_rms64(k_ref[...], g_ref, kw_ref[...])
    kn_ref[...] = kn
    km_ref[0] = jnp.mean(kn, axis=0, keepdims=True)
    lane = lax.broadcasted_iota(jnp.int32, (MOBA_BLOCK, 2 * MOBA_W), 1) % 128
    placed = jnp.dot(kn.astype(BF16), pl_ref[...], preferred_element_type=F32)
    kp_ref[...] = (placed + jnp.where(lane == MOBA_HD + i, BIG, 0.0)).astype(BF16)
    vt_ref[...] = _dg(eye_ref[...], v_ref[...].astype(BF16), NT).astype(BF16)


def _group_mean_matrix():
    r = np.arange(MOBA_W)
    return jnp.asarray((r[:, None] // MOBA_HD == r[None, :] // MOBA_HD) / MOBA_HD, BF16)


def _placement_matrix():
    r = np.arange(MOBA_W)
    c = np.arange(2 * MOBA_W)
    tgt = (r // MOBA_HD) * 128 + r % MOBA_HD
    return jnp.asarray(tgt[:, None] == c[None, :], BF16)


def _qknorm(P, qw3, kw3, l, tm):
    M = P.shape[0]
    return pl.pallas_call(
        _qknorm_kernel, grid=(M // tm,),
        in_specs=[BS((tm, MOBA_W), lambda i: (i, COL_MQ)), BS((tm, MOBA_W), lambda i: (i, COL_MK)),
                  BS((MOBA_W, MOBA_W), lambda i: (0, 0)),
                  BS((None, 1, MOBA_W), lambda i: (l, 0, 0)), BS((None, 1, MOBA_W), lambda i: (l, 0, 0))],
        out_specs=[BS((tm, MOBA_W), lambda i: (i, 0)), BS((tm, MOBA_W), lambda i: (i, 0))],
        out_shape=[SDS((M, MOBA_W), F32), SDS((M, MOBA_W), F32)],
        compiler_params=_cparams("arbitrary"))(P, P, _group_mean_matrix(), qw3, kw3)


def _prompt_prep(P, qw3, kw3, l):
    T = P.shape[0]
    nb = T // MOBA_BLOCK
    tm = MOBA_BLOCK
    row = lambda w, idx: BS((tm, w), lambda i: (i, idx))
    return pl.pallas_call(
        _prompt_prep_kernel, grid=(nb,),
        in_specs=[row(MOBA_W, COL_MQ), row(MOBA_W, COL_MK), row(MOBA_W, COL_MV),
                  BS((MOBA_W, MOBA_W), lambda i: (0, 0)),
                  BS((MOBA_W, 2 * MOBA_W), lambda i: (0, 0)),
                  BS((MOBA_W, MOBA_W), lambda i: (0, 0)),
                  BS((None, 1, MOBA_W), lambda i: (l, 0, 0)), BS((None, 1, MOBA_W), lambda i: (l, 0, 0))],
        out_specs=[row(MOBA_W, 0), row(MOBA_W, 0), BS((1, 1, MOBA_W), lambda i: (i, 0, 0)),
                   row(2 * MOBA_W, 0), BS((MOBA_W, tm), lambda i: (0, i))],
        out_shape=[SDS((T, MOBA_W), F32), SDS((T, MOBA_W), F32), SDS((nb, 1, MOBA_W), F32),
                   SDS((T, 2 * MOBA_W), BF16), SDS((MOBA_W, T), BF16)],
        compiler_params=_cparams("arbitrary"))(
            P, P, P, _group_mean_matrix(), _placement_matrix(), jnp.eye(MOBA_W, dtype=BF16), qw3, kw3)


def _placed_block_means(kmean):
    nb = kmean.shape[0]
    km = kmean.reshape(nb, MOBA_HEADS, MOBA_HD).transpose(1, 2, 0)
    km = jnp.pad(km, ((0, 0), (0, 0), (MOBA_HD, MOBA_HD - nb)))
    full = km[:, :, None, :] * jnp.eye(MOBA_HEADS, dtype=F32)[:, None, :, None]
    return full.reshape(MOBA_W, 2 * MOBA_W)


def _select_top3(gate, valid, axis=-1):
    g1 = jnp.where(valid, gate, NEG)
    m1 = jnp.max(g1, axis=axis, keepdims=True)
    g2 = jnp.where(g1 >= m1, NEG, g1)
    m2 = jnp.max(g2, axis=axis, keepdims=True)
    g3 = jnp.where(g2 >= m2, NEG, g2)
    m3 = jnp.max(g3, axis=axis, keepdims=True)
    return valid & (g1 >= m3)


def _gate_kernel(qn_ref, kmt_ref, plt_ref, qpt_ref):
    i = pl.program_id(0)
    qn = qn_ref[...]
    gate = _dg(kmt_ref[...].astype(BF16), qn.astype(BF16), NT)
    qpart = _dg(plt_ref[...], (qn * (MOBA_HD ** -0.5 * LOG2E)).astype(BF16), NT)
    blk = lax.broadcasted_iota(jnp.int32, (128, MOBA_BLOCK), 0) - MOBA_HD
    valid = (blk >= 0) & (blk < i)
    for h in range(MOBA_HEADS):
        sl = slice(128 * h, 128 * (h + 1))
        sel = _select_top3(gate[sl], valid, axis=0) | (blk == i)
        maskpart = jnp.where((blk >= 0) & jnp.logical_not(sel), -1.0, 0.0)
        qpt_ref[sl, :] = (qpart[sl] + maskpart).astype(BF16)


def _gate(qn, km_placed_t):
    T = qn.shape[0]
    nb = T // MOBA_BLOCK
    return pl.pallas_call(
        _gate_kernel, name="moba_gate", grid=(nb,),
        in_specs=[BS((MOBA_BLOCK, MOBA_W), lambda i: (i, 0)),
                  BS((2 * MOBA_W, MOBA_W), lambda i: (0, 0)),
                  BS((2 * MOBA_W, MOBA_W), lambda i: (0, 0))],
        out_specs=BS((2 * MOBA_W, MOBA_BLOCK), lambda i: (0, i)),
        out_shape=SDS((2 * MOBA_W, T), BF16),
        compiler_params=_cparams("arbitrary"))(qn, km_placed_t, _placement_matrix().T)


def _bucket_np(dist):
    dist = np.maximum(dist, 0)
    max_exact = N_BUCKETS // 2
    large = max_exact + (np.log(np.maximum(dist, max_exact).astype(np.float64) / max_exact)
                         / math.log(MAX_DISTANCE / max_exact) * (N_BUCKETS - max_exact)).astype(np.int64)
    large = np.minimum(large, N_BUCKETS - 1)
    return np.where(dist < max_exact, dist, large).astype(np.int32)


def _bias_kernel(rb_ref, idx_ref, ok_ref, o_ref, *, scale):
    h = pl.program_id(0)
    idx = idx_ref[...]
    far = rb_ref[N_BUCKETS - 1, h]
    acc = jnp.zeros(idx.shape, F32)
    for b in range(N_BUCKETS - 1):
        acc = jnp.where(idx == b, (rb_ref[b, h] - far) * scale, acc)
    o_ref[0] = jnp.where(ok_ref[...] > 0, acc, -BIG)


def _bias_tables(rel_bias, dist, scale=1.0):
    idx = jnp.asarray(_bucket_np(dist))
    ok = jnp.asarray((dist >= 0).astype(np.int32))
    shp = dist.shape
    zeros = (0,) * len(shp)
    return pl.pallas_call(
        functools.partial(_bias_kernel, scale=scale), name="moba_bias_tables", grid=(MOBA_HEADS,),
        in_specs=[BS(memory_space=pltpu.SMEM), BS(shp, lambda h: zeros), BS(shp, lambda h: zeros)],
        out_specs=BS((1,) + shp, lambda h: (h,) + zeros),
        out_shape=SDS((MOBA_HEADS,) + shp, F32),
        compiler_params=_cparams("arbitrary"))(rel_bias, idx, ok)


def _attn_kernel(qpt_ref, kp_ref, vt_ref, bt_ref, eye_ref, o_ref, m_ref, acc_ref, s_ref):
    i = pl.program_id(1)
    tq = MOBA_BLOCK
    m_ref[...] = jnp.full(m_ref.shape, NEG, F32)
    acc_ref[...] = jnp.zeros_like(acc_ref)

    def logits(slot, hh, n, t):
        qt = qpt_ref[128 * hh:128 * (hh + 1), :]
        start = pl.multiple_of(n * MOBA_BLOCK, MOBA_BLOCK)
        kblk = kp_ref[pl.ds(start, MOBA_BLOCK), 128 * hh:128 * (hh + 1)]
        s = jnp.dot(kblk, qt, preferred_element_type=F32)
        if t is not None:
            s = s + bt_ref[hh, t]
        s_ref[slot] = s

    def softmax_pv(slot, hh, n):
        ps, alphas = [], []
        for c in range(tq // ATTN_COLS):
            cols = slice(ATTN_COLS * c, ATTN_COLS * (c + 1))
            sc = s_ref[slot, :, cols]
            m_old = m_ref[hh, :, cols]
            m_new = jnp.maximum(m_old, jnp.max(sc, axis=0, keepdims=True))
            alpha = jnp.exp2(m_old - m_new)
            p = jnp.exp2(sc - m_new)
            m_ref[hh, :, cols] = m_new
            ps.append(p.astype(BF16))
            alphas.append(alpha)
        start = pl.multiple_of(n * MOBA_BLOCK, MOBA_BLOCK)
        vt = vt_ref[MOBA_HD * hh:MOBA_HD * (hh + 1), pl.ds(start, MOBA_BLOCK)]
        vt_ones = jnp.concatenate([vt, jnp.ones((ATTN_DENOM_ROWS, MOBA_BLOCK), BF16)], axis=0)
        pv = jnp.dot(vt_ones, jnp.concatenate(ps, axis=1), preferred_element_type=F32)
        acc_ref[hh] = jnp.concatenate(alphas, axis=1) * acc_ref[hh] + pv

    def group(tiles):
        for slot, (hh, n, t) in enumerate(tiles):
            logits(slot, hh, n, t)
        for slot, (hh, n, t) in enumerate(tiles):
            softmax_pv(slot, hh, n)

    n_far = jnp.maximum(i - 1, 0)

    def far_tiles(first, count):
        return [(hh, first + b, None) for b in range(count) for hh in range(2)]

    def far_body(j, carry):
        group(far_tiles(ATTN_UNROLL * j, ATTN_UNROLL))
        return carry

    lax.fori_loop(0, n_far // ATTN_UNROLL, far_body, 0)
    done = (n_far // ATTN_UNROLL) * ATTN_UNROLL
    width = ATTN_UNROLL // 2
    while width >= 1:
        @pl.when((n_far - done) % (2 * width) >= width)
        def _(done=done, width=width):
            group(far_tiles(done, width))
        done = done + jnp.where((n_far - done) % (2 * width) >= width, width, 0)
        width //= 2

    @pl.when(i >= 1)
    def _():
        group([(0, i - 1, 1), (1, i - 1, 1), (0, i, 0), (1, i, 0)])

    @pl.when(i == 0)
    def _():
        group([(0, i, 0), (1, i, 0)])

    out_t = jnp.concatenate([acc_ref[hh, 0:MOBA_HD] / acc_ref[hh, MOBA_HD:MOBA_HD + 1] for hh in range(2)],
                            axis=0).astype(BF16)
    o_ref[...] = _dg(out_t, eye_ref[...], TN).astype(o_ref.dtype)


def _prompt_attention(qpt, kp, vt, btab_t):
    T = kp.shape[0]
    nb = T // MOBA_BLOCK
    return pl.pallas_call(
        _attn_kernel, name="moba_prompt_attention", grid=(MOBA_HEADS // 2, nb),
        in_specs=[BS((256, MOBA_BLOCK), lambda hp, i: (hp, i)),
                  BS((T, 256), lambda hp, i: (0, hp)),
                  BS((128, T), lambda hp, i: (hp, 0)),
                  BS((2, 2, MOBA_BLOCK, MOBA_BLOCK), lambda hp, i: (hp, 0, 0, 0)),
                  BS((128, 128), lambda hp, i: (0, 0))],
        out_specs=BS((MOBA_BLOCK, 128), lambda hp, i: (i, hp)),
        out_shape=SDS((T, MOBA_W), BF16),
        scratch_shapes=[pltpu.VMEM((2, 1, MOBA_BLOCK), F32),
                        pltpu.VMEM((2, MOBA_HD + ATTN_DENOM_ROWS, MOBA_BLOCK), F32),
                        pltpu.VMEM((2 * ATTN_UNROLL, MOBA_BLOCK, MOBA_BLOCK), F32)],
        compiler_params=_cparams("arbitrary", "arbitrary"))(qpt, kp, vt, btab_t, jnp.eye(128, dtype=BF16))


def _decode_gate_kernel(pt_ref, *refs):
    npg = PAGES_PER_STEP
    pages, (q_ref, g_ref) = refs[:npg], refs[npg:]
    qb = _bf16_round(q_ref[0])
    pages_per_block = MOBA_BLOCK // pages[0].shape[-1]
    for blk in range(npg // pages_per_block):
        tokens = pages[pages_per_block * blk][...]
        for g in range(1, pages_per_block):
            tokens = tokens + pages[pages_per_block * blk + g][...]
        km = jnp.sum(tokens, axis=-1, keepdims=True) * (1.0 / MOBA_BLOCK)
        g_ref[0, blk] = jnp.sum(_bf16_round(km) * qb, axis=1)


def _decode_gates(page_table, cache_t, q_t, l):
    nseq, n_pages = page_table.shape
    page = cache_t.shape[-1]
    npg = PAGES_PER_STEP
    ppb = MOBA_BLOCK // page
    page_specs = [BS((None, None, MOBA_HEADS, MOBA_HD, page), functools.partial(
        lambda b, g, pt, r: (l, pt[b, g * npg + r], 0, 0, 0), r=r)) for r in range(npg)]
    grid_spec = pltpu.PrefetchScalarGridSpec(
        num_scalar_prefetch=1, grid=(nseq, n_pages // npg),
        in_specs=page_specs + [BS((1, MOBA_HEADS, MOBA_HD, 128), lambda b, g, pt: (b, 0, 0, 0))],
        out_specs=BS((1, npg // ppb, MOBA_HEADS, 128), lambda b, g, pt: (b, g, 0, 0)))
    return pl.pallas_call(
        _decode_gate_kernel, name="decode_gates", grid_spec=grid_spec,
        out_shape=SDS((nseq, n_pages // ppb, MOBA_HEADS, 128), F32),
        compiler_params=_cparams("arbitrary", "arbitrary"))(page_table, *([cache_t] * npg), q_t)


def _decode_select_kernel(g_ref, o_ref):
    gate = g_ref[0]
    nblk = gate.shape[0]
    blk = lax.broadcasted_iota(jnp.int32, gate.shape, 0)
    for t in range(MOBA_TOPK):
        m = jnp.max(gate, axis=0, keepdims=True)
        idx = jnp.min(jnp.where(gate >= m, blk, nblk), axis=0, keepdims=True)
        o_ref[0, t] = idx[0]
        gate = jnp.where(blk == idx, NEG, gate)


def _decode_select(gates):
    nseq, nblk = gates.shape[:2]
    return pl.pallas_call(
        _decode_select_kernel, name="decode_select", grid=(nseq,),
        in_specs=[BS((1, nblk, MOBA_HEADS, 128), lambda b: (b, 0, 0, 0))],
        out_specs=BS((1, MOBA_TOPK, MOBA_HEADS, 128), lambda b: (b, 0, 0, 0)),
        out_shape=SDS((nseq, MOBA_TOPK, MOBA_HEADS, 128), jnp.int32),
        compiler_params=_cparams("arbitrary"))(gates)


def _bf16_round(x):
    return x.astype(BF16).astype(F32)


def _decode_attend_kernel(pt_ref, sel_ref, ck_ref, cv_ref, q_ref, kn_ref, vn_ref, bpast_ref, bown_ref, o_ref,
                          kbuf, vbuf, sem, *, l, nd, nblk, page):
    b = pl.program_id(0)
    rows = MOBA_HEADS * nd
    trips = rows // DECODE_ROWS
    pages_per_block = MOBA_BLOCK // page

    def copies(i, slot):
        out = []
        for u in range(DECODE_ROWS):
            r = DECODE_ROWS * i + u
            h = r // nd
            for t in range(MOBA_TOPK):
                n = sel_ref[b, MOBA_TOPK * r + t]
                for g in range(pages_per_block):
                    pg = pt_ref[b, pages_per_block * n + g]
                    dst = slice(page * g, page * (g + 1))
                    out.append(pltpu.make_async_copy(ck_ref.at[l, pg, h], kbuf.at[slot, u, t, :, dst],
                                                     sem.at[slot, 0, u, t, g]))
                    out.append(pltpu.make_async_copy(cv_ref.at[l, pg, h], vbuf.at[slot, u, t, :, dst],
                                                     sem.at[slot, 1, u, t, g]))
        return out

    for c in copies(0, 0):
        c.start()

    def body(i, carry):
        slot = i % 2

        @pl.when(i + 1 < trips)
        def _():
            for c in copies(i + 1, 1 - slot):
                c.start()

        for c in copies(i, slot):
            c.wait()

        steps = [attend_row(DECODE_ROWS * i + u, kbuf.at[slot, u], vbuf.at[slot, u]) for u in range(DECODE_ROWS)]
        for _ in range(3):
            for st in steps:
                next(st, None)
        return carry

    def attend_row(r, kbuf, vbuf):
        h = r // nd
        qv = q_ref[0, r]
        qmat = jnp.broadcast_to(qv, (8, MOBA_HD)).astype(BF16)
        logits = []
        for t in range(MOBA_TOPK):
            s = jnp.dot(qmat, kbuf[t].astype(BF16), preferred_element_type=F32)
            is_last = sel_ref[b, MOBA_TOPK * r + t] == nblk - 1
            logits.append(s + jnp.where(is_last, bpast_ref[pl.ds(r, 1), :], 0.0))
        yield
        s_own = jnp.sum(_bf16_round(kn_ref[0, h]) * _bf16_round(qv), axis=-1, keepdims=True) + bown_ref[r]
        m = jnp.max(s_own, axis=0, keepdims=True)
        for s in logits:
            m = jnp.maximum(m, jnp.max(s[0:1], axis=-1, keepdims=True))
        p_own = jnp.exp(s_own - m)
        denom = jnp.sum(p_own, axis=0, keepdims=True)
        probs = [jnp.exp(s - m) for s in logits]
        for p in probs:
            denom = denom + jnp.sum(p[0:1], axis=-1, keepdims=True)
        inv = 1.0 / denom
        yield
        acc = jnp.zeros((8, MOBA_HD), F32)
        for t in range(MOBA_TOPK):
            acc = acc + _dg((probs[t] * inv).astype(BF16), vbuf[t].astype(BF16), NT)
        own = jnp.sum(_bf16_round(p_own * inv) * _bf16_round(vn_ref[0, h]), axis=0, keepdims=True)
        o_ref[0, r] = acc[0:1] + own

    lax.fori_loop(0, trips, body, 0)


def _decode_attend(page_table, sel, cache_k, cache_v, q_rows, kn_new, vn_new, b_past, b_own, l, nd):
    nseq, n_pages = page_table.shape
    page = cache_k.shape[-1]
    rows = q_rows.shape[1]
    per_seq = lambda shape: BS((1,) + shape, lambda b, pt, sl: (b,) + (0,) * len(shape))
    const = lambda shape: BS(shape, lambda b, pt, sl: (0,) * len(shape))
    assert rows % DECODE_ROWS == 0
    slab = pltpu.VMEM((2, DECODE_ROWS, MOBA_TOPK, MOBA_HD, MOBA_BLOCK), F32)
    grid_spec = pltpu.PrefetchScalarGridSpec(
        num_scalar_prefetch=2, grid=(nseq,),
        in_specs=[BS(memory_space=pl.ANY), BS(memory_space=pl.ANY),
                  per_seq((rows, 1, MOBA_HD)), per_seq((MOBA_HEADS, 8, MOBA_HD)), per_seq((MOBA_HEADS, 8, MOBA_HD)),
                  const((rows, MOBA_BLOCK)), const((rows, 8, 1))],
        out_specs=per_seq((rows, 1, MOBA_HD)),
        scratch_shapes=[slab, slab, pltpu.SemaphoreType.DMA((2, 2, DECODE_ROWS, MOBA_TOPK, MOBA_BLOCK // page))])
    kern = functools.partial(_decode_attend_kernel, l=l, nd=nd, nblk=n_pages * page // MOBA_BLOCK, page=page)
    return pl.pallas_call(
        kern, name="decode_attend", grid_spec=grid_spec,
        out_shape=SDS((nseq, rows, 1, MOBA_HD), F32),
        compiler_params=_cparams("arbitrary"))(page_table, sel, cache_k, cache_v, q_rows, kn_new, vn_new, b_past, b_own)


def _merge_kernel(oa_ref, ob_ref, oc_ref, mg_ref, x_ref, g1_ref, wa_ref, wb_ref, wc_ref, wo_ref, o_ref):
    mg = mg_ref[...]
    ya = jnp.dot(oa_ref[...], wa_ref[...], preferred_element_type=F32)
    yb = jnp.dot(ob_ref[...], wb_ref[...], preferred_element_type=F32)
    yc = jnp.dot(oc_ref[...], wc_ref[...], preferred_element_type=F32)
    merged = (_sigmoid(mg[:, 0:D_MODEL]) * ya + _sigmoid(mg[:, D_MODEL:2 * D_MODEL]) * yb
              + _sigmoid(mg[:, 2 * D_MODEL:]) * yc)
    mix = jnp.dot(merged.astype(BF16), wo_ref[...], preferred_element_type=F32)
    o_ref[...] = x_ref[...] + g1_ref[...] * mix


def _merge(oa, ob, oc, P, x, g1, wa, wb, wc, wo, l, tm):
    M = x.shape[0]
    row = lambda w: BS((tm, w), lambda i: (i, 0))
    wspec = lambda k: BS((None, k, D_MODEL), lambda i: (l, 0, 0))
    return pl.pallas_call(
        _merge_kernel, grid=(M // tm,),
        in_specs=[row(V_W), row(V_W), row(MOBA_W), row(3 * D_MODEL), row(D_MODEL),
                  _mod_spec(g1.shape[0], tm, 1),
                  wspec(V_W), wspec(V_W), wspec(MOBA_W), wspec(D_MODEL)],
        out_specs=row(D_MODEL),
        out_shape=SDS((M, D_MODEL), F32),
        compiler_params=_cparams("arbitrary"))(oa, ob, oc, P, x, g1, wa, wb, wc, wo)


def _ffn_kernel(x_ref, nw_ref, sc_ref, sh_ref, g2_ref, w1_ref, w3_ref, w2_ref, o_ref, h_ref, acc_ref):
    j = pl.program_id(1)

    @pl.when(j == 0)
    def _():
        h_ref[...] = _modulated_norm(x_ref[...], nw_ref[...], sc_ref[...], sh_ref[...]).astype(BF16)
        acc_ref[...] = jnp.zeros_like(acc_ref)

    h = h_ref[...]
    a = jnp.dot(h, w1_ref[...], preferred_element_type=F32)
    b = jnp.dot(h, w3_ref[...], preferred_element_type=F32)
    acc_ref[...] += jnp.dot((_silu(a) * b).astype(BF16), w2_ref[...], preferred_element_type=F32)

    @pl.when(j == pl.num_programs(1) - 1)
    def _():
        o_ref[...] = x_ref[...] + g2_ref[...] * acc_ref[...]


def _ffn(x, nw3, sc, sh, g2, w1, w3, w2, l, li, tm):
    M = x.shape[0]
    tf = D_FF // 2
    ms = _mod_spec(sc.shape[0], tm, 2)
    return pl.pallas_call(
        _ffn_kernel, grid=(M // tm, D_FF // tf),
        in_specs=[BS((tm, D_MODEL), lambda i, j: (i, 0)),
                  BS((None, 1, D_MODEL), lambda i, j: (l, 0, 0)), ms, ms, ms,
                  BS((None, D_MODEL, tf), lambda i, j: (li, 0, j)),
                  BS((None, D_MODEL, tf), lambda i, j: (li, 0, j)),
                  BS((None, tf, D_MODEL), lambda i, j: (li, j, 0))],
        out_specs=BS((tm, D_MODEL), lambda i, j: (i, 0)),
        out_shape=SDS((M, D_MODEL), F32),
        scratch_shapes=[pltpu.VMEM((tm, D_MODEL), BF16), pltpu.VMEM((tm, D_MODEL), F32)],
        compiler_params=_cparams("arbitrary", "arbitrary"))(x, nw3, sc, sh, g2, w1, w3, w2)


def _moe_kernel(x_ref, nw_ref, sc_ref, sh_ref, g2_ref, wr_ref, w1_ref, w3_ref, w2_ref, o_ref,
                h_ref, gate_ref, acc_ref):
    e = pl.program_id(1)
    tm = x_ref.shape[0]
    lane = lax.broadcasted_iota(jnp.int32, (tm, 128), 1)

    @pl.when(e == 0)
    def _():
        h = _modulated_norm(x_ref[...], nw_ref[...], sc_ref[...], sh_ref[...])
        h_ref[...] = h.astype(BF16)
        router = jnp.dot(h.astype(BF16), wr_ref[...].astype(BF16), preferred_element_type=F32)
        logits = jnp.where(lane < N_EXPERTS, router, NEG)
        m1 = jnp.max(logits, axis=-1, keepdims=True)
        i1 = jnp.min(jnp.where(logits >= m1, lane, 128), axis=-1, keepdims=True)
        rest = jnp.where(lane == i1, NEG, logits)
        m2 = jnp.max(rest, axis=-1, keepdims=True)
        i2 = jnp.min(jnp.where(rest >= m2, lane, 128), axis=-1, keepdims=True)
        e2 = jnp.exp(m2 - m1)
        w_first = 1.0 / (1.0 + e2)
        gate_ref[...] = jnp.where(lane == i1, w_first, 0.0) + jnp.where(lane == i2, e2 * w_first, 0.0)
        acc_ref[...] = jnp.zeros_like(acc_ref)

    h = h_ref[...]
    a = jnp.dot(h, w1_ref[...], preferred_element_type=F32)
    b = jnp.dot(h, w3_ref[...], preferred_element_type=F32)
    y = jnp.dot((_silu(a) * b).astype(BF16), w2_ref[...], preferred_element_type=F32)
    ge = jnp.sum(jnp.where(lane == e, gate_ref[...], 0.0), axis=-1, keepdims=True)
    acc_ref[...] += ge * y

    @pl.when(e == pl.num_programs(1) - 1)
    def _():
        o_ref[...] = x_ref[...] + g2_ref[...] * acc_ref[...]


def _moe(x, nw3, sc, sh, g2, wr, w1, w3, w2, l, li, tm):
    M = x.shape[0]
    ms = _mod_spec(sc.shape[0], tm, 2)
    return pl.pallas_call(
        _moe_kernel, grid=(M // tm, N_EXPERTS),
        in_specs=[BS((tm, D_MODEL), lambda i, e: (i, 0)),
                  BS((None, 1, D_MODEL), lambda i, e: (l, 0, 0)), ms, ms, ms,
                  BS((None, D_MODEL, 128), lambda i, e: (li, 0, 0)),
                  BS((None, None, D_MODEL, D_FF_EXPERT), lambda i, e: (li, e, 0, 0)),
                  BS((None, None, D_MODEL, D_FF_EXPERT), lambda i, e: (li, e, 0, 0)),
                  BS((None, None, D_FF_EXPERT, D_MODEL), lambda i, e: (li, e, 0, 0))],
        out_specs=BS((tm, D_MODEL), lambda i, e: (i, 0)),
        out_shape=SDS((M, D_MODEL), F32),
        scratch_shapes=[pltpu.VMEM((tm, D_MODEL), BF16), pltpu.VMEM((tm, 128), F32),
                        pltpu.VMEM((tm, D_MODEL), F32)],
        compiler_params=_cparams("arbitrary", "arbitrary"))(x, nw3, sc, sh, g2, wr, w1, w3, w2)


def kernel(x_prompt, x_sample, cache_k, cache_v, state_gla, state_ret, page_table, c_prompt, c_sample,
           ada_w, ada_b, norm1, norm2, w_in, gla_gk_w2, gla_gk_b, gla_norm, ret_norm, moba_qnorm,
           moba_knorm, rel_bias, w_br_gla, w_br_ret, w_br_moba, w_out, ffn_w1, ffn_w3, ffn_w2,
           moe_router, moe_w1, moe_w3, moe_w2):
    depth = w_in.shape[0]
    bp, T, _ = x_prompt.shape
    nseq, nd, _ = x_sample.shape
    n_pages, page = page_table.shape[1], cache_k.shape[2]
    past_len = n_pages * page
    nb = T // MOBA_BLOCK
    assert bp == 1 and T % MOBA_BLOCK == 0 and nb <= MOBA_HD and page == 128
    assert past_len % MOBA_BLOCK == 0 and n_pages % PAGES_PER_STEP == 0 and nd <= 8
    ms_rows = nseq * nd
    nq = nd * MOBA_HEADS

    sizes = np.cumsum([QK_W, QK_W, V_W, GLA_LOWRANK, V_W, QK_W, QK_W, V_W, V_W, MOBA_W, MOBA_W, MOBA_W])
    glr0, glr1, mg0 = int(sizes[2]), int(sizes[3]), int(sizes[-1])
    w_main = jnp.concatenate([w_in[:, :, mg0:], w_in[:, :, :glr0], w_in[:, :, glr1:mg0]], axis=2).astype(BF16)
    w_glr = jnp.pad(w_in[:, :, glr0:glr1], ((0, 0), (0, 0), (0, 128 - GLA_LOWRANK))).astype(BF16)
    w2p = jnp.pad(gla_gk_w2, ((0, 0), (0, 128 - GLA_LOWRANK), (0, 0)))
    gkb3 = gla_gk_b[:, None, :]
    gla_nw3 = jnp.tile(gla_norm, (1, GLA_HEADS))[:, None, :]
    ret_nw3 = jnp.tile(ret_norm, (1, RET_HEADS))[:, None, :]
    qw3 = jnp.tile(moba_qnorm, (1, MOBA_HEADS))[:, None, :]
    kw3 = jnp.tile(moba_knorm, (1, MOBA_HEADS))[:, None, :]
    n1_3, n2_3 = norm1[:, None, :], norm2[:, None, :]
    ada_b3 = ada_b[:, None, :]
    wa, wb, wc, wo = (w.astype(BF16) for w in (w_br_gla, w_br_ret, w_br_moba, w_out))
    f1, f3, f2 = ffn_w1.astype(BF16), ffn_w3.astype(BF16), ffn_w2.astype(BF16)
    e1, e3, e2 = moe_w1.astype(BF16), moe_w3.astype(BF16), moe_w2.astype(BF16)
    wr = jnp.pad(moe_router, ((0, 0), (0, 0), (0, 128 - N_EXPERTS)))

    cos_p, sin_p = _rope_tables(0, T)
    cos_s, sin_s = _rope_tables(past_len, CHUNK)
    tq = np.arange(MOBA_BLOCK)
    d_own = tq[:, None] - tq[None, :]
    btab = _bias_tables(rel_bias, np.stack([d_own.T, d_own.T + MOBA_BLOCK]), LOG2E)
    qi = np.arange(8)
    d_past = (MOBA_BLOCK + qi[:, None] - tq[None, :])
    d_new = np.where((qi[None, :] < nd) & (qi[:, None] < nd), qi[:, None] - qi[None, :], -1)
    bt_past = _bias_tables(rel_bias, d_past)[:, :nd].reshape(nq, MOBA_BLOCK)
    bt_own = _bias_tables(rel_bias, np.pad(d_new, ((0, 0), (0, 120)), constant_values=-1))
    bt_own = bt_own[:, :nd, :8].reshape(nq, 8, 1)
    ck_t = cache_k.transpose(0, 1, 3, 4, 2)
    cv_t = cache_v.transpose(0, 1, 3, 4, 2)

    c_all = jnp.concatenate([c_prompt, c_sample, jnp.zeros((-(bp + nseq) % 8, D_MODEL), F32)], axis=0)
    zero_gla = jnp.zeros((bp, GLA_HEADS, GLA_DK, GLA_DV), F32)
    zero_ret = jnp.zeros((bp, RET_HEADS, RET_DK, RET_DV), F32)

    xp = x_prompt.reshape(T, D_MODEL)
    xs = x_sample.reshape(ms_rows, D_MODEL)
    outs = {k: [] for k in ("kp", "vp", "gp", "rp", "ks", "vs", "gs", "rs")}
    for l in range(depth):
        mod = _ada_mod(c_all, ada_w, ada_b3, l)
        mp = [mod[0:1, j * D_MODEL:(j + 1) * D_MODEL] for j in range(6)]
        msm = [jnp.repeat(mod[bp:bp + nseq, j * D_MODEL:(j + 1) * D_MODEL], nd, axis=0) for j in range(6)]
        li = l // 2

        def channel(x, m, tm):
            if l % 2 == 0:
                return _ffn(x, n2_3, m[4], m[3], m[5], f1, f3, f2, l, li, tm)
            return _moe(x, n2_3, m[4], m[3], m[5], wr, e1, e3, e2, l, li, tm)

        P, G = _inproj(xp, n1_3, mp[1], mp[0], w_main, w_glr, l, 1024 if T % 1024 == 0 else MOBA_BLOCK)
        P3, G3 = P.reshape(1, T, N_MAIN), G.reshape(1, T, 128)
        oa, sg = _gla(P3, G3, w2p, gkb3, gla_nw3, zero_gla, l, CHUNK, CHUNK)
        ob, sr = _ret(P3, cos_p, sin_p, ret_nw3, zero_ret, l, CHUNK, CHUNK)
        qn, kn, kmean, kp, vt = _prompt_prep(P, qw3, kw3, l)
        qpt = _gate(qn, _placed_block_means(kmean.reshape(nb, MOBA_W)).T)
        oc = _prompt_attention(qpt, kp, vt, btab)
        xm = _merge(oa.reshape(T, V_W), ob.reshape(T, V_W), oc, P, xp, mp[2], wa, wb, wc, wo, l, 512 if T % 512 == 0 else MOBA_BLOCK)
        xp = channel(xm, mp, 512 if T % 512 == 0 else MOBA_BLOCK)
        outs["kp"].append(kn.reshape(bp, T, MOBA_HEADS, MOBA_HD))
        outs["vp"].append(P[:, COL_MV * MOBA_W:(COL_MV + 1) * MOBA_W].reshape(bp, T, MOBA_HEADS, MOBA_HD))
        outs["gp"].append(sg)
        outs["rp"].append(sr)

        Ps, Gs = _inproj(xs, n1_3, msm[1], msm[0], w_main, w_glr, l, ms_rows)
        Ps3, Gs3 = Ps.reshape(nseq, nd, N_MAIN), Gs.reshape(nseq, nd, 128)
        oas, sgs = _gla(Ps3, Gs3, w2p, gkb3, gla_nw3, state_gla[l], l, CHUNK, nd)
        obs, srs = _ret(Ps3, cos_s, sin_s, ret_nw3, state_ret[l], l, CHUNK, nd)
        qns, kns = _qknorm(Ps, qw3, kw3, l, ms_rows)
        vns = Ps[:, COL_MV * MOBA_W:(COL_MV + 1) * MOBA_W]
        qs4 = (qns * (MOBA_HD ** -0.5)).reshape(nseq, nd, MOBA_HEADS, MOBA_HD).transpose(0, 2, 1, 3)
        q_t = jnp.pad(qs4.transpose(0, 1, 3, 2), ((0, 0), (0, 0), (0, 0), (0, 128 - nd)))
        top = _decode_select(_decode_gates(page_table, ck_t, q_t, l))
        sel = top[:, :, :, :nd].transpose(0, 2, 3, 1).reshape(nseq, nq * MOBA_TOPK)
        head_major = lambda a: jnp.pad(a.reshape(nseq, nd, MOBA_HEADS, MOBA_HD).transpose(0, 2, 1, 3),
                                       ((0, 0), (0, 0), (0, 8 - nd), (0, 0)))
        ocs4 = _decode_attend(page_table, sel, ck_t, cv_t, qs4.reshape(nseq, nq, 1, MOBA_HD),
                              head_major(kns), head_major(vns), bt_past, bt_own, l, nd)
        ocs = ocs4.reshape(nseq, MOBA_HEADS, nd, MOBA_HD).transpose(0, 2, 1, 3)
        xms = _merge(oas.reshape(ms_rows, V_W), obs.reshape(ms_rows, V_W), ocs.reshape(ms_rows, MOBA_W).astype(BF16),
                     Ps, xs, msm[2], wa, wb, wc, wo, l, ms_rows)
        xs = channel(xms, msm, ms_rows)
        outs["ks"].append(kns.reshape(nseq, nd, MOBA_HEADS, MOBA_HD))
        outs["vs"].append(vns.reshape(nseq, nd, MOBA_HEADS, MOBA_HD))
        outs["gs"].append(sgs)
        outs["rs"].append(srs)

    st = lambda k: jnp.stack(outs[k])
    return (xp.reshape(bp, T, D_MODEL), xs.reshape(nseq, nd, D_MODEL), st("kp"), st("vp"), st("gp"), st("rp"),
            st("ks"), st("vs"), st("gs"), st("rs"))
```

```python
import functools
import math

import numpy as np
import jax
import jax.numpy as jnp
from jax import lax
from jax.experimental import pallas as pl
from jax.experimental.pallas import tpu as pltpu

F32 = jnp.float32
BF16 = jnp.bfloat16
BS = pl.BlockSpec
SDS = jax.ShapeDtypeStruct

D_MODEL = 1024
GLA_HEADS, GLA_DK, GLA_DV, GLA_LOWRANK, GLA_GATE_TEMP = 4, 64, 128, 16, 16.0
RET_HEADS, RET_DK, RET_DV = 4, 64, 128
ROPE_BASE = 10000.0
MOBA_HEADS, MOBA_HD, MOBA_BLOCK, MOBA_TOPK = 8, 64, 256, 3
N_BUCKETS, MAX_DISTANCE = 32, 128
D_FF, N_EXPERTS, TOP_K, D_FF_EXPERT = 2816, 8, 2, 1408
EPS = 1e-6
BIG = 1e30
NEG = -3.0e38
LOG2E = math.log2(math.e)

QK_W = GLA_HEADS * GLA_DK
V_W = GLA_HEADS * GLA_DV
MOBA_W = MOBA_HEADS * MOBA_HD
N_MAIN = 7680
COL_GQ, COL_GK, COL_RQ, COL_RK = 12, 13, 18, 19
COL_GV, COL_GR, COL_RV, COL_RG, COL_MQ, COL_MK, COL_MV = 7, 8, 10, 11, 12, 13, 14

VMEM_LIMIT_BYTES = 56 * 1024 * 1024
CHUNK = 128
ATTN_COLS = 256
ATTN_DENOM_ROWS = 16
ATTN_UNROLL = 8
PAGES_PER_STEP = 16
DECODE_ROWS = 8


def _cparams(*sem):
    return pltpu.CompilerParams(dimension_semantics=sem, vmem_limit_bytes=VMEM_LIMIT_BYTES)


def _dg(a, b, dims):
    return lax.dot_general(a, b, (dims, ((), ())), preferred_element_type=F32)


NN = ((1,), (0,))
NT = ((1,), (1,))
TN = ((0,), (0,))


def _split2(x):
    h = x.astype(BF16)
    m = (x - h.astype(F32)).astype(BF16)
    return h, m


def _split3(x):
    h = x.astype(BF16)
    r = x - h.astype(F32)
    m = r.astype(BF16)
    l = (r - m.astype(F32)).astype(BF16)
    return h, m, l


def _dot_x3(a, b, dims=NN):
    ah, am = _split2(a)
    bh, bm = _split2(b)
    return _dg(ah, bh, dims) + _dg(ah, bm, dims) + _dg(am, bh, dims)


def _sigmoid(x):
    return 1.0 / (1.0 + jnp.exp(-x))


def _silu(x):
    return x * _sigmoid(x)


def _mod_kernel(c_ref, w_ref, b_ref, o_ref):
    o_ref[...] = jnp.dot(c_ref[...].astype(BF16), w_ref[...].astype(BF16),
                         preferred_element_type=F32) + b_ref[...]


def _ada_mod(c_all, ada_w, ada_b3, l):
    R = c_all.shape[0]
    TN_ = 1536
    n = 6 * D_MODEL
    return pl.pallas_call(
        _mod_kernel, grid=(n // TN_,),
        in_specs=[BS((R, D_MODEL), lambda j: (0, 0)),
                  BS((None, D_MODEL, TN_), lambda j: (l, 0, j)),
                  BS((None, 1, TN_), lambda j: (l, 0, j))],
        out_specs=BS((R, TN_), lambda j: (0, j)),
        out_shape=SDS((R, n), F32),
        compiler_params=_cparams("arbitrary"))(c_all, ada_w, ada_b3)


def _modulated_norm(x, nw, sc, sh):
    ms = jnp.mean(x * x, axis=-1, keepdims=True)
    return (x * lax.rsqrt(ms + EPS) * nw) * (1.0 + sc) + sh


def _inproj_kernel(x_ref, nw_ref, sc_ref, sh_ref, w_ref, wg_ref, o_ref, og_ref, h_ref):
    @pl.when(pl.program_id(1) == 0)
    def _():
        h = _modulated_norm(x_ref[...], nw_ref[...], sc_ref[...], sh_ref[...]).astype(BF16)
        h_ref[...] = h
        og_ref[...] = jnp.dot(h, wg_ref[...], preferred_element_type=F32)

    o_ref[...] = jnp.dot(h_ref[...], w_ref[...], preferred_element_type=F32)


def _mod_spec(mod_rows, tm, ngrid):
    if mod_rows == 1:
        return BS((1, D_MODEL), (lambda i, j: (0, 0)) if ngrid == 2 else (lambda i: (0, 0)))
    return BS((tm, D_MODEL), (lambda i, j: (i, 0)) if ngrid == 2 else (lambda i: (i, 0)))


def _inproj(x, nw3, sc, sh, w_main, w_glr, l, tm):
    M = x.shape[0]
    TN_ = 1536
    ms = _mod_spec(sc.shape[0], tm, 2)
    return pl.pallas_call(
        _inproj_kernel, grid=(M // tm, N_MAIN // TN_),
        in_specs=[BS((tm, D_MODEL), lambda i, j: (i, 0)),
                  BS((None, 1, D_MODEL), lambda i, j: (l, 0, 0)),
                  ms, ms,
                  BS((None, D_MODEL, TN_), lambda i, j: (l, 0, j)),
                  BS((None, D_MODEL, 128), lambda i, j: (l, 0, 0))],
        out_specs=[BS((tm, TN_), lambda i, j: (i, j)),
                   BS((tm, 128), lambda i, j: (i, 0))],
        out_shape=[SDS((M, N_MAIN), F32), SDS((M, 128), F32)],
        scratch_shapes=[pltpu.VMEM((tm, D_MODEL), BF16)],
        compiler_params=_cparams("arbitrary", "arbitrary"))(x, nw3, sc, sh, w_main, w_glr)


def _load_rows(ref, pad_ref, rows, C):
    if rows == C:
        return ref[0]
    pad_ref[...] = jnp.zeros_like(pad_ref)
    pad_ref[0:rows, :] = ref[0]
    return pad_ref[...]


def _state_init(S_ref, s0_ref, nh, dk, dv):
    S_ref[...] = jnp.zeros_like(S_ref)
    for h in range(nh):
        S_ref[dk * h:dk * (h + 1), dv * h:dv * (h + 1)] = s0_ref[0, h]


def _state_out(so_ref, S_ref, nh, dk, dv):
    for h in range(nh):
        so_ref[0, h] = S_ref[dk * h:dk * (h + 1), dv * h:dv * (h + 1)]


def _block_diag_mask():
    r = lax.broadcasted_iota(jnp.int32, (QK_W, V_W), 0) // GLA_DK
    c = lax.broadcasted_iota(jnp.int32, (QK_W, V_W), 1) // GLA_DV
    return r == c


def _gla_kernel(q_ref, k_ref, v_ref, r_ref, g_ref, w2_ref, b_ref, nw_ref, s0_ref,
                o_ref, so_ref, S_ref, *pads, C, rows, nc):
    c = pl.program_id(1)

    @pl.when(c == 0)
    def _():
        _state_init(S_ref, s0_ref, GLA_HEADS, GLA_DK, GLA_DV)

    pads = list(pads) + [None] * 5
    q = _load_rows(q_ref, pads[0], rows, C)
    k = _load_rows(k_ref, pads[1], rows, C)
    v = _load_rows(v_ref, pads[2], rows, C)
    r = _load_rows(r_ref, pads[3], rows, C)
    g = _load_rows(g_ref, pads[4], rows, C)

    row = lax.broadcasted_iota(jnp.int32, (C, QK_W), 0)
    x = jnp.dot(g.astype(BF16), w2_ref[...].astype(BF16), preferred_element_type=F32) + b_ref[...]
    log_a = (jnp.minimum(x, 0.0) - jnp.log(1.0 + jnp.exp(-jnp.abs(x)))) * (1.0 / GLA_GATE_TEMP)
    if rows < C:
        log_a = jnp.where(row < rows, log_a, 0.0)

    ri = lax.broadcasted_iota(jnp.int32, (C, C), 0)
    ci = lax.broadcasted_iota(jnp.int32, (C, C), 1)
    causal = ri >= ci
    ltri = jnp.where(causal, 1.0, 0.0).astype(BF16)
    ones_c = jnp.ones((C, 128), BF16)
    pieces = _split3(log_a)
    b = sum(_dg(ltri, p, NN) for p in pieces)
    b_last_col = sum(_dg(p, ones_c, TN) for p in pieces)
    mid = max(min(rows, C) // 2, 1)
    b_ref_row = b[mid - 1:mid, :]
    b_last = b[C - 1:C, :]

    qs = q * (GLA_DK ** -0.5)
    qt_h, qt_m = _split2(qs * jnp.exp(b - b_ref_row))
    kt_h, kt_m = _split2(k * jnp.exp(b_ref_row - b))
    q_state = (qs * jnp.exp(b)).astype(BF16)
    k_state = (k * jnp.exp(b_last - b)).astype(BF16)
    vb = v.astype(BF16)

    S = S_ref[...]
    o_state = _dg(q_state, S.astype(BF16), NN)
    lane_head = lax.broadcasted_iota(jnp.int32, (C, QK_W), 1) // GLA_DK
    nw = nw_ref[...]
    scores = []
    for h in range(GLA_HEADS):
        mine = lane_head == h
        qh_h = jnp.where(mine, qt_h, jnp.zeros_like(qt_h))
        qh_m = jnp.where(mine, qt_m, jnp.zeros_like(qt_m))
        a = _dg(qh_h, kt_h, NT) + _dg(qh_h, kt_m, NT) + _dg(qh_m, kt_h, NT)
        scores.append(jnp.where(causal, a, 0.0).astype(BF16))
    for h in range(GLA_HEADS):
        a = scores[h]
        sl = slice(GLA_DV * h, GLA_DV * (h + 1))
        oh = _dg(a, vb[:, sl], NN) + o_state[:, sl]
        ms = jnp.mean(oh * oh, axis=-1, keepdims=True)
        y = oh * lax.rsqrt(ms + EPS) * nw[:, sl] * _silu(r[:, sl])
        o_ref[0, :, sl] = y[0:rows].astype(o_ref.dtype)

    u = _dg(k_state, vb, TN)
    e_col = jnp.exp(b_last_col)
    e_full = jnp.concatenate([e_col] * (V_W // 128), axis=1)
    S_ref[...] = e_full * S + jnp.where(_block_diag_mask(), u, 0.0)

    @pl.when(c == nc - 1)
    def _():
        _state_out(so_ref, S_ref, GLA_HEADS, GLA_DK, GLA_DV)


def _mixer_specs(rows):
    def col(width, idx):
        return BS((1, rows, width), lambda b, c: (b, c, idx))
    return col


def _gla(P3, G3, w2p, bias3, nw3, s0, l, C, rows):
    B, T, _ = P3.shape
    nc = T // rows
    col = _mixer_specs(rows)
    kern = functools.partial(_gla_kernel, C=C, rows=rows, nc=nc)
    scratch = [pltpu.VMEM((QK_W, V_W), F32)]
    if rows < C:
        scratch += [pltpu.VMEM((C, w), F32) for w in (QK_W, QK_W, V_W, V_W, 128)]
    return pl.pallas_call(
        kern, grid=(B, nc),
        in_specs=[col(QK_W, COL_GQ), col(QK_W, COL_GK), col(V_W, COL_GV), col(V_W, COL_GR),
                  BS((1, rows, 128), lambda b, c: (b, c, 0)),
                  BS((None, 128, QK_W), lambda b, c: (l, 0, 0)),
                  BS((None, 1, QK_W), lambda b, c: (l, 0, 0)),
                  BS((None, 1, V_W), lambda b, c: (l, 0, 0)),
                  BS((1, GLA_HEADS, GLA_DK, GLA_DV), lambda b, c: (b, 0, 0, 0))],
        out_specs=[BS((1, rows, V_W), lambda b, c: (b, c, 0)),
                   BS((1, GLA_HEADS, GLA_DK, GLA_DV), lambda b, c: (b, 0, 0, 0))],
        out_shape=[SDS((B, T, V_W), BF16), SDS((B, GLA_HEADS, GLA_DK, GLA_DV), F32)],
        scratch_shapes=scratch,
        compiler_params=_cparams("arbitrary", "arbitrary"))(P3, P3, P3, P3, G3, w2p, bias3, nw3, s0)


def _ret_kernel(q_ref, k_ref, v_ref, g_ref, cos_ref, sin_ref, dm_ref, rs_ref, ks_ref, cd_ref,
                nw_ref, s0_ref, o_ref, so_ref, S_ref, *pads, C, rows, nc):
    c = pl.program_id(1)

    @pl.when(c == 0)
    def _():
        _state_init(S_ref, s0_ref, RET_HEADS, RET_DK, RET_DV)

    pads = list(pads) + [None] * 4
    q = _load_rows(q_ref, pads[0], rows, C)
    k = _load_rows(k_ref, pads[1], rows, C)
    v = _load_rows(v_ref, pads[2], rows, C)
    g = _load_rows(g_ref, pads[3], rows, C)

    cos = cos_ref[...]
    sin = sin_ref[...]
    first_half = (lax.broadcasted_iota(jnp.int32, (C, QK_W), 1) % RET_DK) < (RET_DK // 2)

    def rope(x):
        partner = jnp.where(first_half, pltpu.roll(x, QK_W - RET_DK // 2, 1),
                            pltpu.roll(x, RET_DK // 2, 1))
        return x * cos + partner * sin

    qr = rope(q)
    kr = rope(k) * (RET_DK ** -0.5)
    qb = qr.astype(BF16)
    kb = kr.astype(BF16)
    vb = v.astype(BF16)
    k_state = (kr * ks_ref[...]).astype(BF16)

    S = S_ref[...]
    o_state = _dg(qb, S.astype(BF16), NN) * rs_ref[...]
    lane_head = lax.broadcasted_iota(jnp.int32, (C, QK_W), 1) // RET_DK
    nw = nw_ref[...]
    scores = [(_dg(jnp.where(lane_head == h, qb, jnp.zeros_like(qb)), kb, NT) * dm_ref[h]).astype(BF16)
              for h in range(RET_HEADS)]
    for h in range(RET_HEADS):
        a = scores[h]
        sl = slice(RET_DV * h, RET_DV * (h + 1))
        oh = _dg(a, vb[:, sl], NN) + o_state[:, sl]
        oh = oh - jnp.mean(oh, axis=-1, keepdims=True)
        var = jnp.mean(oh * oh, axis=-1, keepdims=True)
        y = oh * lax.rsqrt(var + EPS) * nw[:, sl] * _silu(g[:, sl])
        o_ref[0, :, sl] = y[0:rows].astype(o_ref.dtype)

    u = _dg(k_state, vb, TN)
    S_ref[...] = cd_ref[...] * S + jnp.where(_block_diag_mask(), u, 0.0)

    @pl.when(c == nc - 1)
    def _():
        _state_out(so_ref, S_ref, RET_HEADS, RET_DK, RET_DV)


def _ret_constants(C, rows):
    log_gamma = np.log(1.0 - 2.0 ** (-5.0 - np.arange(RET_HEADS, dtype=np.float64)))
    bt = np.minimum(np.arange(C) + 1, rows).astype(np.float64)
    diff = bt[:, None] - bt[None, :]
    causal = np.arange(C)[:, None] >= np.arange(C)[None, :]
    dm = np.where(causal[None], np.exp(diff[None] * log_gamma[:, None, None]), 0.0)
    rs = np.exp(bt[:, None] * np.repeat(log_gamma, RET_DV)[None, :])
    ks = np.exp((bt[-1] - bt)[:, None] * np.repeat(log_gamma, RET_DK)[None, :])
    cd = np.exp(bt[-1] * np.repeat(log_gamma, RET_DV))[None, :]
    return (jnp.asarray(dm, F32), jnp.asarray(rs, F32), jnp.asarray(ks, F32), jnp.asarray(cd, F32))


def _ret(P3, cos_t, sin_t, nw3, s0, l, C, rows):
    B, T, _ = P3.shape
    nc = T // rows
    col = _mixer_specs(rows)
    dm, rs, ks, cd = _ret_constants(C, rows)
    kern = functools.partial(_ret_kernel, C=C, rows=rows, nc=nc)
    scratch = [pltpu.VMEM((QK_W, V_W), F32)]
    if rows < C:
        scratch += [pltpu.VMEM((C, w), F32) for w in (QK_W, QK_W, V_W, V_W)]
    const = lambda shape: BS(shape, lambda b, c: (0,) * len(shape))
    return pl.pallas_call(
        kern, grid=(B, nc),
        in_specs=[col(QK_W, COL_RQ), col(QK_W, COL_RK), col(V_W, COL_RV), col(V_W, COL_RG),
                  BS((C, QK_W), lambda b, c: (c, 0)), BS((C, QK_W), lambda b, c: (c, 0)),
                  const((RET_HEADS, C, C)), const((C, V_W)), const((C, QK_W)), const((1, V_W)),
                  BS((None, 1, V_W), lambda b, c: (l, 0, 0)),
                  BS((1, RET_HEADS, RET_DK, RET_DV), lambda b, c: (b, 0, 0, 0))],
        out_specs=[BS((1, rows, V_W), lambda b, c: (b, c, 0)),
                   BS((1, RET_HEADS, RET_DK, RET_DV), lambda b, c: (b, 0, 0, 0))],
        out_shape=[SDS((B, T, V_W), BF16), SDS((B, RET_HEADS, RET_DK, RET_DV), F32)],
        scratch_shapes=scratch,
        compiler_params=_cparams("arbitrary", "arbitrary"))(
            P3, P3, P3, P3, cos_t, sin_t, dm, rs, ks, cd, nw3, s0)


def _rope_tables(pos0, n):
    half = RET_DK // 2
    inv = ROPE_BASE ** (-jnp.arange(half, dtype=F32) / half)
    pos = (pos0 + jnp.arange(n, dtype=jnp.int32)).astype(F32)
    ang = pos[:, None] * inv[None, :]
    cos, sin = jnp.cos(ang), jnp.sin(ang)
    cos_t = jnp.tile(jnp.concatenate([cos, cos], -1), (1, RET_HEADS))
    sin_t = jnp.tile(jnp.concatenate([-sin, sin], -1), (1, RET_HEADS))
    return cos_t, sin_t


def _head_rms64(x, g_ref, w):
    xh, xm = _split2(x * x)
    ms = jnp.dot(xh, g_ref[...], preferred_element_type=F32) + jnp.dot(xm, g_ref[...], preferred_element_type=F32)
    return x * lax.rsqrt(ms + EPS) * w


def _qknorm_kernel(q_ref, k_ref, g_ref, qw_ref, kw_ref, qn_ref, kn_ref):
    qn_ref[...] = _head_rms64(q_ref[...], g_ref, qw_ref[...])
    kn_ref[...] = _head_rms64(k_ref[...], g_ref, kw_ref[...])


def _prompt_prep_kernel(q_ref, k_ref, v_ref, g_ref, pl_ref, eye_ref, qw_ref, kw_ref,
                        qn_ref, kn_ref, km_ref, kp_ref, vt_ref):
    i = pl.program_id(0)
    qn_ref[...] = _head_rms64(q_ref[...], g_ref, qw_ref[...])
    kn = _head_rms64(k_ref[...], g_ref, kw_ref[...])
    kn_ref[...] = kn
    km_ref[0] = jnp.mean(kn, axis=0, keepdims=True)
    lane = lax.broadcasted_iota(jnp.int32, (MOBA_BLOCK, 2 * MOBA_W), 1) % 128
    placed = jnp.dot(kn.astype(BF16), pl_ref[...], preferred_element_type=F32)
    kp_ref[...] = (placed + jnp.where(lane == MOBA_HD + i, BIG, 0.0)).astype(BF16)
    vt_ref[...] = _dg(eye_ref[...], v_ref[...].astype(BF16), NT).astype(BF16)


def _group_mean_matrix():
    r = np.arange(MOBA_W)
    return jnp.asarray((r[:, None] // MOBA_HD == r[None, :] // MOBA_HD) / MOBA_HD, BF16)


def _placement_matrix():
    r = np.arange(MOBA_W)
    c = np.arange(2 * MOBA_W)
    tgt = (r // MOBA_HD) * 128 + r % MOBA_HD
    return jnp.asarray(tgt[:, None] == c[None, :], BF16)


def _qknorm(P, qw3, kw3, l, tm):
    M = P.shape[0]
    return pl.pallas_call(
        _qknorm_kernel, grid=(M // tm,),
        in_specs=[BS((tm, MOBA_W), lambda i: (i, COL_MQ)), BS((tm, MOBA_W), lambda i: (i, COL_MK)),
                  BS((MOBA_W, MOBA_W), lambda i: (0, 0)),
                  BS((None, 1, MOBA_W), lambda i: (l, 0, 0)), BS((None, 1, MOBA_W), lambda i: (l, 0, 0))],
        out_specs=[BS((tm, MOBA_W), lambda i: (i, 0)), BS((tm, MOBA_W), lambda i: (i, 0))],
        out_shape=[SDS((M, MOBA_W), F32), SDS((M, MOBA_W), F32)],
        compiler_params=_cparams("arbitrary"))(P, P, _group_mean_matrix(), qw3, kw3)


def _prompt_prep(P, qw3, kw3, l):
    T = P.shape[0]
    nb = T // MOBA_BLOCK
    tm = MOBA_BLOCK
    row = lambda w, idx: BS((tm, w), lambda i: (i, idx))
    return pl.pallas_call(
        _prompt_prep_kernel, grid=(nb,),
        in_specs=[row(MOBA_W, COL_MQ), row(MOBA_W, COL_MK), row(MOBA_W, COL_MV),
                  BS((MOBA_W, MOBA_W), lambda i: (0, 0)),
                  BS((MOBA_W, 2 * MOBA_W), lambda i: (0, 0)),
                  BS((MOBA_W, MOBA_W), lambda i: (0, 0)),
                  BS((None, 1, MOBA_W), lambda i: (l, 0, 0)), BS((None, 1, MOBA_W), lambda i: (l, 0, 0))],
        out_specs=[row(MOBA_W, 0), row(MOBA_W, 0), BS((1, 1, MOBA_W), lambda i: (i, 0, 0)),
                   row(2 * MOBA_W, 0), BS((MOBA_W, tm), lambda i: (0, i))],
        out_shape=[SDS((T, MOBA_W), F32), SDS((T, MOBA_W), F32), SDS((nb, 1, MOBA_W), F32),
                   SDS((T, 2 * MOBA_W), BF16), SDS((MOBA_W, T), BF16)],
        compiler_params=_cparams("arbitrary"))(
            P, P, P, _group_mean_matrix(), _placement_matrix(), jnp.eye(MOBA_W, dtype=BF16), qw3, kw3)


def _placed_block_means(kmean):
    nb = kmean.shape[0]
    km = kmean.reshape(nb, MOBA_HEADS, MOBA_HD).transpose(1, 2, 0)
    km = jnp.pad(km, ((0, 0), (0, 0), (MOBA_HD, MOBA_HD - nb)))
    full = km[:, :, None, :] * jnp.eye(MOBA_HEADS, dtype=F32)[:, None, :, None]
    return full.reshape(MOBA_W, 2 * MOBA_W)


def _select_top3(gate, valid, axis=-1):
    g1 = jnp.where(valid, gate, NEG)
    m1 = jnp.max(g1, axis=axis, keepdims=True)
    g2 = jnp.where(g1 >= m1, NEG, g1)
    m2 = jnp.max(g2, axis=axis, keepdims=True)
    g3 = jnp.where(g2 >= m2, NEG, g2)
    m3 = jnp.max(g3, axis=axis, keepdims=True)
    return valid & (g1 >= m3)


def _gate_kernel(qn_ref, kmt_ref, plt_ref, qpt_ref):
    i = pl.program_id(0)
    qn = qn_ref[...]
    gate = _dg(kmt_ref[...].astype(BF16), qn.astype(BF16), NT)
    qpart = _dg(plt_ref[...], (qn * (MOBA_HD ** -0.5 * LOG2E)).astype(BF16), NT)
    blk = lax.broadcasted_iota(jnp.int32, (128, MOBA_BLOCK), 0) - MOBA_HD
    valid = (blk >= 0) & (blk < i)
    for h in range(MOBA_HEADS):
        sl = slice(128 * h, 128 * (h + 1))
        sel = _select_top3(gate[sl], valid, axis=0) | (blk == i)
        maskpart = jnp.where((blk >= 0) & jnp.logical_not(sel), -1.0, 0.0)
        qpt_ref[sl, :] = (qpart[sl] + maskpart).astype(BF16)


def _gate(qn, km_placed_t):
    T = qn.shape[0]
    nb = T // MOBA_BLOCK
    return pl.pallas_call(
        _gate_kernel, name="moba_gate", grid=(nb,),
        in_specs=[BS((MOBA_BLOCK, MOBA_W), lambda i: (i, 0)),
                  BS((2 * MOBA_W, MOBA_W), lambda i: (0, 0)),
                  BS((2 * MOBA_W, MOBA_W), lambda i: (0, 0))],
        out_specs=BS((2 * MOBA_W, MOBA_BLOCK), lambda i: (0, i)),
        out_shape=SDS((2 * MOBA_W, T), BF16),
        compiler_params=_cparams("arbitrary"))(qn, km_placed_t, _placement_matrix().T)


def _bucket_np(dist):
    dist = np.maximum(dist, 0)
    max_exact = N_BUCKETS // 2
    large = max_exact + (np.log(np.maximum(dist, max_exact).astype(np.float64) / max_exact)
                         / math.log(MAX_DISTANCE / max_exact) * (N_BUCKETS - max_exact)).astype(np.int64)
    large = np.minimum(large, N_BUCKETS - 1)
    return np.where(dist < max_exact, dist, large).astype(np.int32)


def _bias_kernel(rb_ref, idx_ref, ok_ref, o_ref, *, scale):
    h = pl.program_id(0)
    idx = idx_ref[...]
    far = rb_ref[N_BUCKETS - 1, h]
    acc = jnp.zeros(idx.shape, F32)
    for b in range(N_BUCKETS - 1):
        acc = jnp.where(idx == b, (rb_ref[b, h] - far) * scale, acc)
    o_ref[0] = jnp.where(ok_ref[...] > 0, acc, -BIG)


def _bias_tables(rel_bias, dist, scale=1.0):
    idx = jnp.asarray(_bucket_np(dist))
    ok = jnp.asarray((dist >= 0).astype(np.int32))
    shp = dist.shape
    zeros = (0,) * len(shp)
    return pl.pallas_call(
        functools.partial(_bias_kernel, scale=scale), name="moba_bias_tables", grid=(MOBA_HEADS,),
        in_specs=[BS(memory_space=pltpu.SMEM), BS(shp, lambda h: zeros), BS(shp, lambda h: zeros)],
        out_specs=BS((1,) + shp, lambda h: (h,) + zeros),
        out_shape=SDS((MOBA_HEADS,) + shp, F32),
        compiler_params=_cparams("arbitrary"))(rel_bias, idx, ok)


def _attn_kernel(qpt_ref, kp_ref, vt_ref, bt_ref, eye_ref, o_ref, m_ref, acc_ref, s_ref):
    i = pl.program_id(1)
    tq = MOBA_BLOCK
    m_ref[...] = jnp.full(m_ref.shape, NEG, F32)
    acc_ref[...] = jnp.zeros_like(acc_ref)

    def logits(slot, hh, n, t):
        qt = qpt_ref[128 * hh:128 * (hh + 1), :]
        start = pl.multiple_of(n * MOBA_BLOCK, MOBA_BLOCK)
        kblk = kp_ref[pl.ds(start, MOBA_BLOCK), 128 * hh:128 * (hh + 1)]
        s = jnp.dot(kblk, qt, preferred_element_type=F32)
        if t is not None:
            s = s + bt_ref[hh, t]
        s_ref[slot] = s

    def softmax_pv(slot, hh, n):
        ps, alphas = [], []
        for c in range(tq // ATTN_COLS):
            cols = slice(ATTN_COLS * c, ATTN_COLS * (c + 1))
            sc = s_ref[slot, :, cols]
            m_old = m_ref[hh, :, cols]
            m_new = jnp.maximum(m_old, jnp.max(sc, axis=0, keepdims=True))
            alpha = jnp.exp2(m_old - m_new)
            p = jnp.exp2(sc - m_new)
            m_ref[hh, :, cols] = m_new
            ps.append(p.astype(BF16))
            alphas.append(alpha)
        start = pl.multiple_of(n * MOBA_BLOCK, MOBA_BLOCK)
        vt = vt_ref[MOBA_HD * hh:MOBA_HD * (hh + 1), pl.ds(start, MOBA_BLOCK)]
        vt_ones = jnp.concatenate([vt, jnp.ones((ATTN_DENOM_ROWS, MOBA_BLOCK), BF16)], axis=0)
        pv = jnp.dot(vt_ones, jnp.concatenate(ps, axis=1), preferred_element_type=F32)
        acc_ref[hh] = jnp.concatenate(alphas, axis=1) * acc_ref[hh] + pv

    def group(tiles):
        for slot, (hh, n, t) in enumerate(tiles):
            logits(slot, hh, n, t)
        for slot, (hh, n, t) in enumerate(tiles):
            softmax_pv(slot, hh, n)

    n_far = jnp.maximum(i - 1, 0)

    def far_tiles(first, count):
        return [(hh, first + b, None) for b in range(count) for hh in range(2)]

    def far_body(j, carry):
        group(far_tiles(ATTN_UNROLL * j, ATTN_UNROLL))
        return carry

    lax.fori_loop(0, n_far // ATTN_UNROLL, far_body, 0)
    done = (n_far // ATTN_UNROLL) * ATTN_UNROLL
    width = ATTN_UNROLL // 2
    while width >= 1:
        @pl.when((n_far - done) % (2 * width) >= width)
        def _(done=done, width=width):
            group(far_tiles(done, width))
        done = done + jnp.where((n_far - done) % (2 * width) >= width, width, 0)
        width //= 2

    @pl.when(i >= 1)
    def _():
        group([(0, i - 1, 1), (1, i - 1, 1), (0, i, 0), (1, i, 0)])

    @pl.when(i == 0)
    def _():
        group([(0, i, 0), (1, i, 0)])

    out_t = jnp.concatenate([acc_ref[hh, 0:MOBA_HD] / acc_ref[hh, MOBA_HD:MOBA_HD + 1] for hh in range(2)],
                            axis=0).astype(BF16)
    o_ref[...] = _dg(out_t, eye_ref[...], TN).astype(o_ref.dtype)


def _prompt_attention(qpt, kp, vt, btab_t):
    T = kp.shape[0]
    nb = T // MOBA_BLOCK
    return pl.pallas_call(
        _attn_kernel, name="moba_prompt_attention", grid=(MOBA_HEADS // 2, nb),
        in_specs=[BS((256, MOBA_BLOCK), lambda hp, i: (hp, i)),
                  BS((T, 256), lambda hp, i: (0, hp)),
                  BS((128, T), lambda hp, i: (hp, 0)),
                  BS((2, 2, MOBA_BLOCK, MOBA_BLOCK), lambda hp, i: (hp, 0, 0, 0)),
                  BS((128, 128), lambda hp, i: (0, 0))],
        out_specs=BS((MOBA_BLOCK, 128), lambda hp, i: (i, hp)),
        out_shape=SDS((T, MOBA_W), BF16),
        scratch_shapes=[pltpu.VMEM((2, 1, MOBA_BLOCK), F32),
                        pltpu.VMEM((2, MOBA_HD + ATTN_DENOM_ROWS, MOBA_BLOCK), F32),
                        pltpu.VMEM((2 * ATTN_UNROLL, MOBA_BLOCK, MOBA_BLOCK), F32)],
        compiler_params=_cparams("arbitrary", "arbitrary"))(qpt, kp, vt, btab_t, jnp.eye(128, dtype=BF16))


def _decode_gate_kernel(pt_ref, *refs):
    npg = PAGES_PER_STEP
    pages, (q_ref, g_ref) = refs[:npg], refs[npg:]
    qb = _bf16_round(q_ref[0])
    pages_per_block = MOBA_BLOCK // pages[0].shape[-1]
    for blk in range(npg // pages_per_block):
        tokens = pages[pages_per_block * blk][...]
        for g in range(1, pages_per_block):
            tokens = tokens + pages[pages_per_block * blk + g][...]
        km = jnp.sum(tokens, axis=-1, keepdims=True) * (1.0 / MOBA_BLOCK)
        g_ref[0, blk] = jnp.sum(_bf16_round(km) * qb, axis=1)


def _decode_gates(page_table, cache_t, q_t, l):
    nseq, n_pages = page_table.shape
    page = cache_t.shape[-1]
    npg = PAGES_PER_STEP
    ppb = MOBA_BLOCK // page
    page_specs = [BS((None, None, MOBA_HEADS, MOBA_HD, page), functools.partial(
        lambda b, g, pt, r: (l, pt[b, g * npg + r], 0, 0, 0), r=r)) for r in range(npg)]
    grid_spec = pltpu.PrefetchScalarGridSpec(
        num_scalar_prefetch=1, grid=(nseq, n_pages // npg),
        in_specs=page_specs + [BS((1, MOBA_HEADS, MOBA_HD, 128), lambda b, g, pt: (b, 0, 0, 0))],
        out_specs=BS((1, npg // ppb, MOBA_HEADS, 128), lambda b, g, pt: (b, g, 0, 0)))
    return pl.pallas_call(
        _decode_gate_kernel, name="decode_gates", grid_spec=grid_spec,
        out_shape=SDS((nseq, n_pages // ppb, MOBA_HEADS, 128), F32),
        compiler_params=_cparams("arbitrary", "arbitrary"))(page_table, *([cache_t] * npg), q_t)


def _decode_select_kernel(g_ref, o_ref):
    gate = g_ref[0]
    nblk = gate.shape[0]
    blk = lax.broadcasted_iota(jnp.int32, gate.shape, 0)
    for t in range(MOBA_TOPK):
        m = jnp.max(gate, axis=0, keepdims=True)
        idx = jnp.min(jnp.where(gate >= m, blk, nblk), axis=0, keepdims=True)
        o_ref[0, t] = idx[0]
        gate = jnp.where(blk == idx, NEG, gate)


def _decode_select(gates):
    nseq, nblk = gates.shape[:2]
    return pl.pallas_call(
        _decode_select_kernel, name="decode_select", grid=(nseq,),
        in_specs=[BS((1, nblk, MOBA_HEADS, 128), lambda b: (b, 0, 0, 0))],
        out_specs=BS((1, MOBA_TOPK, MOBA_HEADS, 128), lambda b: (b, 0, 0, 0)),
        out_shape=SDS((nseq, MOBA_TOPK, MOBA_HEADS, 128), jnp.int32),
        compiler_params=_cparams("arbitrary"))(gates)


def _bf16_round(x):
    return x.astype(BF16).astype(F32)


def _decode_attend_kernel(pt_ref, sel_ref, ck_ref, cv_ref, q_ref, kn_ref, vn_ref, bpast_ref, bown_ref, o_ref,
                          kbuf, vbuf, sem, *, l, nd, nblk, page):
    b = pl.program_id(0)
    rows = MOBA_HEADS * nd
    trips = rows // DECODE_ROWS
    pages_per_block = MOBA_BLOCK // page

    def copies(i, slot):
        out = []
        for u in range(DECODE_ROWS):
            r = DECODE_ROWS * i + u
            h = r // nd
            for t in range(MOBA_TOPK):
                n = sel_ref[b, MOBA_TOPK * r + t]
                for g in range(pages_per_block):
                    pg = pt_ref[b, pages_per_block * n + g]
                    dst = slice(page * g, page * (g + 1))
                    out.append(pltpu.make_async_copy(ck_ref.at[l, pg, h], kbuf.at[slot, u, t, :, dst],
                                                     sem.at[slot, 0, u, t, g]))
                    out.append(pltpu.make_async_copy(cv_ref.at[l, pg, h], vbuf.at[slot, u, t, :, dst],
                                                     sem.at[slot, 1, u, t, g]))
        return out

    for c in copies(0, 0):
        c.start()

    def body(i, carry):
        slot = i % 2

        @pl.when(i + 1 < trips)
        def _():
            for c in copies(i + 1, 1 - slot):
                c.start()

        for c in copies(i, slot):
            c.wait()

        steps = [attend_row(DECODE_ROWS * i + u, kbuf.at[slot, u], vbuf.at[slot, u]) for u in range(DECODE_ROWS)]
        for _ in range(3):
            for st in steps:
                next(st, None)
        return carry

    def attend_row(r, kbuf, vbuf):
        h = r // nd
        qv = q_ref[0, r]
        qmat = jnp.broadcast_to(qv, (8, MOBA_HD)).astype(BF16)
        logits = []
        for t in range(MOBA_TOPK):
            s = jnp.dot(qmat, kbuf[t].astype(BF16), preferred_element_type=F32)
            is_last = sel_ref[b, MOBA_TOPK * r + t] == nblk - 1
            logits.append(s + jnp.where(is_last, bpast_ref[pl.ds(r, 1), :], 0.0))
        yield
        s_own = jnp.sum(_bf16_round(kn_ref[0, h]) * _bf16_round(qv), axis=-1, keepdims=True) + bown_ref[r]
        m = jnp.max(s_own, axis=0, keepdims=True)
        for s in logits:
            m = jnp.maximum(m, jnp.max(s[0:1], axis=-1, keepdims=True))
        p_own = jnp.exp(s_own - m)
        denom = jnp.sum(p_own, axis=0, keepdims=True)
        probs = [jnp.exp(s - m) for s in logits]
        for p in probs:
            denom = denom + jnp.sum(p[0:1], axis=-1, keepdims=True)
        inv = 1.0 / denom
        yield
        acc = jnp.zeros((8, MOBA_HD), F32)
        for t in range(MOBA_TOPK):
            acc = acc + _dg((probs[t] * inv).astype(BF16), vbuf[t].astype(BF16), NT)
        own = jnp.sum(_bf16_round(p_own * inv) * _bf16_round(vn_ref[0, h]), axis=0, keepdims=True)
        o_ref[0, r] = acc[0:1] + own

    lax.fori_loop(0, trips, body, 0)


def _decode_attend(page_table, sel, cache_k, cache_v, q_rows, kn_new, vn_new, b_past, b_own, l, nd):
    nseq, n_pages = page_table.shape
    page = cache_k.shape[-1]
    rows = q_rows.shape[1]
    per_seq = lambda shape: BS((1,) + shape, lambda b, pt, sl: (b,) + (0,) * len(shape))
    const = lambda shape: BS(shape, lambda b, pt, sl: (0,) * len(shape))
    assert rows % DECODE_ROWS == 0
    slab = pltpu.VMEM((2, DECODE_ROWS, MOBA_TOPK, MOBA_HD, MOBA_BLOCK), F32)
    grid_spec = pltpu.PrefetchScalarGridSpec(
        num_scalar_prefetch=2, grid=(nseq,),
        in_specs=[BS(memory_space=pl.ANY), BS(memory_space=pl.ANY),
                  per_seq((rows, 1, MOBA_HD)), per_seq((MOBA_HEADS, 8, MOBA_HD)), per_seq((MOBA_HEADS, 8, MOBA_HD)),
                  const((rows, MOBA_BLOCK)), const((rows, 8, 1))],
        out_specs=per_seq((rows, 1, MOBA_HD)),
        scratch_shapes=[slab, slab, pltpu.SemaphoreType.DMA((2, 2, DECODE_ROWS, MOBA_TOPK, MOBA_BLOCK // page))])
    kern = functools.partial(_decode_attend_kernel, l=l, nd=nd, nblk=n_pages * page // MOBA_BLOCK, page=page)
    return pl.pallas_call(
        kern, name="decode_attend", grid_spec=grid_spec,
        out_shape=SDS((nseq, rows, 1, MOBA_HD), F32),
        compiler_params=_cparams("arbitrary"))(page_table, sel, cache_k, cache_v, q_rows, kn_new, vn_new, b_past, b_own)


def _merge_kernel(oa_ref, ob_ref, oc_ref, mg_ref, x_ref, g1_ref, wa_ref, wb_ref, wc_ref, wo_ref, o_ref):
    mg = mg_ref[...]
    ya = jnp.dot(oa_ref[...], wa_ref[...], preferred_element_type=F32)
    yb = jnp.dot(ob_ref[...], wb_ref[...], preferred_element_type=F32)
    yc = jnp.dot(oc_ref[...], wc_ref[...], preferred_element_type=F32)
    merged = (_sigmoid(mg[:, 0:D_MODEL]) * ya + _sigmoid(mg[:, D_MODEL:2 * D_MODEL]) * yb
              + _sigmoid(mg[:, 2 * D_MODEL:]) * yc)
    mix = jnp.dot(merged.astype(BF16), wo_ref[...], preferred_element_type=F32)
    o_ref[...] = x_ref[...] + g1_ref[...] * mix


def _merge(oa, ob, oc, P, x, g1, wa, wb, wc, wo, l, tm):
    M = x.shape[0]
    row = lambda w: BS((tm, w), lambda i: (i, 0))
    wspec = lambda k: BS((None, k, D_MODEL), lambda i: (l, 0, 0))
    return pl.pallas_call(
        _merge_kernel, grid=(M // tm,),
        in_specs=[row(V_W), row(V_W), row(MOBA_W), row(3 * D_MODEL), row(D_MODEL),
                  _mod_spec(g1.shape[0], tm, 1),
                  wspec(V_W), wspec(V_W), wspec(MOBA_W), wspec(D_MODEL)],
        out_specs=row(D_MODEL),
        out_shape=SDS((M, D_MODEL), F32),
        compiler_params=_cparams("arbitrary"))(oa, ob, oc, P, x, g1, wa, wb, wc, wo)


def _ffn_kernel(x_ref, nw_ref, sc_ref, sh_ref, g2_ref, w1_ref, w3_ref, w2_ref, o_ref, h_ref, acc_ref):
    j = pl.program_id(1)

    @pl.when(j == 0)
    def _():
        h_ref[...] = _modulated_norm(x_ref[...], nw_ref[...], sc_ref[...], sh_ref[...]).astype(BF16)
        acc_ref[...] = jnp.zeros_like(acc_ref)

    h = h_ref[...]
    a = jnp.dot(h, w1_ref[...], preferred_element_type=F32)
    b = jnp.dot(h, w3_ref[...], preferred_element_type=F32)
    acc_ref[...] += jnp.dot((_silu(a) * b).astype(BF16), w2_ref[...], preferred_element_type=F32)

    @pl.when(j == pl.num_programs(1) - 1)
    def _():
        o_ref[...] = x_ref[...] + g2_ref[...] * acc_ref[...]


def _ffn(x, nw3, sc, sh, g2, w1, w3, w2, l, li, tm):
    M = x.shape[0]
    tf = D_FF // 2
    ms = _mod_spec(sc.shape[0], tm, 2)
    return pl.pallas_call(
        _ffn_kernel, grid=(M // tm, D_FF // tf),
        in_specs=[BS((tm, D_MODEL), lambda i, j: (i, 0)),
                  BS((None, 1, D_MODEL), lambda i, j: (l, 0, 0)), ms, ms, ms,
                  BS((None, D_MODEL, tf), lambda i, j: (li, 0, j)),
                  BS((None, D_MODEL, tf), lambda i, j: (li, 0, j)),
                  BS((None, tf, D_MODEL), lambda i, j: (li, j, 0))],
        out_specs=BS((tm, D_MODEL), lambda i, j: (i, 0)),
        out_shape=SDS((M, D_MODEL), F32),
        scratch_shapes=[pltpu.VMEM((tm, D_MODEL), BF16), pltpu.VMEM((tm, D_MODEL), F32)],
        compiler_params=_cparams("arbitrary", "arbitrary"))(x, nw3, sc, sh, g2, w1, w3, w2)


def _moe_kernel(x_ref, nw_ref, sc_ref, sh_ref, g2_ref, wr_ref, w1_ref, w3_ref, w2_ref, o_ref,
                h_ref, gate_ref, acc_ref):
    e = pl.program_id(1)
    tm = x_ref.shape[0]
    lane = lax.broadcasted_iota(jnp.int32, (tm, 128), 1)

    @pl.when(e == 0)
    def _():
        h = _modulated_norm(x_ref[...], nw_ref[...], sc_ref[...], sh_ref[...])
        h_ref[...] = h.astype(BF16)
        router = jnp.dot(h.astype(BF16), wr_ref[...].astype(BF16), preferred_element_type=F32)
        logits = jnp.where(lane < N_EXPERTS, router, NEG)
        m1 = jnp.max(logits, axis=-1, keepdims=True)
        i1 = jnp.min(jnp.where(logits >= m1, lane, 128), axis=-1, keepdims=True)
        rest = jnp.where(lane == i1, NEG, logits)
        m2 = jnp.max(rest, axis=-1, keepdims=True)
        i2 = jnp.min(jnp.where(rest >= m2, lane, 128), axis=-1, keepdims=True)
        e2 = jnp.exp(m2 - m1)
        w_first = 1.0 / (1.0 + e2)
        gate_ref[...] = jnp.where(lane == i1, w_first, 0.0) + jnp.where(lane == i2, e2 * w_first, 0.0)
        acc_ref[...] = jnp.zeros_like(acc_ref)

    h = h_ref[...]
    a = jnp.dot(h, w1_ref[...], preferred_element_type=F32)
    b = jnp.dot(h, w3_ref[...], preferred_element_type=F32)
    y = jnp.dot((_silu(a) * b).astype(BF16), w2_ref[...], preferred_element_type=F32)
    ge = jnp.sum(jnp.where(lane == e, gate_ref[...], 0.0), axis=-1, keepdims=True)
    acc_ref[...] += ge * y

    @pl.when(e == pl.num_programs(1) - 1)
    def _():
        o_ref[...] = x_ref[...] + g2_ref[...] * acc_ref[...]


def _moe(x, nw3, sc, sh, g2, wr, w1, w3, w2, l, li, tm):
    M = x.shape[0]
    ms = _mod_spec(sc.shape[0], tm, 2)
    return pl.pallas_call(
        _moe_kernel, grid=(M // tm, N_EXPERTS),
        in_specs=[BS((tm, D_MODEL), lambda i, e: (i, 0)),
                  BS((None, 1, D_MODEL), lambda i, e: (l, 0, 0)), ms, ms, ms,
                  BS((None, D_MODEL, 128), lambda i, e: (li, 0, 0)),
                  BS((None, None, D_MODEL, D_FF_EXPERT), lambda i, e: (li, e, 0, 0)),
                  BS((None, None, D_MODEL, D_FF_EXPERT), lambda i, e: (li, e, 0, 0)),
                  BS((None, None, D_FF_EXPERT, D_MODEL), lambda i, e: (li, e, 0, 0))],
        out_specs=BS((tm, D_MODEL), lambda i, e: (i, 0)),
        out_shape=SDS((M, D_MODEL), F32),
        scratch_shapes=[pltpu.VMEM((tm, D_MODEL), BF16), pltpu.VMEM((tm, 128), F32),
                        pltpu.VMEM((tm, D_MODEL), F32)],
        compiler_params=_cparams("arbitrary", "arbitrary"))(x, nw3, sc, sh, g2, wr, w1, w3, w2)


def kernel(x_prompt, x_sample, cache_k, cache_v, state_gla, state_ret, page_table, c_prompt, c_sample,
           ada_w, ada_b, norm1, norm2, w_in, gla_gk_w2, gla_gk_b, gla_norm, ret_norm, moba_qnorm,
           moba_knorm, rel_bias, w_br_gla, w_br_ret, w_br_moba, w_out, ffn_w1, ffn_w3, ffn_w2,
           moe_router, moe_w1, moe_w3, moe_w2):
    depth = w_in.shape[0]
    bp, T, _ = x_prompt.shape
    nseq, nd, _ = x_sample.shape
    n_pages, page = page_table.shape[1], cache_k.shape[2]
    past_len = n_pages * page
    nb = T // MOBA_BLOCK
    assert bp == 1 and T % MOBA_BLOCK == 0 and nb <= MOBA_HD and page == 128
    assert past_len % MOBA_BLOCK == 0 and n_pages % PAGES_PER_STEP == 0 and nd <= 8
    ms_rows = nseq * nd
    nq = nd * MOBA_HEADS

    sizes = np.cumsum([QK_W, QK_W, V_W, GLA_LOWRANK, V_W, QK_W, QK_W, V_W, V_W, MOBA_W, MOBA_W, MOBA_W])
    glr0, glr1, mg0 = int(sizes[2]), int(sizes[3]), int(sizes[-1])
    w_main = jnp.concatenate([w_in[:, :, mg0:], w_in[:, :, :glr0], w_in[:, :, glr1:mg0]], axis=2).astype(BF16)
    w_glr = jnp.pad(w_in[:, :, glr0:glr1], ((0, 0), (0, 0), (0, 128 - GLA_LOWRANK))).astype(BF16)
    w2p = jnp.pad(gla_gk_w2, ((0, 0), (0, 128 - GLA_LOWRANK), (0, 0)))
    gkb3 = gla_gk_b[:, None, :]
    gla_nw3 = jnp.tile(gla_norm, (1, GLA_HEADS))[:, None, :]
    ret_nw3 = jnp.tile(ret_norm, (1, RET_HEADS))[:, None, :]
    qw3 = jnp.tile(moba_qnorm, (1, MOBA_HEADS))[:, None, :]
    kw3 = jnp.tile(moba_knorm, (1, MOBA_HEADS))[:, None, :]
    n1_3, n2_3 = norm1[:, None, :], norm2[:, None, :]
    ada_b3 = ada_b[:, None, :]
    wa, wb, wc, wo = (w.astype(BF16) for w in (w_br_gla, w_br_ret, w_br_moba, w_out))
    f1, f3, f2 = ffn_w1.astype(BF16), ffn_w3.astype(BF16), ffn_w2.astype(BF16)
    e1, e3, e2 = moe_w1.astype(BF16), moe_w3.astype(BF16), moe_w2.astype(BF16)
    wr = jnp.pad(moe_router, ((0, 0), (0, 0), (0, 128 - N_EXPERTS)))

    cos_p, sin_p = _rope_tables(0, T)
    cos_s, sin_s = _rope_tables(past_len, CHUNK)
    tq = np.arange(MOBA_BLOCK)
    d_own = tq[:, None] - tq[None, :]
    btab = _bias_tables(rel_bias, np.stack([d_own.T, d_own.T + MOBA_BLOCK]), LOG2E)
    qi = np.arange(8)
    d_past = (MOBA_BLOCK + qi[:, None] - tq[None, :])
    d_new = np.where((qi[None, :] < nd) & (qi[:, None] < nd), qi[:, None] - qi[None, :], -1)
    bt_past = _bias_tables(rel_bias, d_past)[:, :nd].reshape(nq, MOBA_BLOCK)
    bt_own = _bias_tables(rel_bias, np.pad(d_new, ((0, 0), (0, 120)), constant_values=-1))
    bt_own = bt_own[:, :nd, :8].reshape(nq, 8, 1)
    ck_t = cache_k.transpose(0, 1, 3, 4, 2)
    cv_t = cache_v.transpose(0, 1, 3, 4, 2)

    c_all = jnp.concatenate([c_prompt, c_sample, jnp.zeros((-(bp + nseq) % 8, D_MODEL), F32)], axis=0)
    zero_gla = jnp.zeros((bp, GLA_HEADS, GLA_DK, GLA_DV), F32)
    zero_ret = jnp.zeros((bp, RET_HEADS, RET_DK, RET_DV), F32)

    xp = x_prompt.reshape(T, D_MODEL)
    xs = x_sample.reshape(ms_rows, D_MODEL)
    outs = {k: [] for k in ("kp", "vp", "gp", "rp", "ks", "vs", "gs", "rs")}
    for l in range(depth):
        mod = _ada_mod(c_all, ada_w, ada_b3, l)
        mp = [mod[0:1, j * D_MODEL:(j + 1) * D_MODEL] for j in range(6)]
        msm = [jnp.repeat(mod[bp:bp + nseq, j * D_MODEL:(j + 1) * D_MODEL], nd, axis=0) for j in range(6)]
        li = l // 2

        def channel(x, m, tm):
            if l % 2 == 0:
                return _ffn(x, n2_3, m[4], m[3], m[5], f1, f3, f2, l, li, tm)
            return _moe(x, n2_3, m[4], m[3], m[5], wr, e1, e3, e2, l, li, tm)

        P, G = _inproj(xp, n1_3, mp[1], mp[0], w_main, w_glr, l, 1024 if T % 1024 == 0 else MOBA_BLOCK)
        P3, G3 = P.reshape(1, T, N_MAIN), G.reshape(1, T, 128)
        oa, sg = _gla(P3, G3, w2p, gkb3, gla_nw3, zero_gla, l, CHUNK, CHUNK)
        ob, sr = _ret(P3, cos_p, sin_p, ret_nw3, zero_ret, l, CHUNK, CHUNK)
        qn, kn, kmean, kp, vt = _prompt_prep(P, qw3, kw3, l)
        qpt = _gate(qn, _placed_block_means(kmean.reshape(nb, MOBA_W)).T)
        oc = _prompt_attention(qpt, kp, vt, btab)
        xm = _merge(oa.reshape(T, V_W), ob.reshape(T, V_W), oc, P, xp, mp[2], wa, wb, wc, wo, l, 512 if T % 512 == 0 else MOBA_BLOCK)
        xp = channel(xm, mp, 512 if T % 512 == 0 else MOBA_BLOCK)
        outs["kp"].append(kn.reshape(bp, T, MOBA_HEADS, MOBA_HD))
        outs["vp"].append(P[:, COL_MV * MOBA_W:(COL_MV + 1) * MOBA_W].reshape(bp, T, MOBA_HEADS, MOBA_HD))
        outs["gp"].append(sg)
        outs["rp"].append(sr)

        Ps, Gs = _inproj(xs, n1_3, msm[1], msm[0], w_main, w_glr, l, ms_rows)
        Ps3, Gs3 = Ps.reshape(nseq, nd, N_MAIN), Gs.reshape(nseq, nd, 128)
        oas, sgs = _gla(Ps3, Gs3, w2p, gkb3, gla_nw3, state_gla[l], l, CHUNK, nd)
        obs, srs = _ret(Ps3, cos_s, sin_s, ret_nw3, state_ret[l], l, CHUNK, nd)
        qns, kns = _qknorm(Ps, qw3, kw3, l, ms_rows)
        vns = Ps[:, COL_MV * MOBA_W:(COL_MV + 1) * MOBA_W]
        qs4 = (qns * (MOBA_HD ** -0.5)).reshape(nseq, nd, MOBA_HEADS, MOBA_HD).transpose(0, 2, 1, 3)
        q_t = jnp.pad(qs4.transpose(0, 1, 3, 2), ((0, 0), (0, 0), (0, 0), (0, 128 - nd)))
        top = _decode_select(_decode_gates(page_table, ck_t, q_t, l))
        sel = top[:, :, :, :nd].transpose(0, 2, 3, 1).reshape(nseq, nq * MOBA_TOPK)
        head_major = lambda a: jnp.pad(a.reshape(nseq, nd, MOBA_HEADS, MOBA_HD).transpose(0, 2, 1, 3),
                                       ((0, 0), (0, 0), (0, 8 - nd), (0, 0)))
        ocs4 = _decode_attend(page_table, sel, ck_t, cv_t, qs4.reshape(nseq, nq, 1, MOBA_HD),
                              head_major(kns), head_major(vns), bt_past, bt_own, l, nd)
        ocs = ocs4.reshape(nseq, MOBA_HEADS, nd, MOBA_HD).transpose(0, 2, 1, 3)
        xms = _merge(oas.reshape(ms_rows, V_W), obs.reshape(ms_rows, V_W), ocs.reshape(ms_rows, MOBA_W).astype(BF16),
                     Ps, xs, msm[2], wa, wb, wc, wo, l, ms_rows)
        xs = channel(xms, msm, ms_rows)
        outs["ks"].append(kns.reshape(nseq, nd, MOBA_HEADS, MOBA_HD))
        outs["vs"].append(vns.reshape(nseq, nd, MOBA_HEADS, MOBA_HD))
        outs["gs"].append(sgs)
        outs["rs"].append(srs)

    st = lambda k: jnp.stack(outs[k])
    return (xp.reshape(bp, T, D_MODEL), xs.reshape(nseq, nd, D_MODEL), st("kp"), st("vp"), st("gp"), st("rp"),
            st("ks"), st("vs"), st("gs"), st("rs"))
```
